```python
import math
import numpy as np
import jax
import jax.numpy as jnp
from jax import lax

D_MODEL = 1024
BATCH = 8
SEQ = 4096
DEPTH = 2

PLE_DIM = 256
N_MIXERS = 2
N_NSA_LAYERS = (DEPTH + 1) // 2
N_RET_LAYERS = DEPTH // 2
RMS_EPS = 1e-6
GN_EPS = 1e-5

NSA_HEADS = 16
NSA_HEAD_DIM = D_MODEL // NSA_HEADS
NSA_KV_GROUPS = 4
NSA_HPG = NSA_HEADS // NSA_KV_GROUPS
CMP_BLOCK = 32
CMP_STRIDE = 16
CMP_HIDDEN = 256
SEL_BLOCK = 64
SEL_TOPN = 16
WINDOW = 512
Q_CHUNK = 32
FORCE_BONUS = 1e4
NEG = -1e30
NSA_Q_W = NSA_HEADS * NSA_HEAD_DIM
NSA_KV_W = NSA_KV_GROUPS * NSA_HEAD_DIM
NSA_GATE_COLS = 3 * NSA_HEADS
NSA_Z_W = NSA_HEADS * NSA_HEAD_DIM
NSA_SIZES = [NSA_Q_W] + [NSA_KV_W] * 6 + [NSA_GATE_COLS, NSA_Z_W]
NSA_IN_COLS = int(sum(NSA_SIZES))
NSA_SPLITS = [int(v) for v in np.cumsum(NSA_SIZES[:-1])]

RET_HEADS = 4
RET_KEY_DIM = D_MODEL // RET_HEADS
RET_VAL_DIM = 2 * RET_KEY_DIM
RET_CHUNK = 128
ROPE_BASE = 10000.0
RET_SIZES = [RET_HEADS * RET_KEY_DIM, RET_HEADS * RET_KEY_DIM,
             RET_HEADS * RET_VAL_DIM, RET_HEADS * RET_VAL_DIM]
RET_IN_COLS = int(sum(RET_SIZES))
RET_SPLITS = [int(v) for v in np.cumsum(RET_SIZES[:-1])]

kernel_name = "nsa_retention_interleaved_hybrid"


def rms_norm(x, g):
    xf = x.astype(jnp.float32)
    y = xf * lax.rsqrt(jnp.mean(xf * xf, axis=-1, keepdims=True) + RMS_EPS)
    return (y * g.astype(jnp.float32)).astype(x.dtype)


def masked_softmax(s, mask):
    s32 = jnp.where(mask, s.astype(jnp.float32), NEG)
    p = jax.nn.softmax(s32, axis=-1)
    return jnp.where(mask, p, 0.0)


def cmp_sel_overlap(n_cmp, n_sel):
    i = np.arange(n_cmp)[:, None]
    j = np.arange(n_sel)[None, :]
    c_start = i * CMP_STRIDE
    c_end = c_start + CMP_BLOCK
    ov = (c_start < (j + 1) * SEL_BLOCK) & (c_end > j * SEL_BLOCK)
    return ov.astype(np.float32)


def compress(k, pos, w1, w2):
    b, s, g, d = k.shape
    n_sub_tot = s // CMP_STRIDE
    n_sub = CMP_BLOCK // CMP_STRIDE
    sub = k.reshape(b, n_sub_tot, CMP_STRIDE, g, d)
    n_cmp = n_sub_tot - n_sub + 1
    blocks = jnp.concatenate([sub[:, j:j + n_cmp] for j in range(n_sub)], axis=2)
    blocks = blocks + pos[None, None, :, None, :]
    flat = blocks.transpose(0, 1, 3, 2, 4).reshape(b, n_cmp, g, CMP_BLOCK * d)
    return jax.nn.silu(flat @ w1) @ w2


def nsa_mixer(h, w_in, q_g, kc_g, ks_g, kw_g, pos_k, pos_v, ck_w1, ck_w2, cv_w1, cv_w2, w_out):
    b, s, _ = h.shape
    G, HPG, dk = NSA_KV_GROUPS, NSA_HPG, NSA_HEAD_DIM
    proj = h @ w_in
    q, kc, vc, ks, vs, kw, vw, gl, z = jnp.split(proj, NSA_SPLITS, axis=-1)
    q = rms_norm(q.reshape(b, s, G, HPG, dk), q_g) * (dk ** -0.5)
    kvs = (b, s, G, dk)
    kc = rms_norm(compress(kc.reshape(kvs), pos_k, ck_w1, ck_w2), kc_g)
    vc = compress(vc.reshape(kvs), pos_v, cv_w1, cv_w2)
    ks = rms_norm(ks.reshape(kvs), ks_g)
    vs = vs.reshape(kvs)
    kw = rms_norm(kw.reshape(kvs), kw_g)
    vw = vw.reshape(kvs)
    gates = jax.nn.sigmoid(gl.reshape(b, s, G, HPG, 3))

    n_cmp = kc.shape[1]
    n_sel = s // SEL_BLOCK
    top_n = min(SEL_TOPN, n_sel)
    cmp_end = jnp.arange(n_cmp) * CMP_STRIDE + CMP_BLOCK - 1
    overlap = jnp.asarray(cmp_sel_overlap(n_cmp, n_sel))
    ks_blk = ks.reshape(b, n_sel, SEL_BLOCK, G, dk).transpose(0, 3, 1, 2, 4)
    vs_blk = vs.reshape(b, n_sel, SEL_BLOCK, G, dk).transpose(0, 3, 1, 2, 4)
    kw_pad = jnp.pad(kw, ((0, 0), (WINDOW, 0), (0, 0), (0, 0)))
    vw_pad = jnp.pad(vw, ((0, 0), (WINDOW, 0), (0, 0), (0, 0)))
    b_ix = jnp.arange(b)[:, None, None, None]
    g_ix = jnp.arange(G)[None, None, :, None]
    sel_off = jnp.arange(SEL_BLOCK)
    win_off = jnp.arange(WINDOW + Q_CHUNK)
    blk = jnp.arange(n_sel)
    n_keys_sel = top_n * SEL_BLOCK

    def chunk(c):
        t0 = c * Q_CHUNK
        t = t0 + jnp.arange(Q_CHUNK)
        qc = lax.dynamic_slice_in_dim(q, t0, Q_CHUNK, axis=1)
        gc = lax.dynamic_slice_in_dim(gates, t0, Q_CHUNK, axis=1)
        sc = jnp.einsum('btghd,bngd->btghn', qc, kc)
        mc = (cmp_end[None, :] <= t[:, None])[None, :, None, None, :]
        pc = masked_softmax(sc, mc)
        o_c = jnp.einsum('btghn,bngd->btghd', pc.astype(vc.dtype), vc)
        imp = jnp.einsum('btgn,nj->btgj', pc.sum(axis=3), overlap)
        cur = t // SEL_BLOCK
        valid = blk[None, :] <= cur[:, None]
        forced = (blk[None, :] == 0) | (blk[None, :] == cur[:, None]) | (blk[None, :] == cur[:, None] - 1)
        score = jnp.where(valid[None, :, None, :],
                          imp + FORCE_BONUS * forced[None, :, None, :].astype(jnp.float32), NEG)
        _, idx = lax.top_k(score, top_n)
        kg = ks_blk[b_ix, g_ix, idx].reshape(b, Q_CHUNK, G, n_keys_sel, dk)
        vg = vs_blk[b_ix, g_ix, idx].reshape(b, Q_CHUNK, G, n_keys_sel, dk)
        ss = jnp.einsum('btghd,btgmd->btghm', qc, kg)
        kpos = (idx[..., None] * SEL_BLOCK + sel_off).reshape(b, Q_CHUNK, G, n_keys_sel)
        ms = (kpos <= t[None, :, None, None])[:, :, :, None, :]
        ps = masked_softmax(ss, ms)
        o_s = jnp.einsum('btghm,btgmd->btghd', ps.astype(vg.dtype), vg)
        kwc = lax.dynamic_slice_in_dim(kw_pad, t0, WINDOW + Q_CHUNK, axis=1)
        vwc = lax.dynamic_slice_in_dim(vw_pad, t0, WINDOW + Q_CHUNK, axis=1)
        spos = t0 - WINDOW + win_off
        mw = (spos[None, :] >= 0) & (spos[None, :] <= t[:, None]) & (t[:, None] - spos[None, :] < WINDOW)
        sw = jnp.einsum('btghd,bsgd->btghs', qc, kwc)
        pw = masked_softmax(sw, mw[None, :, None, None, :])
        o_w = jnp.einsum('btghs,bsgd->btghd', pw.astype(vwc.dtype), vwc)
        return gc[..., 0:1] * o_c + gc[..., 1:2] * o_s + gc[..., 2:3] * o_w

    out = lax.map(chunk, jnp.arange(s // Q_CHUNK))
    out = out.transpose(1, 0, 2, 3, 4, 5).reshape(b, s, NSA_HEADS * dk)
    return (out * jax.nn.silu(z)) @ w_out


def rotary(x, pos):
    half = x.shape[-1] // 2
    inv = ROPE_BASE ** (-jnp.linspace(0.0, 1.0, half, dtype=jnp.float32))
    ang = pos[:, None] * inv[None, :]
    cos = jnp.cos(ang)[None, :, None, :].astype(x.dtype)
    sin = jnp.sin(ang)[None, :, None, :].astype(x.dtype)
    x1, x2 = x[..., :half], x[..., half:]
    return jnp.concatenate([x1 * cos - x2 * sin, x1 * sin + x2 * cos], axis=-1)


def retention_mixer(h, w_in, w_out):
    b, s, _ = h.shape
    H, dk, dv, C = RET_HEADS, RET_KEY_DIM, RET_VAL_DIM, RET_CHUNK
    proj = h @ w_in
    q, k, v, z = jnp.split(proj, RET_SPLITS, axis=-1)
    pos = jnp.arange(s, dtype=jnp.float32)
    q = rotary(q.reshape(b, s, H, dk), pos)
    k = rotary(k.reshape(b, s, H, dk), pos) * (dk ** -0.5)
    v = v.reshape(b, s, H, dv)
    n_c = s // C

    def to_chunks(a):
        return a.reshape(b, n_c, C, H, a.shape[-1]).transpose(1, 0, 3, 2, 4).astype(jnp.float32)

    log_g = jnp.log(1.0 - 2.0 ** (-5.0 - jnp.arange(H, dtype=jnp.float32)))
    ix = jnp.arange(C, dtype=jnp.float32)
    diff = ix[:, None] - ix[None, :]
    intra_decay = jnp.where(diff >= 0, jnp.exp(log_g[:, None, None] * jnp.maximum(diff, 0.0)), 0.0)
    q_decay = jnp.exp(log_g[:, None] * (ix + 1.0))
    k_decay = jnp.exp(log_g[:, None] * (C - 1.0 - ix))
    chunk_decay = jnp.exp(log_g * C)

    def step(state, qkv):
        qc, kc, vc = qkv
        att = jnp.einsum('bhid,bhjd->bhij', qc, kc) * intra_decay
        o = (jnp.einsum('bhij,bhje->bhie', att, vc)
             + jnp.einsum('bhid,bhde->bhie', qc, state) * q_decay[..., None])
        state = (state * chunk_decay[:, None, None]
                 + jnp.einsum('bhjd,bhje->bhde', kc * k_decay[..., None], vc))
        return state, o

    state0 = jnp.zeros((b, H, dk, dv), jnp.float32)
    _, o = lax.scan(step, state0, (to_chunks(q), to_chunks(k), to_chunks(v)))
    o = o.transpose(1, 0, 3, 2, 4).reshape(b, s, H, dv)
    mu = jnp.mean(o, axis=-1, keepdims=True)
    var = jnp.mean(jnp.square(o - mu), axis=-1, keepdims=True)
    o = ((o - mu) * lax.rsqrt(var + GN_EPS)).reshape(b, s, H * dv).astype(h.dtype)
    return (o * jax.nn.silu(z)) @ w_out


def setup_inputs(seed: int = 0) -> dict:
    key = jax.random.key(seed)
    ks = jax.random.split(key, 20)

    def nrm(k, shape, fan_in):
        return jax.random.normal(k, shape, jnp.float32) * (fan_in ** -0.5)

    def gain(k, shape):
        return 1.0 + 0.02 * jax.random.normal(k, shape, jnp.float32)

    NA, NB, dk = N_NSA_LAYERS, N_RET_LAYERS, NSA_HEAD_DIM
    return {
        "x": jax.random.normal(ks[0], (BATCH, SEQ, D_MODEL), jnp.float32),
        "p": jax.random.normal(ks[1], (DEPTH, BATCH, SEQ, PLE_DIM), jnp.float32),
        "norm_g": gain(ks[2], (DEPTH, D_MODEL)),
        "nsa_w_in": nrm(ks[3], (NA, D_MODEL, NSA_IN_COLS), D_MODEL),
        "nsa_q_g": gain(ks[4], (NA, dk)),
        "nsa_kc_g": gain(ks[5], (NA, dk)),
        "nsa_ks_g": gain(ks[6], (NA, dk)),
        "nsa_kw_g": gain(ks[7], (NA, dk)),
        "nsa_cmp_pos_k": 0.1 * jax.random.normal(ks[8], (NA, CMP_BLOCK, dk), jnp.float32),
        "nsa_cmp_pos_v": 0.1 * jax.random.normal(ks[9], (NA, CMP_BLOCK, dk), jnp.float32),
        "nsa_cmp_k_w1": nrm(ks[10], (NA, CMP_BLOCK * dk, CMP_HIDDEN), CMP_BLOCK * dk),
        "nsa_cmp_k_w2": nrm(ks[11], (NA, CMP_HIDDEN, dk), CMP_HIDDEN),
        "nsa_cmp_v_w1": nrm(ks[12], (NA, CMP_BLOCK * dk, CMP_HIDDEN), CMP_BLOCK * dk),
        "nsa_cmp_v_w2": nrm(ks[13], (NA, CMP_HIDDEN, dk), CMP_HIDDEN),
        "nsa_w_out": nrm(ks[14], (NA, NSA_Z_W, D_MODEL), NSA_Z_W),
        "ret_w_in": nrm(ks[15], (NB, D_MODEL, RET_IN_COLS), D_MODEL),
        "ret_w_out": nrm(ks[16], (NB, RET_HEADS * RET_VAL_DIM, D_MODEL), RET_HEADS * RET_VAL_DIM),
        "ple_w": nrm(ks[17], (DEPTH, PLE_DIM, D_MODEL), PLE_DIM),
        "ple_gate_w": nrm(ks[18], (DEPTH, D_MODEL, D_MODEL), D_MODEL),
    }


def reference(x, p, norm_g, nsa_w_in, nsa_q_g, nsa_kc_g, nsa_ks_g, nsa_kw_g,
              nsa_cmp_pos_k, nsa_cmp_pos_v, nsa_cmp_k_w1, nsa_cmp_k_w2,
              nsa_cmp_v_w1, nsa_cmp_v_w2, nsa_w_out, ret_w_in, ret_w_out,
              ple_w, ple_gate_w):
    for i in range(DEPTH):
        h = rms_norm(x, norm_g[i])
        j = i // N_MIXERS
        if i % N_MIXERS == 0:
            y = nsa_mixer(h, nsa_w_in[j], nsa_q_g[j], nsa_kc_g[j], nsa_ks_g[j], nsa_kw_g[j],
                          nsa_cmp_pos_k[j], nsa_cmp_pos_v[j], nsa_cmp_k_w1[j], nsa_cmp_k_w2[j],
                          nsa_cmp_v_w1[j], nsa_cmp_v_w2[j], nsa_w_out[j])
        else:
            y = retention_mixer(h, ret_w_in[j], ret_w_out[j])
        x = x + y
        x = x + jax.nn.sigmoid(x @ ple_gate_w[i]) * (p[i] @ ple_w[i])
    return x
```

```python
import functools

import numpy as np
import jax
import jax.numpy as jnp
from jax import lax
from jax.experimental import pallas as pl
from jax.experimental.pallas import tpu as pltpu

F32 = jnp.float32
BF16 = jnp.bfloat16

D_MODEL = 1024
PLE_DIM = 256
RMS_EPS = 1e-6
GN_EPS = 1e-5

NSA_HEADS = 16
NSA_DK = 64
NSA_G = 4
NSA_HPG = 4
CMP_BLOCK = 32
CMP_STRIDE = 16
CMP_HIDDEN = 256
SEL_BLOCK = 64
SEL_TOPN = 16
WINDOW = 512
FORCE_BONUS = 1e4
NEG = -1e30
SEL_MASK_BIAS = -30000.0

RET_HEADS = 4
RET_DK = 256
RET_DV = 512
RET_CHUNK = 128
ROPE_BASE = 10000.0

LANES = 128
VMEM_LIMIT = 56 * 1024 * 1024

ROW_TILE = 512
Q_TILE = 128
K_TILE = 256


def _dot(a, b):
    return jnp.dot(a, b, preferred_element_type=F32)


def _dot_nt(a, b):
    return lax.dot_general(a, b, (((1,), (1,)), ((), ())), preferred_element_type=F32)


def _silu(x):
    return x * jax.nn.sigmoid(x)


def _const_spec(shape):
    nd = len(shape)
    return pl.BlockSpec(shape, lambda *_: (0,) * nd)


def _proj_kernel(x_ref, g_ref, w_ref, hg_ref, bd_ref, cos_ref, sin_ref, *out_refs, plan):
    x = x_ref[...]
    h = x * lax.rsqrt(jnp.mean(x * x, axis=-1, keepdims=True) + RMS_EPS) * g_ref[...]
    h = h.astype(BF16)
    for (c0, width, kind, dst, d0, scale) in plan:
        acc = _dot(h, w_ref[:, c0:c0 + width])
        if kind == "headnorm":
            ss = _dot((acc * acc).astype(BF16), bd_ref[...])
            acc = acc * lax.rsqrt(ss * (1.0 / NSA_DK) + RMS_EPS) * hg_ref[:, c0:c0 + width]
        elif kind == "rot":
            half = width // 2
            a1, a2 = acc[:, :half], acc[:, half:]
            cos, sin = cos_ref[...], sin_ref[...]
            acc = jnp.concatenate([a1 * cos - a2 * sin, a1 * sin + a2 * cos], axis=-1)
        if scale != 1.0:
            acc = acc * scale
        o_ref = out_refs[dst]
        o_ref[:, d0:d0 + width] = acc.astype(o_ref.dtype)


def _proj(x2d, norm_g, w, head_gain, cos, sin, plan, out_defs, seq):
    n = x2d.shape[0]
    cols = w.shape[1]
    n_pos_blocks = seq // ROW_TILE
    bd = (np.arange(256)[:, None] // NSA_DK == np.arange(256)[None, :] // NSA_DK)
    bd = jnp.asarray(bd, BF16)
    out_shape = [jax.ShapeDtypeStruct((n, c), dt) for (c, dt) in out_defs]
    out_specs = [pl.BlockSpec((ROW_TILE, c), lambda i: (i, 0)) for (c, _) in out_defs]
    return pl.pallas_call(
        functools.partial(_proj_kernel, plan=plan),
        grid=(n // ROW_TILE,),
        in_specs=[
            pl.BlockSpec((ROW_TILE, D_MODEL), lambda i: (i, 0)),
            _const_spec((1, D_MODEL)),
            _const_spec((D_MODEL, cols)),
            _const_spec((1, cols)),
            _const_spec((256, 256)),
            pl.BlockSpec((ROW_TILE, LANES), lambda i: (i % n_pos_blocks, 0)),
            pl.BlockSpec((ROW_TILE, LANES), lambda i: (i % n_pos_blocks, 0)),
        ],
        out_specs=out_specs,
        out_shape=out_shape,
        compiler_params=pltpu.CompilerParams(
            dimension_semantics=("arbitrary",), vmem_limit_bytes=VMEM_LIMIT),
        name="norm_proj",
    )(x2d, norm_g.reshape(1, D_MODEL), w, head_gain, bd, cos, sin)


def _out_kernel(a_ref, wo_ref, x_ref, p_ref, wg_ref, wp_ref, o_ref):
    x1 = x_ref[...] + _dot(a_ref[...], wo_ref[...])
    gate = jax.nn.sigmoid(_dot(x1.astype(BF16), wg_ref[...]))
    emb = _dot(p_ref[...].astype(BF16), wp_ref[...])
    o_ref[...] = x1 + gate * emb


def _out_proj(a, w_out, x2d, p2d, w_gate, w_ple):
    n, k = a.shape
    return pl.pallas_call(
        _out_kernel,
        grid=(n // ROW_TILE,),
        in_specs=[
            pl.BlockSpec((ROW_TILE, k), lambda i: (i, 0)),
            _const_spec((k, D_MODEL)),
            pl.BlockSpec((ROW_TILE, D_MODEL), lambda i: (i, 0)),
            pl.BlockSpec((ROW_TILE, PLE_DIM), lambda i: (i, 0)),
            _const_spec((D_MODEL, D_MODEL)),
            _const_spec((PLE_DIM, D_MODEL)),
        ],
        out_specs=pl.BlockSpec((ROW_TILE, D_MODEL), lambda i: (i, 0)),
        out_shape=jax.ShapeDtypeStruct((n, D_MODEL), F32),
        compiler_params=pltpu.CompilerParams(
            dimension_semantics=("arbitrary",), vmem_limit_bytes=VMEM_LIMIT),
        name="out_proj_ple",
    )(a, w_out.astype(BF16), x2d, p2d, w_gate.astype(BF16), w_ple.astype(BF16))


def _cmp_kernel(xk_ref, xv_ref, w1k_ref, w1v_ref, posk_ref, posv_ref, w2k_ref, w2v_ref,
                gk_ref, kc_ref, vc_ref):
    n_sub = xk_ref.shape[2]
    half = xk_ref.shape[3]
    lane = lax.broadcasted_iota(jnp.int32, (n_sub, LANES), 1)
    for is_k, x_ref, w1_ref, pos_ref, w2_ref, o_ref in (
            (True, xk_ref, w1k_ref, posk_ref, w2k_ref, kc_ref),
            (False, xv_ref, w1v_ref, posv_ref, w2v_ref, vc_ref)):
        w1 = w1_ref[...]
        x = x_ref[0].reshape(NSA_G * n_sub, half)
        ab = _dot(x, w1)
        pos = pos_ref[...]
        pos_term = (_dot(pos[:, :half], w1[:, :CMP_HIDDEN])
                    + _dot(pos[:, half:], w1[:, CMP_HIDDEN:]))[0:1]
        for g in range(NSA_G):
            first = ab[g * n_sub:(g + 1) * n_sub, :CMP_HIDDEN]
            second = ab[g * n_sub:(g + 1) * n_sub, CMP_HIDDEN:]
            hid = _silu(first + pltpu.roll(second, n_sub - 1, 0) + pos_term)
            c = _dot(hid.astype(BF16), w2_ref[...])
            if is_k:
                ss = jnp.sum(c * c, axis=-1, keepdims=True) * (1.0 / NSA_DK)
                c = c * lax.rsqrt(ss + RMS_EPS) * gk_ref[...]
            else:
                c = jnp.where(lane < NSA_DK, c, 1.0)
            o_ref[0, g] = c.astype(BF16)


def _compress(xk, xv, w1k, w1v, posk, posv, w2k, w2v, gk):
    b, g, n_sub, half = xk.shape
    blk = pl.BlockSpec((1, g, n_sub, half), lambda i: (i, 0, 0, 0))
    out_blk = pl.BlockSpec((1, g, n_sub, LANES), lambda i: (i, 0, 0, 0))
    out_sds = jax.ShapeDtypeStruct((b, g, n_sub, LANES), BF16)
    return pl.pallas_call(
        _cmp_kernel,
        grid=(b,),
        in_specs=[blk, blk,
                  _const_spec(w1k.shape), _const_spec(w1v.shape),
                  _const_spec(posk.shape), _const_spec(posv.shape),
                  _const_spec(w2k.shape), _const_spec(w2v.shape),
                  _const_spec(gk.shape)],
        out_specs=[out_blk, out_blk],
        out_shape=[out_sds, out_sds],
        compiler_params=pltpu.CompilerParams(
            dimension_semantics=("arbitrary",), vmem_limit_bytes=VMEM_LIMIT),
        name="nsa_compress",
    )(xk, xv, w1k, w1v, posk, posv, w2k, w2v, gk)


def _nsa_attn_kernel(q_ref, gl_ref, z_ref, ks_ref, vs_ref, kw_ref, vw_ref, kc_ref, vc_ref,
                     ov_ref, eg_ref, o_ref):
    tq = Q_TILE
    rows = NSA_HPG * tq
    t0 = pl.program_id(2) * tq
    n_cmp = kc_ref.shape[2]
    n_sel = SEL_BLOCK

    lane = lax.broadcasted_iota(jnp.int32, (tq, LANES), 1)
    low = lane < NSA_DK

    qf = q_ref[...].astype(F32)
    pairs = (qf[:, :LANES], qf[:, LANES:])
    bases = (pairs[0], pltpu.roll(pairs[0], NSA_DK, 1), pairs[1], pltpu.roll(pairs[1], NSA_DK, 1))
    q_stack = jnp.concatenate(bases, axis=0).astype(BF16)

    t_q = t0 + lax.broadcasted_iota(jnp.int32, (tq, 1), 0)
    t_row = jnp.concatenate([t_q] * NSA_HPG, axis=0)

    sc = _dot_nt(q_stack, kc_ref[0, 0])
    cmp_end = lax.broadcasted_iota(jnp.int32, (rows, n_cmp), 1) * CMP_STRIDE + (CMP_BLOCK - 1)
    mc = cmp_end <= t_row
    sc = jnp.where(mc, sc, NEG)
    ec = jnp.where(mc, jnp.exp(sc - jnp.max(sc, axis=-1, keepdims=True)), 0.0)
    lc = jnp.sum(ec, axis=-1, keepdims=True)
    pc = ec / jnp.where(lc > 0.0, lc, 1.0)
    a_c = _dot(pc.astype(BF16), vc_ref[0, 0])

    ps = pc[0:tq] + pc[tq:2 * tq] + pc[2 * tq:3 * tq] + pc[3 * tq:4 * tq]
    ps_hi = ps.astype(BF16)
    ps_lo = (ps - ps_hi.astype(F32)).astype(BF16)
    imp = _dot(ps_hi, ov_ref[...]) + _dot(ps_lo, ov_ref[...])
    imp_t = imp.T[:n_sel]
    blk = lax.broadcasted_iota(jnp.int32, (n_sel, tq), 0)
    cur = (t0 + lax.broadcasted_iota(jnp.int32, (n_sel, tq), 1)) // SEL_BLOCK
    forced = (blk == 0) | (blk == cur) | (blk == cur - 1)
    score = jnp.where(blk <= cur, imp_t + jnp.where(forced, FORCE_BONUS, 0.0), NEG)
    sub = 8
    chunks = [score[c * sub:(c + 1) * sub] for c in range(n_sel // sub)]
    ranks = [jnp.zeros((sub, tq), F32) for _ in chunks]
    blk_sub = lax.broadcasted_iota(jnp.int32, (sub, tq), 0)
    for j in range(n_sel):
        row = jnp.broadcast_to(score[j:j + 1, :], (sub, tq))
        for c, chunk in enumerate(chunks):
            if c * sub > j:
                one = jnp.where(row >= chunk, 1.0, 0.0)
            elif c * sub + sub - 1 <= j:
                one = jnp.where(row > chunk, 1.0, 0.0)
            else:
                one = jnp.where(blk_sub > j - c * sub,
                                jnp.where(row >= chunk, 1.0, 0.0), jnp.where(row > chunk, 1.0, 0.0))
            ranks[c] = ranks[c] + one
    rank = jnp.concatenate(ranks, axis=0)
    bias_t = jnp.where(rank < float(SEL_TOPN), 0.0, SEL_MASK_BIAS)
    bias = jnp.concatenate([jnp.zeros((LANES - n_sel, tq), F32), bias_t], axis=0).T

    q_aug = jnp.concatenate([jnp.where(low, b, bias) for b in bases], axis=0).astype(BF16)

    def sel_step(kt, carry, diagonal):
        m, acc = carry
        start = pl.multiple_of(kt * K_TILE, K_TILE)
        s = _dot_nt(q_aug, ks_ref[0, 0, pl.ds(start, K_TILE), :])
        if diagonal:
            key = start + lax.broadcasted_iota(jnp.int32, (rows, K_TILE), 1)
            s = jnp.where(key <= t_row, s, NEG)
        m_new = jnp.maximum(m, jnp.max(s, axis=-1, keepdims=True))
        p = jnp.exp(s - m_new).astype(BF16)
        acc = jnp.exp(m - m_new) * acc + _dot(p, vs_ref[0, 0, pl.ds(start, K_TILE), :])
        return m_new, acc

    diag = t0 // K_TILE
    carry = (jnp.full((rows, 1), NEG, F32), jnp.zeros((rows, LANES), F32))
    carry = lax.fori_loop(0, diag, functools.partial(sel_step, diagonal=False), carry)
    _, a_s = sel_step(diag, carry, True)

    n_win = WINDOW + tq
    w_start = pl.multiple_of(t0, tq)
    sw = _dot_nt(q_aug, kw_ref[0, 0, pl.ds(w_start, n_win), :])
    key = t0 - WINDOW + lax.broadcasted_iota(jnp.int32, (rows, n_win), 1)
    mw = (key >= 0) & (key <= t_row) & (t_row - key < WINDOW)
    sw = jnp.where(mw, sw, NEG)
    pw = jnp.exp(sw - jnp.max(sw, axis=-1, keepdims=True)).astype(BF16)
    a_w = _dot(pw, vw_ref[0, 0, pl.ds(w_start, n_win), :])

    def widen(r):
        hs = [r[i * tq:(i + 1) * tq] for i in range(NSA_HPG)]
        p0 = jnp.where(low, hs[0], pltpu.roll(hs[1], NSA_DK, 1))
        p1 = jnp.where(low, hs[2], pltpu.roll(hs[3], NSA_DK, 1))
        return jnp.concatenate([p0, p1], axis=-1)

    o_c = widen(a_c)
    o_s = widen(a_s / pltpu.roll(a_s, NSA_DK, 1))
    o_w = widen(a_w / pltpu.roll(a_w, NSA_DK, 1))

    gates = jax.nn.sigmoid(gl_ref[...])
    g_hi = gates.astype(BF16)
    g_lo = (gates - g_hi.astype(F32)).astype(BF16)
    ge = _dot(g_hi, eg_ref[0]) + _dot(g_lo, eg_ref[0])
    w = NSA_HPG * NSA_DK
    out = ge[:, :w] * o_c + ge[:, w:2 * w] * o_s + ge[:, 2 * w:] * o_w
    o_ref[...] = (out * _silu(z_ref[...].astype(F32))).astype(o_ref.dtype)


def _nsa_attention(pm, pg, ks_aug, vs1, kw_pad, vw1, kc_pad, vc1, overlap, gate_expand,
                   batch, seq, z_col_block):
    n = batch * seq
    n_q = seq // Q_TILE
    gw = NSA_HPG * NSA_DK
    kv_spec = lambda arr: pl.BlockSpec((1, 1) + arr.shape[2:], lambda b, g, i: (b, g, 0, 0))
    return pl.pallas_call(
        _nsa_attn_kernel,
        grid=(batch, NSA_G, n_q),
        in_specs=[
            pl.BlockSpec((Q_TILE, gw), lambda b, g, i: (b * n_q + i, g)),
            pl.BlockSpec((Q_TILE, LANES), lambda b, g, i: (b * n_q + i, 0)),
            pl.BlockSpec((Q_TILE, gw), lambda b, g, i: (b * n_q + i, z_col_block + g)),
            kv_spec(ks_aug), kv_spec(vs1), kv_spec(kw_pad), kv_spec(vw1),
            kv_spec(kc_pad), kv_spec(vc1),
            _const_spec(overlap.shape),
            pl.BlockSpec((1,) + gate_expand.shape[1:], lambda b, g, i: (g, 0, 0)),
        ],
        out_specs=pl.BlockSpec((Q_TILE, gw), lambda b, g, i: (b * n_q + i, g)),
        out_shape=jax.ShapeDtypeStruct((n, NSA_HEADS * NSA_DK), BF16),
        compiler_params=pltpu.CompilerParams(
            dimension_semantics=("arbitrary", "arbitrary", "arbitrary"),
            vmem_limit_bytes=VMEM_LIMIT),
        name="nsa_attention",
    )(pm, pg, pm, ks_aug, vs1, kw_pad, vw1, kc_pad, vc1, overlap, gate_expand)


def _ret_kernel(q_ref, k_ref, v_ref, z_ref, dec_ref, qd_ref, kd_ref, cd_ref, o_ref, st_ref):
    @pl.when(pl.program_id(1) == 0)
    def _():
        st_ref[...] = jnp.zeros_like(st_ref)

    for h in range(RET_HEADS):
        q = q_ref[:, h * RET_DK:(h + 1) * RET_DK]
        k = k_ref[:, h * RET_DK:(h + 1) * RET_DK]
        v = v_ref[:, h * RET_DV:(h + 1) * RET_DV]
        att = _dot_nt(q, k) * dec_ref[h]
        state = st_ref[h]
        o = _dot(att.astype(BF16), v) + _dot(q, state.astype(BF16)) * qd_ref[h]
        kd_t = (k.astype(F32) * kd_ref[h]).T.astype(BF16)
        st_ref[h] = state * cd_ref[h] + _dot(kd_t, v)
        mu = jnp.mean(o, axis=-1, keepdims=True)
        d = o - mu
        var = jnp.mean(d * d, axis=-1, keepdims=True)
        z = z_ref[:, h * RET_DV:(h + 1) * RET_DV].astype(F32)
        o_ref[:, h * RET_DV:(h + 1) * RET_DV] = (d * lax.rsqrt(var + GN_EPS) * _silu(z)).astype(o_ref.dtype)


def _retention(pr, tables, batch, seq):
    n = batch * seq
    n_c = seq // RET_CHUNK
    dec, qd, kd, cd = tables
    qk_w = RET_HEADS * RET_DK
    v_w = RET_HEADS * RET_DV
    row = lambda b, c: b * n_c + c
    return pl.pallas_call(
        _ret_kernel,
        grid=(batch, n_c),
        in_specs=[
            pl.BlockSpec((RET_CHUNK, qk_w), lambda b, c: (row(b, c), 0)),
            pl.BlockSpec((RET_CHUNK, qk_w), lambda b, c: (row(b, c), 1)),
            pl.BlockSpec((RET_CHUNK, v_w), lambda b, c: (row(b, c), 1)),
            pl.BlockSpec((RET_CHUNK, v_w), lambda b, c: (row(b, c), 2)),
            _const_spec(dec.shape), _const_spec(qd.shape), _const_spec(kd.shape), _const_spec(cd.shape),
        ],
        out_specs=pl.BlockSpec((RET_CHUNK, v_w), lambda b, c: (row(b, c), 0)),
        out_shape=jax.ShapeDtypeStruct((n, v_w), BF16),
        scratch_shapes=[pltpu.VMEM((RET_HEADS, RET_DK, RET_DV), F32)],
        compiler_params=pltpu.CompilerParams(
            dimension_semantics=("arbitrary", "arbitrary"), vmem_limit_bytes=VMEM_LIMIT),
        name="retention",
    )(pr, pr, pr, pr, dec, qd, kd, cd)


def _overlap_matrix(n_cmp_pad):
    i = np.arange(n_cmp_pad)[:, None]
    j = np.arange(LANES)[None, :]
    ov = (i * CMP_STRIDE < (j + 1) * SEL_BLOCK) & (i * CMP_STRIDE + CMP_BLOCK > j * SEL_BLOCK)
    return jnp.asarray(ov, BF16)


def _gate_expand_matrix():
    e = np.zeros((NSA_G, LANES, 3 * NSA_HPG * NSA_DK), np.float32)
    for g in range(NSA_G):
        for h in range(NSA_HPG):
            for br in range(3):
                c0 = br * NSA_HPG * NSA_DK + h * NSA_DK
                e[g, (g * NSA_HPG + h) * 3 + br, c0:c0 + NSA_DK] = 1.0
    return jnp.asarray(e, BF16)


def _rotary_tables(seq):
    half = RET_DK // 2
    inv = ROPE_BASE ** (-jnp.linspace(0.0, 1.0, half, dtype=F32))
    ang = jnp.arange(seq, dtype=F32)[:, None] * inv[None, :]
    return jnp.cos(ang), jnp.sin(ang)


def _retention_tables():
    c = RET_CHUNK
    log_g = jnp.log(1.0 - 2.0 ** (-5.0 - jnp.arange(RET_HEADS, dtype=F32)))
    ix = jnp.arange(c, dtype=F32)
    diff = ix[:, None] - ix[None, :]
    dec = jnp.where(diff >= 0, jnp.exp(log_g[:, None, None] * jnp.maximum(diff, 0.0)), 0.0)
    qd = jnp.exp(log_g[:, None] * (ix + 1.0))[:, :, None]
    kd = jnp.exp(log_g[:, None] * (c - 1.0 - ix))[:, :, None]
    cd = jnp.broadcast_to(jnp.exp(log_g * c)[:, None, None], (RET_HEADS, 1, RET_DV))
    return dec, qd, kd, cd


def _nsa_layer(x2d, p2d, batch, seq, norm_g, w_in, q_g, kc_g, ks_g, kw_g, pos_k, pos_v,
               ck_w1, ck_w2, cv_w1, cv_w2, w_out, ple_w, ple_gate_w):
    n = batch * seq
    qw = NSA_HEADS * NSA_DK
    kvw = NSA_G * NSA_DK
    sizes = [qw] + [kvw] * 6 + [3 * NSA_HEADS, qw]
    offs = np.concatenate([[0], np.cumsum(sizes)])
    col = lambda i: w_in[:, offs[i]:offs[i + 1]]
    wq, wkc, wvc, wks, wvs, wkw, wvw, wgl, wz = [col(i) for i in range(9)]
    wgl = jnp.pad(wgl, ((0, 0), (0, LANES - 3 * NSA_HEADS)))
    w = jnp.concatenate([wq, wks, wkw, wvs, wvw, wz, wkc, wvc, wgl], axis=1).astype(BF16)
    head_gain = jnp.concatenate([
        jnp.tile(q_g, NSA_HEADS), jnp.tile(ks_g, NSA_G), jnp.tile(kw_g, NSA_G),
        jnp.zeros((w.shape[1] - qw - 2 * kvw,), F32)]).reshape(1, -1)
    plan = []
    c = 0
    for _ in range(qw // 256):
        plan.append((c, 256, "headnorm", 0, c, NSA_DK ** -0.5)); c += 256
    for _ in range(2):
        plan.append((c, 256, "headnorm", 0, c, 1.0)); c += 256
    for _ in range((2 * kvw + qw + 2 * kvw) // 256):
        plan.append((c, 256, "raw", 0, c, 1.0)); c += 256
    main_cols = c
    plan.append((c, LANES, "raw", 1, 0, 1.0))
    dummy = jnp.zeros((seq, LANES), F32)
    pm, pg = _proj(x2d, norm_g, w, head_gain, dummy, dummy, tuple(plan),
                   [(main_cols, BF16), (LANES, F32)], seq)

    o_ks, o_kw, o_vs, o_vw = qw, qw + kvw, qw + 2 * kvw, qw + 3 * kvw
    o_z = qw + 4 * kvw
    o_kc, o_vc = o_z + qw, o_z + qw + kvw

    def grouped(c0, length):
        a = pm[:, c0:c0 + kvw].reshape(batch, seq, NSA_G, NSA_DK)
        return a.transpose(0, 2, 1, 3)

    n_sel = seq // SEL_BLOCK
    onehot = (np.arange(seq)[:, None] // SEL_BLOCK == np.arange(NSA_DK)[None, :])
    onehot = jnp.broadcast_to(jnp.asarray(onehot, BF16), (batch, NSA_G, seq, NSA_DK))
    ones = jnp.ones((batch, NSA_G, seq, NSA_DK), BF16)
    zeros = jnp.zeros((batch, NSA_G, seq, NSA_DK), BF16)
    front = ((0, 0), (0, 0), (WINDOW, 0), (0, 0))
    ks_aug = jnp.concatenate([grouped(o_ks, seq), onehot], axis=-1)
    vs1 = jnp.concatenate([grouped(o_vs, seq), ones], axis=-1)
    kw_pad = jnp.pad(jnp.concatenate([grouped(o_kw, seq), zeros], axis=-1), front)
    vw1 = jnp.pad(jnp.concatenate([grouped(o_vw, seq), ones], axis=-1), front)

    n_sub = seq // CMP_STRIDE

    def sub_blocks(c0):
        a = pm[:, c0:c0 + kvw].reshape(batch, n_sub, CMP_STRIDE, NSA_G, NSA_DK)
        return a.transpose(0, 3, 1, 2, 4).reshape(batch, NSA_G, n_sub, CMP_STRIDE * NSA_DK)

    half = CMP_STRIDE * NSA_DK
    w1cat = lambda w1: jnp.concatenate([w1[:half], w1[half:]], axis=1).astype(BF16)
    pos_flat = lambda pos: jnp.pad(pos.reshape(1, CMP_BLOCK * NSA_DK), ((0, 7), (0, 0))).astype(BF16)
    w2pad = lambda w2: jnp.pad(w2, ((0, 0), (0, LANES - NSA_DK))).astype(BF16)
    gk = jnp.pad(kc_g, (0, LANES - NSA_DK)).reshape(1, LANES)
    kc_pad, vc1 = _compress(sub_blocks(o_kc), sub_blocks(o_vc), w1cat(ck_w1), w1cat(cv_w1),
                            pos_flat(pos_k), pos_flat(pos_v), w2pad(ck_w2), w2pad(cv_w2), gk)

    a = _nsa_attention(pm, pg, ks_aug, vs1, kw_pad, vw1, kc_pad, vc1,
                       _overlap_matrix(n_sub), _gate_expand_matrix(), batch, seq, o_z // 256)
    return _out_proj(a, w_out, x2d, p2d, ple_gate_w, ple_w)


def _ret_layer(x2d, p2d, batch, seq, norm_g, w_in, w_out, ple_w, ple_gate_w):
    qk_w = RET_HEADS * RET_DK
    v_w = RET_HEADS * RET_DV
    cols = 2 * qk_w + 2 * v_w
    plan = []
    c = 0
    for _ in range(RET_HEADS):
        plan.append((c, RET_DK, "rot", 0, c, 1.0)); c += RET_DK
    for _ in range(RET_HEADS):
        plan.append((c, RET_DK, "rot", 0, c, RET_DK ** -0.5)); c += RET_DK
    while c < cols:
        plan.append((c, 512, "raw", 0, c, 1.0)); c += 512
    cos, sin = _rotary_tables(seq)
    (pr,) = _proj(x2d, norm_g, w_in.astype(BF16), jnp.zeros((1, cols), F32), cos, sin,
                  tuple(plan), [(cols, BF16)], seq)
    a = _retention(pr, _retention_tables(), batch, seq)
    return _out_proj(a, w_out, x2d, p2d, ple_gate_w, ple_w)


def kernel(x, p, norm_g, nsa_w_in, nsa_q_g, nsa_kc_g, nsa_ks_g, nsa_kw_g, nsa_cmp_pos_k, nsa_cmp_pos_v, nsa_cmp_k_w1, nsa_cmp_k_w2, nsa_cmp_v_w1, nsa_cmp_v_w2, nsa_w_out, ret_w_in, ret_w_out, ple_w, ple_gate_w):
    batch, seq, d_model = x.shape
    depth = p.shape[0]
    n = batch * seq
    x2d = x.reshape(n, d_model)
    for i in range(depth):
        p2d = p[i].reshape(n, PLE_DIM)
        j = i // 2
        if i % 2 == 0:
            x2d = _nsa_layer(x2d, p2d, batch, seq, norm_g[i], nsa_w_in[j], nsa_q_g[j], nsa_kc_g[j],
                             nsa_ks_g[j], nsa_kw_g[j], nsa_cmp_pos_k[j], nsa_cmp_pos_v[j],
                             nsa_cmp_k_w1[j], nsa_cmp_k_w2[j], nsa_cmp_v_w1[j], nsa_cmp_v_w2[j],
                             nsa_w_out[j], ple_w[i], ple_gate_w[i])
        else:
            x2d = _ret_layer(x2d, p2d, batch, seq, norm_g[i], ret_w_in[j], ret_w_out[j],
                             ple_w[i], ple_gate_w[i])
    return x2d.reshape(batch, seq, d_model)
```

```python
import functools

import numpy as np
import jax
import jax.numpy as jnp
from jax import lax
from jax.experimental import pallas as pl
from jax.experimental.pallas import tpu as pltpu

F32 = jnp.float32
BF16 = jnp.bfloat16

D_MODEL = 1024
PLE_DIM = 256
RMS_EPS = 1e-6
GN_EPS = 1e-5

NSA_HEADS = 16
NSA_DK = 64
NSA_G = 4
NSA_HPG = 4
CMP_BLOCK = 32
CMP_STRIDE = 16
CMP_HIDDEN = 256
SEL_BLOCK = 64
SEL_TOPN = 16
WINDOW = 512
FORCE_BONUS = 1e4
NEG = -1e30
SEL_MASK_BIAS = -30000.0

RET_HEADS = 4
RET_DK = 256
RET_DV = 512
RET_CHUNK = 128
ROPE_BASE = 10000.0

LANES = 128
VMEM_LIMIT = 56 * 1024 * 1024

ROW_TILE = 512
Q_TILE = 128
SEL_TILES = 4
V_ROWS = NSA_DK + 16


def _dot(a, b):
    return jnp.dot(a, b, preferred_element_type=F32)


def _dot_nt(a, b):
    return lax.dot_general(a, b, (((1,), (1,)), ((), ())), preferred_element_type=F32)


def _silu(x):
    return x * jax.nn.sigmoid(x)


def _const_spec(shape):
    nd = len(shape)
    return pl.BlockSpec(shape, lambda *_: (0,) * nd)


def _proj_kernel(x_ref, g_ref, w_ref, hg_ref, bd_ref, cos_ref, sin_ref, *out_refs, plan):
    x = x_ref[...]
    h = x * lax.rsqrt(jnp.mean(x * x, axis=-1, keepdims=True) + RMS_EPS) * g_ref[...]
    h = h.astype(BF16)
    for (c0, width, kind, dst, d0, scale) in plan:
        acc = _dot(h, w_ref[:, c0:c0 + width])
        if kind == "headnorm":
            ss = _dot((acc * acc).astype(BF16), bd_ref[...])
            acc = acc * lax.rsqrt(ss * (1.0 / NSA_DK) + RMS_EPS) * hg_ref[:, c0:c0 + width]
        elif kind == "rot":
            half = width // 2
            a1, a2 = acc[:, :half], acc[:, half:]
            cos, sin = cos_ref[...], sin_ref[...]
            acc = jnp.concatenate([a1 * cos - a2 * sin, a1 * sin + a2 * cos], axis=-1)
        if scale != 1.0:
            acc = acc * scale
        o_ref = out_refs[dst]
        o_ref[:, d0:d0 + width] = acc.astype(o_ref.dtype)


def _proj(x2d, norm_g, w, head_gain, cos, sin, plan, out_defs, seq):
    n = x2d.shape[0]
    cols = w.shape[1]
    n_pos_blocks = seq // ROW_TILE
    bd = (np.arange(256)[:, None] // NSA_DK == np.arange(256)[None, :] // NSA_DK)
    bd = jnp.asarray(bd, BF16)
    out_shape = [jax.ShapeDtypeStruct((n, c), dt) for (c, dt) in out_defs]
    out_specs = [pl.BlockSpec((ROW_TILE, c), lambda i: (i, 0)) for (c, _) in out_defs]
    return pl.pallas_call(
        functools.partial(_proj_kernel, plan=plan),
        grid=(n // ROW_TILE,),
        in_specs=[
            pl.BlockSpec((ROW_TILE, D_MODEL), lambda i: (i, 0)),
            _const_spec((1, D_MODEL)),
            _const_spec((D_MODEL, cols)),
            _const_spec((1, cols)),
            _const_spec((256, 256)),
            pl.BlockSpec((ROW_TILE, LANES), lambda i: (i % n_pos_blocks, 0)),
            pl.BlockSpec((ROW_TILE, LANES), lambda i: (i % n_pos_blocks, 0)),
        ],
        out_specs=out_specs,
        out_shape=out_shape,
        compiler_params=pltpu.CompilerParams(
            dimension_semantics=("arbitrary",), vmem_limit_bytes=VMEM_LIMIT),
        name="norm_proj",
    )(x2d, norm_g.reshape(1, D_MODEL), w, head_gain, bd, cos, sin)


def _out_kernel(a_ref, wo_ref, x_ref, p_ref, wg_ref, wp_ref, o_ref):
    x1 = x_ref[...] + _dot(a_ref[...], wo_ref[...])
    gate = jax.nn.sigmoid(_dot(x1.astype(BF16), wg_ref[...]))
    emb = _dot(p_ref[...].astype(BF16), wp_ref[...])
    o_ref[...] = x1 + gate * emb


def _out_proj(a, w_out, x2d, p2d, w_gate, w_ple):
    n, k = a.shape
    return pl.pallas_call(
        _out_kernel,
        grid=(n // ROW_TILE,),
        in_specs=[
            pl.BlockSpec((ROW_TILE, k), lambda i: (i, 0)),
            _const_spec((k, D_MODEL)),
            pl.BlockSpec((ROW_TILE, D_MODEL), lambda i: (i, 0)),
            pl.BlockSpec((ROW_TILE, PLE_DIM), lambda i: (i, 0)),
            _const_spec((D_MODEL, D_MODEL)),
            _const_spec((PLE_DIM, D_MODEL)),
        ],
        out_specs=pl.BlockSpec((ROW_TILE, D_MODEL), lambda i: (i, 0)),
        out_shape=jax.ShapeDtypeStruct((n, D_MODEL), F32),
        compiler_params=pltpu.CompilerParams(
            dimension_semantics=("arbitrary",), vmem_limit_bytes=VMEM_LIMIT),
        name="out_proj_ple",
    )(a, w_out.astype(BF16), x2d, p2d, w_gate.astype(BF16), w_ple.astype(BF16))


def _cmp_kernel(xk_ref, xv_ref, w1k_ref, w1v_ref, posk_ref, posv_ref, w2k_ref, w2v_ref,
                gk_ref, kc_ref, vc_ref):
    n_sub = xk_ref.shape[2]
    half = xk_ref.shape[3]
    for is_k, x_ref, w1_ref, pos_ref, w2_ref, o_ref in (
            (True, xk_ref, w1k_ref, posk_ref, w2k_ref, kc_ref),
            (False, xv_ref, w1v_ref, posv_ref, w2v_ref, vc_ref)):
        w1 = w1_ref[...]
        x = x_ref[0].reshape(NSA_G * n_sub, half)
        ab = _dot(x, w1)
        pos = pos_ref[...]
        pos_term = (_dot(pos[:, :half], w1[:, :CMP_HIDDEN])
                    + _dot(pos[:, half:], w1[:, CMP_HIDDEN:]))[0:1]
        for g in range(NSA_G):
            first = ab[g * n_sub:(g + 1) * n_sub, :CMP_HIDDEN]
            second = ab[g * n_sub:(g + 1) * n_sub, CMP_HIDDEN:]
            hid = _silu(first + pltpu.roll(second, n_sub - 1, 0) + pos_term)
            c = _dot(hid.astype(BF16), w2_ref[...])
            if is_k:
                ss = jnp.sum(c * c, axis=-1, keepdims=True) * (1.0 / NSA_DK)
                c = c * lax.rsqrt(ss + RMS_EPS) * gk_ref[...]
                o_ref[0, g] = c.astype(BF16)
            else:
                ct = jnp.concatenate([c.T[:NSA_DK], jnp.ones((V_ROWS - NSA_DK, n_sub), F32)], axis=0)
                o_ref[0, g] = ct.astype(BF16)


def _compress(xk, xv, w1k, w1v, posk, posv, w2k, w2v, gk):
    b, g, n_sub, half = xk.shape
    blk = pl.BlockSpec((1, g, n_sub, half), lambda i: (i, 0, 0, 0))
    k_blk = pl.BlockSpec((1, g, n_sub, LANES), lambda i: (i, 0, 0, 0))
    k_sds = jax.ShapeDtypeStruct((b, g, n_sub, LANES), BF16)
    v_blk = pl.BlockSpec((1, g, V_ROWS, n_sub), lambda i: (i, 0, 0, 0))
    v_sds = jax.ShapeDtypeStruct((b, g, V_ROWS, n_sub), BF16)
    return pl.pallas_call(
        _cmp_kernel,
        grid=(b,),
        in_specs=[blk, blk,
                  _const_spec(w1k.shape), _const_spec(w1v.shape),
                  _const_spec(posk.shape), _const_spec(posv.shape),
                  _const_spec(w2k.shape), _const_spec(w2v.shape),
                  _const_spec(gk.shape)],
        out_specs=[k_blk, v_blk],
        out_shape=[k_sds, v_sds],
        compiler_params=pltpu.CompilerParams(
            dimension_semantics=("arbitrary",), vmem_limit_bytes=VMEM_LIMIT),
        name="nsa_compress",
    )(xk, xv, w1k, w1v, posk, posv, w2k, w2v, gk)


def _nsa_attn_kernel(q_ref, glt_ref, z_ref, ks_ref, vst_ref, kw_ref, vwt_ref, kc_ref, vct_ref,
                     ovt_ref, o_ref, acc_ref):
    tq = Q_TILE
    cols = NSA_HPG * tq
    qi = pl.program_id(2)
    t0 = qi * tq
    n_cmp = kc_ref.shape[2]
    n_sel = SEL_BLOCK
    dk = NSA_DK

    qt = q_ref[...].astype(F32).T
    q_heads = [qt[h * dk:(h + 1) * dk] for h in range(NSA_HPG)]

    def stack_heads(extra_rows):
        return jnp.concatenate(
            [jnp.concatenate([qh, extra_rows], axis=0) for qh in q_heads], axis=1).astype(BF16)

    def lane_query(n_rows):
        return lax.broadcasted_iota(jnp.int32, (n_rows, cols), 1) & (tq - 1)

    sc = _dot(kc_ref[0, 0], stack_heads(jnp.zeros((dk, tq), F32)))
    cmp_end = lax.broadcasted_iota(jnp.int32, (n_cmp, cols), 0) * CMP_STRIDE + (CMP_BLOCK - 1)
    mc = cmp_end <= t0 + lane_query(n_cmp)
    sc = jnp.where(mc, sc, NEG)
    ec = jnp.where(mc, jnp.exp(sc - jnp.max(sc, axis=0, keepdims=True)), 0.0)
    lc = jnp.sum(ec, axis=0, keepdims=True)
    pc = ec * (1.0 / jnp.where(lc > 0.0, lc, 1.0))
    a_c = _dot(vct_ref[0, 0], pc.astype(BF16))

    ps = pc[:, 0:tq] + pc[:, tq:2 * tq] + pc[:, 2 * tq:3 * tq] + pc[:, 3 * tq:4 * tq]
    ps_hi = ps.astype(BF16)
    ps_lo = (ps - ps_hi.astype(F32)).astype(BF16)
    imp_t = _dot(ovt_ref[...], ps_hi) + _dot(ovt_ref[...], ps_lo)
    blk = lax.broadcasted_iota(jnp.int32, (n_sel, tq), 0)
    cur = (t0 + lax.broadcasted_iota(jnp.int32, (n_sel, tq), 1)) // SEL_BLOCK
    forced = (blk == 0) | (blk == cur) | (blk == cur - 1)
    score = jnp.where(blk <= cur, imp_t + jnp.where(forced, FORCE_BONUS, 0.0), NEG)
    sub = 8
    chunks = [score[c * sub:(c + 1) * sub] for c in range(n_sel // sub)]
    ranks = [jnp.zeros((sub, tq), F32) for _ in chunks]
    blk_sub = lax.broadcasted_iota(jnp.int32, (sub, tq), 0)
    for j in range(n_sel):
        row = jnp.broadcast_to(score[j:j + 1, :], (sub, tq))
        for c, chunk in enumerate(chunks):
            if c * sub > j:
                one = jnp.where(row >= chunk, 1.0, 0.0)
            elif c * sub + sub - 1 <= j:
                one = jnp.where(row > chunk, 1.0, 0.0)
            else:
                one = jnp.where(blk_sub > j - c * sub,
                                jnp.where(row >= chunk, 1.0, 0.0), jnp.where(row > chunk, 1.0, 0.0))
            ranks[c] = ranks[c] + one
    rank = jnp.concatenate(ranks, axis=0)
    bias_t = jnp.where(rank < float(SEL_TOPN), 0.0, SEL_MASK_BIAS)
    q_aug = stack_heads(jnp.where(blk == 0, 1.0, bias_t))

    diag_keep = lax.broadcasted_iota(jnp.int32, (tq, cols), 0) <= lane_query(tq)

    acc_ref[...] = jnp.zeros_like(acc_ref)

    def sel_tiles(first_tile, n_tiles, m, diagonal):
        start = pl.multiple_of(first_tile * tq, tq)
        s = _dot(ks_ref[0, 0, pl.ds(start, n_tiles * tq), :], q_aug)
        if diagonal:
            s = jnp.where(diag_keep, s, NEG)
        m_new = jnp.maximum(m, jnp.max(s, axis=0, keepdims=True))
        p = jnp.exp(s - m_new).astype(BF16)
        vt = vst_ref[0, 0, pl.ds(first_tile, n_tiles)]
        vt = jnp.concatenate([vt[i] for i in range(n_tiles)], axis=1)
        acc_ref[...] = jnp.exp(m - m_new) * acc_ref[...] + _dot(vt, p)
        return m_new

    big = SEL_TILES
    n_big = qi // big
    m = jnp.full((1, cols), NEG, F32)
    m = lax.fori_loop(0, n_big, lambda i, m: sel_tiles(i * big, big, m, False), m)
    m = lax.fori_loop(0, qi - n_big * big, lambda i, m: sel_tiles(n_big * big + i, 1, m, False), m)
    sel_tiles(qi, 1, m, True)
    a_s = acc_ref[...]

    n_wt = WINDOW // tq + 1
    sw = _dot(kw_ref[0, 0, pl.ds(pl.multiple_of(t0, tq), n_wt * tq), :], q_aug)
    sw = jnp.concatenate([jnp.where(diag_keep, NEG, sw[:tq]),
                          sw[tq:WINDOW],
                          jnp.where(diag_keep, sw[WINDOW:], NEG)], axis=0)
    pw = jnp.exp(sw - jnp.max(sw, axis=0, keepdims=True)).astype(BF16)
    vwt = vwt_ref[0, 0, pl.ds(qi, n_wt)]
    a_w = _dot(jnp.concatenate([vwt[i] for i in range(n_wt)], axis=1), pw)

    a_s = a_s[:dk] * (1.0 / a_s[dk:dk + 1])
    a_w = a_w[:dk] * (1.0 / a_w[dk:dk + 1])
    gates = jax.nn.sigmoid(glt_ref[0, 0])
    outs = []
    for h in range(NSA_HPG):
        sl = slice(h * tq, (h + 1) * tq)
        outs.append(gates[3 * h:3 * h + 1] * a_c[:dk, sl] + gates[3 * h + 1:3 * h + 2] * a_s[:, sl]
                    + gates[3 * h + 2:3 * h + 3] * a_w[:, sl])
    out = jnp.concatenate(outs, axis=0).T
    o_ref[...] = (out * _silu(z_ref[...].astype(F32))).astype(o_ref.dtype)


def _nsa_attention(pm, glt, ks_aug, vst, kw_pad, vwt, kc_pad, vct, overlap_t, batch, seq, z_col_block):
    n = batch * seq
    n_q = seq // Q_TILE
    gw = NSA_HPG * NSA_DK

    def per_group(arr):
        nd = arr.ndim - 2
        return pl.BlockSpec((1, 1) + arr.shape[2:], lambda b, g, i: (b, g) + (0,) * nd)

    return pl.pallas_call(
        _nsa_attn_kernel,
        grid=(batch, NSA_G, n_q),
        in_specs=[
            pl.BlockSpec((Q_TILE, gw), lambda b, g, i: (b * n_q + i, g)),
            pl.BlockSpec((1, 1, glt.shape[2], Q_TILE), lambda b, g, i: (b, g, 0, i)),
            pl.BlockSpec((Q_TILE, gw), lambda b, g, i: (b * n_q + i, z_col_block + g)),
            per_group(ks_aug), per_group(vst), per_group(kw_pad), per_group(vwt),
            per_group(kc_pad), per_group(vct),
            _const_spec(overlap_t.shape),
        ],
        out_specs=pl.BlockSpec((Q_TILE, gw), lambda b, g, i: (b * n_q + i, g)),
        out_shape=jax.ShapeDtypeStruct((n, NSA_HEADS * NSA_DK), BF16),
        scratch_shapes=[pltpu.VMEM((V_ROWS, NSA_HPG * Q_TILE), F32)],
        compiler_params=pltpu.CompilerParams(
            dimension_semantics=("arbitrary", "arbitrary", "arbitrary"),
            vmem_limit_bytes=VMEM_LIMIT),
        name="nsa_attention",
    )(pm, glt, pm, ks_aug, vst, kw_pad, vwt, kc_pad, vct, overlap_t)


def _ret_kernel(q_ref, k_ref, v_ref, z_ref, dec_ref, qd_ref, kd_ref, cd_ref, o_ref, st_ref):
    @pl.when(pl.program_id(1) == 0)
    def _():
        st_ref[...] = jnp.zeros_like(st_ref)

    for h in range(RET_HEADS):
        q = q_ref[:, h * RET_DK:(h + 1) * RET_DK]
        k = k_ref[:, h * RET_DK:(h + 1) * RET_DK]
        v = v_ref[:, h * RET_DV:(h + 1) * RET_DV]
        att = _dot_nt(q, k) * dec_ref[h]
        state = st_ref[h]
        o = _dot(att.astype(BF16), v) + _dot(q, state.astype(BF16)) * qd_ref[h]
        kd_t = (k.astype(F32) * kd_ref[h]).T.astype(BF16)
        st_ref[h] = state * cd_ref[h] + _dot(kd_t, v)
        mu = jnp.mean(o, axis=-1, keepdims=True)
        d = o - mu
        var = jnp.mean(d * d, axis=-1, keepdims=True)
        z = z_ref[:, h * RET_DV:(h + 1) * RET_DV].astype(F32)
        o_ref[:, h * RET_DV:(h + 1) * RET_DV] = (d * lax.rsqrt(var + GN_EPS) * _silu(z)).astype(o_ref.dtype)


def _retention(pr, tables, batch, seq):
    n = batch * seq
    n_c = seq // RET_CHUNK
    dec, qd, kd, cd = tables
    qk_w = RET_HEADS * RET_DK
    v_w = RET_HEADS * RET_DV
    row = lambda b, c: b * n_c + c
    return pl.pallas_call(
        _ret_kernel,
        grid=(batch, n_c),
        in_specs=[
            pl.BlockSpec((RET_CHUNK, qk_w), lambda b, c: (row(b, c), 0)),
            pl.BlockSpec((RET_CHUNK, qk_w), lambda b, c: (row(b, c), 1)),
            pl.BlockSpec((RET_CHUNK, v_w), lambda b, c: (row(b, c), 1)),
            pl.BlockSpec((RET_CHUNK, v_w), lambda b, c: (row(b, c), 2)),
            _const_spec(dec.shape), _const_spec(qd.shape), _const_spec(kd.shape), _const_spec(cd.shape),
        ],
        out_specs=pl.BlockSpec((RET_CHUNK, v_w), lambda b, c: (row(b, c), 0)),
        out_shape=jax.ShapeDtypeStruct((n, v_w), BF16),
        scratch_shapes=[pltpu.VMEM((RET_HEADS, RET_DK, RET_DV), F32)],
        compiler_params=pltpu.CompilerParams(
            dimension_semantics=("arbitrary", "arbitrary"), vmem_limit_bytes=VMEM_LIMIT),
        name="retention",
    )(pr, pr, pr, pr, dec, qd, kd, cd)


def _overlap_matrix_t(n_cmp_pad):
    i = np.arange(n_cmp_pad)[None, :]
    j = np.arange(SEL_BLOCK)[:, None]
    ov = (i * CMP_STRIDE < (j + 1) * SEL_BLOCK) & (i * CMP_STRIDE + CMP_BLOCK > j * SEL_BLOCK)
    return jnp.asarray(ov, BF16)


def _rotary_tables(seq):
    half = RET_DK // 2
    inv = ROPE_BASE ** (-jnp.linspace(0.0, 1.0, half, dtype=F32))
    ang = jnp.arange(seq, dtype=F32)[:, None] * inv[None, :]
    return jnp.cos(ang), jnp.sin(ang)


def _retention_tables():
    c = RET_CHUNK
    log_g = jnp.log(1.0 - 2.0 ** (-5.0 - jnp.arange(RET_HEADS, dtype=F32)))
    ix = jnp.arange(c, dtype=F32)
    diff = ix[:, None] - ix[None, :]
    dec = jnp.where(diff >= 0, jnp.exp(log_g[:, None, None] * jnp.maximum(diff, 0.0)), 0.0)
    qd = jnp.exp(log_g[:, None] * (ix + 1.0))[:, :, None]
    kd = jnp.exp(log_g[:, None] * (c - 1.0 - ix))[:, :, None]
    cd = jnp.broadcast_to(jnp.exp(log_g * c)[:, None, None], (RET_HEADS, 1, RET_DV))
    return dec, qd, kd, cd


def _nsa_layer(x2d, p2d, batch, seq, norm_g, w_in, q_g, kc_g, ks_g, kw_g, pos_k, pos_v,
               ck_w1, ck_w2, cv_w1, cv_w2, w_out, ple_w, ple_gate_w):
    n = batch * seq
    qw = NSA_HEADS * NSA_DK
    kvw = NSA_G * NSA_DK
    sizes = [qw] + [kvw] * 6 + [3 * NSA_HEADS, qw]
    offs = np.concatenate([[0], np.cumsum(sizes)])
    col = lambda i: w_in[:, offs[i]:offs[i + 1]]
    wq, wkc, wvc, wks, wvs, wkw, wvw, wgl, wz = [col(i) for i in range(9)]
    wgl = jnp.pad(wgl, ((0, 0), (0, LANES - 3 * NSA_HEADS)))
    w = jnp.concatenate([wq, wks, wkw, wvs, wvw, wz, wkc, wvc, wgl], axis=1).astype(BF16)
    head_gain = jnp.concatenate([
        jnp.tile(q_g, NSA_HEADS), jnp.tile(ks_g, NSA_G), jnp.tile(kw_g, NSA_G),
        jnp.zeros((w.shape[1] - qw - 2 * kvw,), F32)]).reshape(1, -1)
    plan = []
    c = 0
    for _ in range(qw // 256):
        plan.append((c, 256, "headnorm", 0, c, NSA_DK ** -0.5)); c += 256
    for _ in range(2):
        plan.append((c, 256, "headnorm", 0, c, 1.0)); c += 256
    for _ in range((2 * kvw + qw + 2 * kvw) // 256):
        plan.append((c, 256, "raw", 0, c, 1.0)); c += 256
    main_cols = c
    plan.append((c, LANES, "raw", 1, 0, 1.0))
    dummy = jnp.zeros((seq, LANES), F32)
    pm, pg = _proj(x2d, norm_g, w, head_gain, dummy, dummy, tuple(plan),
                   [(main_cols, BF16), (LANES, F32)], seq)

    o_ks, o_kw, o_vs, o_vw = qw, qw + kvw, qw + 2 * kvw, qw + 3 * kvw
    o_z = qw + 4 * kvw
    o_kc, o_vc = o_z + qw, o_z + qw + kvw

    def grouped(c0, length):
        a = pm[:, c0:c0 + kvw].reshape(batch, seq, NSA_G, NSA_DK)
        return a.transpose(0, 2, 1, 3)

    n_tiles = seq // Q_TILE
    pad_tiles = WINDOW // Q_TILE

    def value_tiles(c0, front_tiles):
        vt = grouped(c0, seq).transpose(0, 1, 3, 2)
        vt = jnp.concatenate([vt, jnp.ones((batch, NSA_G, V_ROWS - NSA_DK, seq), BF16)], axis=2)
        vt = vt.reshape(batch, NSA_G, V_ROWS, n_tiles, Q_TILE).transpose(0, 1, 3, 2, 4)
        return jnp.pad(vt, ((0, 0), (0, 0), (front_tiles, 0), (0, 0), (0, 0)))

    blk_id = np.arange(seq)[:, None] // SEL_BLOCK
    onehot = (blk_id == np.arange(NSA_DK)[None, :]) & (np.arange(NSA_DK)[None, :] > 0)
    onehot = jnp.broadcast_to(jnp.asarray(onehot, BF16), (batch, NSA_G, seq, NSA_DK))
    ks_aug = jnp.concatenate([grouped(o_ks, seq), onehot], axis=-1)
    pad_rows = np.zeros((WINDOW, LANES), np.float32)
    pad_rows[:, NSA_DK] = SEL_MASK_BIAS
    pad_rows = jnp.broadcast_to(jnp.asarray(pad_rows, BF16), (batch, NSA_G, WINDOW, LANES))
    kw_pad = jnp.concatenate(
        [pad_rows, jnp.concatenate([grouped(o_kw, seq), jnp.zeros((batch, NSA_G, seq, NSA_DK), BF16)],
                                   axis=-1)], axis=2)
    vst = value_tiles(o_vs, 0)
    vwt = value_tiles(o_vw, pad_tiles)
    n_gate = 3 * NSA_HPG
    glt = pg[:, :NSA_G * n_gate].reshape(batch, seq, NSA_G, n_gate).transpose(0, 2, 3, 1)
    glt = jnp.pad(glt, ((0, 0), (0, 0), (0, 16 - n_gate), (0, 0)))

    n_sub = seq // CMP_STRIDE

    def sub_blocks(c0):
        a = pm[:, c0:c0 + kvw].reshape(batch, n_sub, CMP_STRIDE, NSA_G, NSA_DK)
        return a.transpose(0, 3, 1, 2, 4).reshape(batch, NSA_G, n_sub, CMP_STRIDE * NSA_DK)

    half = CMP_STRIDE * NSA_DK
    w1cat = lambda w1: jnp.concatenate([w1[:half], w1[half:]], axis=1).astype(BF16)
    pos_flat = lambda pos: jnp.pad(pos.reshape(1, CMP_BLOCK * NSA_DK), ((0, 7), (0, 0))).astype(BF16)
    w2pad = lambda w2: jnp.pad(w2, ((0, 0), (0, LANES - NSA_DK))).astype(BF16)
    gk = jnp.pad(kc_g, (0, LANES - NSA_DK)).reshape(1, LANES)
    kc_pad, vct = _compress(sub_blocks(o_kc), sub_blocks(o_vc), w1cat(ck_w1), w1cat(cv_w1),
                            pos_flat(pos_k), pos_flat(pos_v), w2pad(ck_w2), w2pad(cv_w2), gk)

    a = _nsa_attention(pm, glt, ks_aug, vst, kw_pad, vwt, kc_pad, vct,
                       _overlap_matrix_t(n_sub), batch, seq, o_z // 256)
    return _out_proj(a, w_out, x2d, p2d, ple_gate_w, ple_w)


def _ret_layer(x2d, p2d, batch, seq, norm_g, w_in, w_out, ple_w, ple_gate_w):
    qk_w = RET_HEADS * RET_DK
    v_w = RET_HEADS * RET_DV
    cols = 2 * qk_w + 2 * v_w
    plan = []
    c = 0
    for _ in range(RET_HEADS):
        plan.append((c, RET_DK, "rot", 0, c, 1.0)); c += RET_DK
    for _ in range(RET_HEADS):
        plan.append((c, RET_DK, "rot", 0, c, RET_DK ** -0.5)); c += RET_DK
    while c < cols:
        plan.append((c, 512, "raw", 0, c, 1.0)); c += 512
    cos, sin = _rotary_tables(seq)
    (pr,) = _proj(x2d, norm_g, w_in.astype(BF16), jnp.zeros((1, cols), F32), cos, sin,
                  tuple(plan), [(cols, BF16)], seq)
    a = _retention(pr, _retention_tables(), batch, seq)
    return _out_proj(a, w_out, x2d, p2d, ple_gate_w, ple_w)


def kernel(x, p, norm_g, nsa_w_in, nsa_q_g, nsa_kc_g, nsa_ks_g, nsa_kw_g, nsa_cmp_pos_k, nsa_cmp_pos_v, nsa_cmp_k_w1, nsa_cmp_k_w2, nsa_cmp_v_w1, nsa_cmp_v_w2, nsa_w_out, ret_w_in, ret_w_out, ple_w, ple_gate_w):
    batch, seq, d_model = x.shape
    depth = p.shape[0]
    n = batch * seq
    x2d = x.reshape(n, d_model)
    for i in range(depth):
        p2d = p[i].reshape(n, PLE_DIM)
        j = i // 2
        if i % 2 == 0:
            x2d = _nsa_layer(x2d, p2d, batch, seq, norm_g[i], nsa_w_in[j], nsa_q_g[j], nsa_kc_g[j],
                             nsa_ks_g[j], nsa_kw_g[j], nsa_cmp_pos_k[j], nsa_cmp_pos_v[j],
                             nsa_cmp_k_w1[j], nsa_cmp_k_w2[j], nsa_cmp_v_w1[j], nsa_cmp_v_w2[j],
                             nsa_w_out[j], ple_w[i], ple_gate_w[i])
        else:
            x2d = _ret_layer(x2d, p2d, batch, seq, norm_g[i], ret_w_in[j], ret_w_out[j],
                             ple_w[i], ple_gate_w[i])
    return x2d.reshape(batch, seq, d_model)
```

```python
import functools

import numpy as np
import jax
import jax.numpy as jnp
from jax import lax
from jax.experimental import pallas as pl
from jax.experimental.pallas import tpu as pltpu

F32 = jnp.float32
BF16 = jnp.bfloat16

D_MODEL = 1024
PLE_DIM = 256
RMS_EPS = 1e-6
GN_EPS = 1e-5

NSA_HEADS = 16
NSA_DK = 64
NSA_G = 4
NSA_HPG = 4
CMP_BLOCK = 32
CMP_STRIDE = 16
CMP_HIDDEN = 256
SEL_BLOCK = 64
SEL_TOPN = 16
WINDOW = 512
FORCE_BONUS = 1e4
NEG = -1e30
SEL_MASK_BIAS = -30000.0

RET_HEADS = 4
RET_DK = 256
RET_DV = 512
RET_CHUNK = 128
ROPE_BASE = 10000.0

LANES = 128
VMEM_LIMIT = 56 * 1024 * 1024

ROW_TILE = 512
Q_TILE = 256
SEL_KEYS = 512
LOG2E = 1.4426950408889634
MAX_SAFE_BOUND = 50.0
V_ROWS = NSA_DK + 16


def _dot(a, b):
    return jnp.dot(a, b, preferred_element_type=F32)


def _dot_nt(a, b):
    return lax.dot_general(a, b, (((1,), (1,)), ((), ())), preferred_element_type=F32)


def _silu(x):
    return x * jax.nn.sigmoid(x)


def _const_spec(shape):
    nd = len(shape)
    return pl.BlockSpec(shape, lambda *_: (0,) * nd)


def _proj_kernel(x_ref, g_ref, w_ref, hg_ref, bd_ref, cos_ref, sin_ref, *out_refs, plan):
    x = x_ref[...]
    h = x * lax.rsqrt(jnp.mean(x * x, axis=-1, keepdims=True) + RMS_EPS) * g_ref[...]
    h = h.astype(BF16)
    for (c0, width, kind, dst, d0, scale) in plan:
        acc = _dot(h, w_ref[:, c0:c0 + width])
        if kind == "headnorm":
            ss = _dot((acc * acc).astype(BF16), bd_ref[...])
            acc = acc * lax.rsqrt(ss * (1.0 / NSA_DK) + RMS_EPS) * hg_ref[:, c0:c0 + width]
        elif kind == "rot":
            half = width // 2
            a1, a2 = acc[:, :half], acc[:, half:]
            cos, sin = cos_ref[...], sin_ref[...]
            acc = jnp.concatenate([a1 * cos - a2 * sin, a1 * sin + a2 * cos], axis=-1)
        if scale != 1.0:
            acc = acc * scale
        o_ref = out_refs[dst]
        o_ref[:, d0:d0 + width] = acc.astype(o_ref.dtype)


def _proj(x2d, norm_g, w, head_gain, cos, sin, plan, out_defs, seq):
    n = x2d.shape[0]
    cols = w.shape[1]
    n_pos_blocks = seq // ROW_TILE
    bd = (np.arange(256)[:, None] // NSA_DK == np.arange(256)[None, :] // NSA_DK)
    bd = jnp.asarray(bd, BF16)
    out_shape = [jax.ShapeDtypeStruct((n, c), dt) for (c, dt) in out_defs]
    out_specs = [pl.BlockSpec((ROW_TILE, c), lambda i: (i, 0)) for (c, _) in out_defs]
    return pl.pallas_call(
        functools.partial(_proj_kernel, plan=plan),
        grid=(n // ROW_TILE,),
        in_specs=[
            pl.BlockSpec((ROW_TILE, D_MODEL), lambda i: (i, 0)),
            _const_spec((1, D_MODEL)),
            _const_spec((D_MODEL, cols)),
            _const_spec((1, cols)),
            _const_spec((256, 256)),
            pl.BlockSpec((ROW_TILE, LANES), lambda i: (i % n_pos_blocks, 0)),
            pl.BlockSpec((ROW_TILE, LANES), lambda i: (i % n_pos_blocks, 0)),
        ],
        out_specs=out_specs,
        out_shape=out_shape,
        compiler_params=pltpu.CompilerParams(
            dimension_semantics=("arbitrary",), vmem_limit_bytes=VMEM_LIMIT),
        name="norm_proj",
    )(x2d, norm_g.reshape(1, D_MODEL), w, head_gain, bd, cos, sin)


def _out_kernel(a_ref, wo_ref, x_ref, p_ref, wg_ref, wp_ref, o_ref):
    x1 = x_ref[...] + _dot(a_ref[...], wo_ref[...])
    gate = jax.nn.sigmoid(_dot(x1.astype(BF16), wg_ref[...]))
    emb = _dot(p_ref[...].astype(BF16), wp_ref[...])
    o_ref[...] = x1 + gate * emb


def _out_proj(a, w_out, x2d, p2d, w_gate, w_ple):
    n, k = a.shape
    return pl.pallas_call(
        _out_kernel,
        grid=(n // ROW_TILE,),
        in_specs=[
            pl.BlockSpec((ROW_TILE, k), lambda i: (i, 0)),
            _const_spec((k, D_MODEL)),
            pl.BlockSpec((ROW_TILE, D_MODEL), lambda i: (i, 0)),
            pl.BlockSpec((ROW_TILE, PLE_DIM), lambda i: (i, 0)),
            _const_spec((D_MODEL, D_MODEL)),
            _const_spec((PLE_DIM, D_MODEL)),
        ],
        out_specs=pl.BlockSpec((ROW_TILE, D_MODEL), lambda i: (i, 0)),
        out_shape=jax.ShapeDtypeStruct((n, D_MODEL), F32),
        compiler_params=pltpu.CompilerParams(
            dimension_semantics=("arbitrary",), vmem_limit_bytes=VMEM_LIMIT),
        name="out_proj_ple",
    )(a, w_out.astype(BF16), x2d, p2d, w_gate.astype(BF16), w_ple.astype(BF16))


def _cmp_kernel(xk_ref, xv_ref, w1k_ref, w1v_ref, posk_ref, posv_ref, w2k_ref, w2v_ref,
                gk_ref, kc_ref, vc_ref):
    n_sub = xk_ref.shape[2]
    half = xk_ref.shape[3]
    for is_k, x_ref, w1_ref, pos_ref, w2_ref, o_ref in (
            (True, xk_ref, w1k_ref, posk_ref, w2k_ref, kc_ref),
            (False, xv_ref, w1v_ref, posv_ref, w2v_ref, vc_ref)):
        w1 = w1_ref[...]
        x = x_ref[0].reshape(NSA_G * n_sub, half)
        ab = _dot(x, w1)
        pos = pos_ref[...]
        pos_term = (_dot(pos[:, :half], w1[:, :CMP_HIDDEN])
                    + _dot(pos[:, half:], w1[:, CMP_HIDDEN:]))[0:1]
        for g in range(NSA_G):
            first = ab[g * n_sub:(g + 1) * n_sub, :CMP_HIDDEN]
            second = ab[g * n_sub:(g + 1) * n_sub, CMP_HIDDEN:]
            hid = _silu(first + pltpu.roll(second, n_sub - 1, 0) + pos_term)
            c = _dot(hid.astype(BF16), w2_ref[...])
            if is_k:
                ss = jnp.sum(c * c, axis=-1, keepdims=True) * (1.0 / NSA_DK)
                c = c * lax.rsqrt(ss + RMS_EPS) * gk_ref[...]
                o_ref[0, g] = c.astype(BF16)
            else:
                ct = jnp.concatenate([c.T[:NSA_DK], jnp.ones((V_ROWS - NSA_DK, n_sub), F32)], axis=0)
                o_ref[0, g] = ct.astype(BF16)


def _compress(xk, xv, w1k, w1v, posk, posv, w2k, w2v, gk):
    b, g, n_sub, half = xk.shape
    blk = pl.BlockSpec((1, g, n_sub, half), lambda i: (i, 0, 0, 0))
    k_blk = pl.BlockSpec((1, g, n_sub, LANES), lambda i: (i, 0, 0, 0))
    k_sds = jax.ShapeDtypeStruct((b, g, n_sub, LANES), BF16)
    v_blk = pl.BlockSpec((1, g, V_ROWS, n_sub), lambda i: (i, 0, 0, 0))
    v_sds = jax.ShapeDtypeStruct((b, g, V_ROWS, n_sub), BF16)
    return pl.pallas_call(
        _cmp_kernel,
        grid=(b,),
        in_specs=[blk, blk,
                  _const_spec(w1k.shape), _const_spec(w1v.shape),
                  _const_spec(posk.shape), _const_spec(posv.shape),
                  _const_spec(w2k.shape), _const_spec(w2v.shape),
                  _const_spec(gk.shape)],
        out_specs=[k_blk, v_blk],
        out_shape=[k_sds, v_sds],
        compiler_params=pltpu.CompilerParams(
            dimension_semantics=("arbitrary",), vmem_limit_bytes=VMEM_LIMIT),
        name="nsa_compress",
    )(xk, xv, w1k, w1v, posk, posv, w2k, w2v, gk)


def _nsa_attn_kernel(fast_ref, q_ref, glt_ref, z_ref, ks_ref, vst_ref, kw_ref, vwt_ref, kc_ref, vct_ref,
                     ovt_ref, cthr_ref, kmq_ref, o_ref, acc_ref, accw_ref, rank_ref):
    tq = Q_TILE
    cols = NSA_HPG * tq
    qi = pl.program_id(2)
    t0 = qi * tq
    n_sel = SEL_BLOCK
    dk = NSA_DK

    qt = q_ref[...].astype(F32).T
    q_heads = [qt[h * dk:(h + 1) * dk] for h in range(NSA_HPG)]

    def stack_heads(extra_rows):
        return jnp.concatenate(
            [jnp.concatenate([qh, extra_rows], axis=0) for qh in q_heads], axis=1).astype(BF16)

    sc = _dot(kc_ref[0, 0], stack_heads(jnp.zeros((dk, tq), F32)))
    sc = jnp.where(cthr_ref[...] <= t0, sc, NEG)
    mcol = jnp.maximum(jnp.max(sc, axis=0, keepdims=True), 0.1 * NEG)
    ec = jnp.exp2(sc - mcol).astype(BF16)
    r = _dot(jnp.concatenate([vct_ref[0, 0], ovt_ref[...]], axis=0), ec)
    lc = r[dk:dk + 1]
    inv_lc = 1.0 / jnp.where(lc > 0.0, lc, 1.0)
    a_c = r[:dk] * inv_lc
    imp = r[V_ROWS:] * inv_lc
    imp_t = imp[:, 0:tq] + imp[:, tq:2 * tq] + imp[:, 2 * tq:3 * tq] + imp[:, 3 * tq:4 * tq]

    blk = lax.broadcasted_iota(jnp.int32, (n_sel, tq), 0)
    cur = (t0 + lax.broadcasted_iota(jnp.int32, (n_sel, tq), 1)) // SEL_BLOCK
    forced = (blk == 0) | (blk == cur) | (blk == cur - 1)
    score = jnp.where(blk <= cur, imp_t + jnp.where(forced, FORCE_BONUS, 0.0), NEG)
    sub = 8
    chunks = [score[c * sub:(c + 1) * sub] for c in range(n_sel // sub)]
    blk_sub = lax.broadcasted_iota(jnp.int32, (sub, tq), 0)
    last_blk = (t0 + tq - 1) // SEL_BLOCK
    rank_ref[...] = jnp.zeros_like(rank_ref)
    for jb in range(n_sel // sub):
        @pl.when(jb * sub <= last_blk)
        def _():
            ranks = [jnp.zeros((sub, tq), F32) for _ in chunks]
            for j in range(jb * sub, (jb + 1) * sub):
                row = jnp.broadcast_to(chunks[jb][j - jb * sub:j - jb * sub + 1, :], (sub, tq))
                for c, chunk in enumerate(chunks):
                    if c > jb:
                        one = jnp.where(row >= chunk, 1.0, 0.0)
                    elif c < jb:
                        one = jnp.where(row > chunk, 1.0, 0.0)
                    else:
                        one = jnp.where(blk_sub > j - c * sub,
                                        jnp.where(row >= chunk, 1.0, 0.0), jnp.where(row > chunk, 1.0, 0.0))
                    ranks[c] = ranks[c] + one
            rank_ref[...] += jnp.concatenate(ranks, axis=0)
    bias_t = jnp.where(rank_ref[...] < float(SEL_TOPN), 0.0, SEL_MASK_BIAS)
    q_aug = stack_heads(jnp.where(blk == 0, 1.0, bias_t))

    tiles_per_step = SEL_KEYS // tq
    n_full = t0 // SEL_KEYS
    n_wt = WINDOW // tq + 1
    diag_keep = kmq_ref[0:tq, :] <= 0

    def sel_scores(step, tail):
        base = pl.multiple_of(step * SEL_KEYS, SEL_KEYS)
        s = _dot(ks_ref[0, 0, pl.ds(base, SEL_KEYS), :], q_aug)
        if tail:
            s = jnp.where(kmq_ref[...] <= t0 - base, s, NEG)
        vt = vst_ref[0, 0, pl.ds(step * tiles_per_step, tiles_per_step)]
        return s, jnp.concatenate([vt[i] for i in range(tiles_per_step)], axis=1)

    def win_scores():
        sw = _dot(kw_ref[0, 0, pl.ds(pl.multiple_of(t0, tq), n_wt * tq), :], q_aug)
        parts = [jnp.where(diag_keep, NEG, sw[:tq])]
        if WINDOW > tq:
            parts.append(sw[tq:WINDOW])
        parts.append(jnp.where(diag_keep, sw[WINDOW:], NEG))
        sw = jnp.concatenate(parts, axis=0)
        vwt = vwt_ref[0, 0, pl.ds(qi, n_wt)]
        return sw, jnp.concatenate([vwt[i] for i in range(n_wt)], axis=1)

    fast = fast_ref[0] > 0

    @pl.when(fast)
    def _():
        acc_ref[...] = jnp.zeros_like(acc_ref)

        def step(i, carry):
            s, vt = sel_scores(i, False)
            acc_ref[...] += _dot(vt, jnp.exp2(s).astype(BF16))
            return carry

        lax.fori_loop(0, n_full, step, 0)
        s, vt = sel_scores(n_full, True)
        acc_ref[...] += _dot(vt, jnp.exp2(s).astype(BF16))
        sw, vwt = win_scores()
        accw_ref[...] = _dot(vwt, jnp.exp2(sw).astype(BF16))

    @pl.when(jnp.logical_not(fast))
    def _():
        acc_ref[...] = jnp.zeros_like(acc_ref)

        def online(s, vt, m):
            m_new = jnp.maximum(m, jnp.max(s, axis=0, keepdims=True))
            acc_ref[...] = jnp.exp2(m - m_new) * acc_ref[...] + _dot(vt, jnp.exp2(s - m_new).astype(BF16))
            return m_new

        m = jnp.full((1, cols), NEG, F32)
        m = lax.fori_loop(0, n_full, lambda i, m: online(*sel_scores(i, False), m), m)
        online(*sel_scores(n_full, True), m)
        sw, vwt = win_scores()
        accw_ref[...] = _dot(vwt, jnp.exp2(sw - jnp.max(sw, axis=0, keepdims=True)).astype(BF16))

    a_s = acc_ref[...]
    a_w = accw_ref[...]

    a_s = a_s[:dk] * (1.0 / a_s[dk:dk + 1])
    a_w = a_w[:dk] * (1.0 / a_w[dk:dk + 1])
    gates = jax.nn.sigmoid(glt_ref[0, 0])
    outs = []
    for h in range(NSA_HPG):
        sl = slice(h * tq, (h + 1) * tq)
        outs.append(gates[3 * h:3 * h + 1] * a_c[:, sl] + gates[3 * h + 1:3 * h + 2] * a_s[:, sl]
                    + gates[3 * h + 2:3 * h + 3] * a_w[:, sl])
    out = jnp.concatenate(outs, axis=0).T
    o_ref[...] = (out * _silu(z_ref[...].astype(F32))).astype(o_ref.dtype)


def _nsa_attention(fast, pm, glt, ks_aug, vst, kw_pad, vwt, kc_pad, vct, overlap_t, batch, seq,
                   z_col_block):
    cols = NSA_HPG * Q_TILE
    query = np.arange(cols)[None, :] % Q_TILE
    cmp_thr = jnp.asarray(np.arange(kc_pad.shape[2])[:, None] * CMP_STRIDE + (CMP_BLOCK - 1) - query, jnp.int32)
    key_minus_query = jnp.asarray(np.arange(SEL_KEYS)[:, None] - query, jnp.int32)
    n = batch * seq
    n_q = seq // Q_TILE
    gw = NSA_HPG * NSA_DK

    def per_group(arr):
        nd = arr.ndim - 2
        return pl.BlockSpec((1, 1) + arr.shape[2:], lambda b, g, i: (b, g) + (0,) * nd)

    return pl.pallas_call(
        _nsa_attn_kernel,
        grid=(batch, NSA_G, n_q),
        in_specs=[
            pl.BlockSpec(memory_space=pltpu.SMEM),
            pl.BlockSpec((Q_TILE, gw), lambda b, g, i: (b * n_q + i, g)),
            pl.BlockSpec((1, 1, glt.shape[2], Q_TILE), lambda b, g, i: (b, g, 0, i)),
            pl.BlockSpec((Q_TILE, gw), lambda b, g, i: (b * n_q + i, z_col_block + g)),
            per_group(ks_aug), per_group(vst), per_group(kw_pad), per_group(vwt),
            per_group(kc_pad), per_group(vct),
            _const_spec(overlap_t.shape), _const_spec(cmp_thr.shape), _const_spec(key_minus_query.shape),
        ],
        out_specs=pl.BlockSpec((Q_TILE, gw), lambda b, g, i: (b * n_q + i, g)),
        out_shape=jax.ShapeDtypeStruct((n, NSA_HEADS * NSA_DK), BF16),
        scratch_shapes=[pltpu.VMEM((V_ROWS, cols), F32)] * 2 + [pltpu.VMEM((SEL_BLOCK, Q_TILE), F32)],
        compiler_params=pltpu.CompilerParams(
            dimension_semantics=("arbitrary", "arbitrary", "arbitrary"),
            vmem_limit_bytes=VMEM_LIMIT),
        name="nsa_attention",
    )(fast, pm, glt, pm, ks_aug, vst, kw_pad, vwt, kc_pad, vct, overlap_t, cmp_thr, key_minus_query)


def _ret_kernel(q_ref, k_ref, v_ref, z_ref, dec_ref, qd_ref, kd_ref, cd_ref, o_ref, st_ref):
    @pl.when(pl.program_id(1) == 0)
    def _():
        st_ref[...] = jnp.zeros_like(st_ref)

    for h in range(RET_HEADS):
        q = q_ref[:, h * RET_DK:(h + 1) * RET_DK]
        k = k_ref[:, h * RET_DK:(h + 1) * RET_DK]
        v = v_ref[:, h * RET_DV:(h + 1) * RET_DV]
        att = _dot_nt(q, k) * dec_ref[h]
        state = st_ref[h]
        o = _dot(att.astype(BF16), v) + _dot(q, state.astype(BF16)) * qd_ref[h]
        kd_t = (k.astype(F32) * kd_ref[h]).T.astype(BF16)
        st_ref[h] = state * cd_ref[h] + _dot(kd_t, v)
        mu = jnp.mean(o, axis=-1, keepdims=True)
        d = o - mu
        var = jnp.mean(d * d, axis=-1, keepdims=True)
        z = z_ref[:, h * RET_DV:(h + 1) * RET_DV].astype(F32)
        o_ref[:, h * RET_DV:(h + 1) * RET_DV] = (d * lax.rsqrt(var + GN_EPS) * _silu(z)).astype(o_ref.dtype)


def _retention(pr, tables, batch, seq):
    n = batch * seq
    n_c = seq // RET_CHUNK
    dec, qd, kd, cd = tables
    qk_w = RET_HEADS * RET_DK
    v_w = RET_HEADS * RET_DV
    row = lambda b, c: b * n_c + c
    return pl.pallas_call(
        _ret_kernel,
        grid=(batch, n_c),
        in_specs=[
            pl.BlockSpec((RET_CHUNK, qk_w), lambda b, c: (row(b, c), 0)),
            pl.BlockSpec((RET_CHUNK, qk_w), lambda b, c: (row(b, c), 1)),
            pl.BlockSpec((RET_CHUNK, v_w), lambda b, c: (row(b, c), 1)),
            pl.BlockSpec((RET_CHUNK, v_w), lambda b, c: (row(b, c), 2)),
            _const_spec(dec.shape), _const_spec(qd.shape), _const_spec(kd.shape), _const_spec(cd.shape),
        ],
        out_specs=pl.BlockSpec((RET_CHUNK, v_w), lambda b, c: (row(b, c), 0)),
        out_shape=jax.ShapeDtypeStruct((n, v_w), BF16),
        scratch_shapes=[pltpu.VMEM((RET_HEADS, RET_DK, RET_DV), F32)],
        compiler_params=pltpu.CompilerParams(
            dimension_semantics=("arbitrary", "arbitrary"), vmem_limit_bytes=VMEM_LIMIT),
        name="retention",
    )(pr, pr, pr, pr, dec, qd, kd, cd)


def _overlap_matrix_t(n_cmp_pad):
    i = np.arange(n_cmp_pad)[None, :]
    j = np.arange(SEL_BLOCK)[:, None]
    ov = (i * CMP_STRIDE < (j + 1) * SEL_BLOCK) & (i * CMP_STRIDE + CMP_BLOCK > j * SEL_BLOCK)
    return jnp.asarray(ov, BF16)


def _rotary_tables(seq):
    half = RET_DK // 2
    inv = ROPE_BASE ** (-jnp.linspace(0.0, 1.0, half, dtype=F32))
    ang = jnp.arange(seq, dtype=F32)[:, None] * inv[None, :]
    return jnp.cos(ang), jnp.sin(ang)


def _retention_tables():
    c = RET_CHUNK
    log_g = jnp.log(1.0 - 2.0 ** (-5.0 - jnp.arange(RET_HEADS, dtype=F32)))
    ix = jnp.arange(c, dtype=F32)
    diff = ix[:, None] - ix[None, :]
    dec = jnp.where(diff >= 0, jnp.exp(log_g[:, None, None] * jnp.maximum(diff, 0.0)), 0.0)
    qd = jnp.exp(log_g[:, None] * (ix + 1.0))[:, :, None]
    kd = jnp.exp(log_g[:, None] * (c - 1.0 - ix))[:, :, None]
    cd = jnp.broadcast_to(jnp.exp(log_g * c)[:, None, None], (RET_HEADS, 1, RET_DV))
    return dec, qd, kd, cd


def _nsa_layer(x2d, p2d, batch, seq, norm_g, w_in, q_g, kc_g, ks_g, kw_g, pos_k, pos_v,
               ck_w1, ck_w2, cv_w1, cv_w2, w_out, ple_w, ple_gate_w):
    n = batch * seq
    qw = NSA_HEADS * NSA_DK
    kvw = NSA_G * NSA_DK
    sizes = [qw] + [kvw] * 6 + [3 * NSA_HEADS, qw]
    offs = np.concatenate([[0], np.cumsum(sizes)])
    col = lambda i: w_in[:, offs[i]:offs[i + 1]]
    wq, wkc, wvc, wks, wvs, wkw, wvw, wgl, wz = [col(i) for i in range(9)]
    wgl = jnp.pad(wgl, ((0, 0), (0, LANES - 3 * NSA_HEADS)))
    w = jnp.concatenate([wq, wks, wkw, wvs, wvw, wz, wkc, wvc, wgl], axis=1).astype(BF16)
    head_gain = jnp.concatenate([
        jnp.tile(q_g, NSA_HEADS), jnp.tile(ks_g, NSA_G), jnp.tile(kw_g, NSA_G),
        jnp.zeros((w.shape[1] - qw - 2 * kvw,), F32)]).reshape(1, -1)
    plan = []
    c = 0
    for _ in range(qw // 256):
        plan.append((c, 256, "headnorm", 0, c, NSA_DK ** -0.5 * LOG2E)); c += 256
    for _ in range(2):
        plan.append((c, 256, "headnorm", 0, c, 1.0)); c += 256
    for _ in range((2 * kvw + qw + 2 * kvw) // 256):
        plan.append((c, 256, "raw", 0, c, 1.0)); c += 256
    main_cols = c
    plan.append((c, LANES, "raw", 1, 0, 1.0))
    dummy = jnp.zeros((seq, LANES), F32)
    pm, pg = _proj(x2d, norm_g, w, head_gain, dummy, dummy, tuple(plan),
                   [(main_cols, BF16), (LANES, F32)], seq)

    o_ks, o_kw, o_vs, o_vw = qw, qw + kvw, qw + 2 * kvw, qw + 3 * kvw
    o_z = qw + 4 * kvw
    o_kc, o_vc = o_z + qw, o_z + qw + kvw

    def grouped(c0, length):
        a = pm[:, c0:c0 + kvw].reshape(batch, seq, NSA_G, NSA_DK)
        return a.transpose(0, 2, 1, 3)

    n_tiles = seq // Q_TILE
    pad_tiles = WINDOW // Q_TILE

    def value_tiles(c0, front_tiles):
        vt = grouped(c0, seq).transpose(0, 1, 3, 2)
        vt = jnp.concatenate([vt, jnp.ones((batch, NSA_G, V_ROWS - NSA_DK, seq), BF16)], axis=2)
        vt = vt.reshape(batch, NSA_G, V_ROWS, n_tiles, Q_TILE).transpose(0, 1, 3, 2, 4)
        return jnp.pad(vt, ((0, 0), (0, 0), (front_tiles, 0), (0, 0), (0, 0)))

    q_norm = LOG2E * jnp.max(jnp.abs(q_g)) * 1.02
    bound_sel = (q_norm * NSA_DK ** 0.5 * jnp.max(jnp.abs(ks_g))).astype(BF16).astype(F32) * 1.01
    bound_win = (q_norm * NSA_DK ** 0.5 * jnp.max(jnp.abs(kw_g))).astype(BF16).astype(F32) * 1.01
    fast = (jnp.maximum(bound_sel, bound_win) <= MAX_SAFE_BOUND).astype(jnp.int32).reshape(1)

    def with_bound(extra, bound):
        lane0 = (np.arange(NSA_DK) == 0)
        return jnp.where(lane0, -bound, extra.astype(F32)).astype(BF16)

    blk_id = np.arange(seq)[:, None] // SEL_BLOCK
    onehot = (blk_id == np.arange(NSA_DK)[None, :]) & (np.arange(NSA_DK)[None, :] > 0)
    onehot = jnp.broadcast_to(jnp.asarray(onehot, BF16), (batch, NSA_G, seq, NSA_DK))
    ks_aug = jnp.concatenate([grouped(o_ks, seq), with_bound(onehot, bound_sel)], axis=-1)
    pad_rows = np.zeros((WINDOW, LANES), np.float32)
    pad_rows[:, NSA_DK] = SEL_MASK_BIAS
    pad_rows = jnp.broadcast_to(jnp.asarray(pad_rows, BF16), (batch, NSA_G, WINDOW, LANES))
    kw_pad = jnp.concatenate(
        [pad_rows, jnp.concatenate([grouped(o_kw, seq),
                                    with_bound(jnp.zeros((batch, NSA_G, seq, NSA_DK), BF16), bound_win)],
                                   axis=-1)], axis=2)
    vst = value_tiles(o_vs, 0)
    vwt = value_tiles(o_vw, pad_tiles)
    n_gate = 3 * NSA_HPG
    glt = pg[:, :NSA_G * n_gate].reshape(batch, seq, NSA_G, n_gate).transpose(0, 2, 3, 1)
    glt = jnp.pad(glt, ((0, 0), (0, 0), (0, 16 - n_gate), (0, 0)))

    n_sub = seq // CMP_STRIDE

    def sub_blocks(c0):
        a = pm[:, c0:c0 + kvw].reshape(batch, n_sub, CMP_STRIDE, NSA_G, NSA_DK)
        return a.transpose(0, 3, 1, 2, 4).reshape(batch, NSA_G, n_sub, CMP_STRIDE * NSA_DK)

    half = CMP_STRIDE * NSA_DK
    w1cat = lambda w1: jnp.concatenate([w1[:half], w1[half:]], axis=1).astype(BF16)
    pos_flat = lambda pos: jnp.pad(pos.reshape(1, CMP_BLOCK * NSA_DK), ((0, 7), (0, 0))).astype(BF16)
    w2pad = lambda w2: jnp.pad(w2, ((0, 0), (0, LANES - NSA_DK))).astype(BF16)
    gk = jnp.pad(kc_g, (0, LANES - NSA_DK)).reshape(1, LANES)
    kc_pad, vct = _compress(sub_blocks(o_kc), sub_blocks(o_vc), w1cat(ck_w1), w1cat(cv_w1),
                            pos_flat(pos_k), pos_flat(pos_v), w2pad(ck_w2), w2pad(cv_w2), gk)

    a = _nsa_attention(fast, pm, glt, ks_aug, vst, kw_pad, vwt, kc_pad, vct,
                       _overlap_matrix_t(n_sub), batch, seq, o_z // 256)
    return _out_proj(a, w_out, x2d, p2d, ple_gate_w, ple_w)


def _ret_layer(x2d, p2d, batch, seq, norm_g, w_in, w_out, ple_w, ple_gate_w):
    qk_w = RET_HEADS * RET_DK
    v_w = RET_HEADS * RET_DV
    cols = 2 * qk_w + 2 * v_w
    plan = []
    c = 0
    for _ in range(RET_HEADS):
        plan.append((c, RET_DK, "rot", 0, c, 1.0)); c += RET_DK
    for _ in range(RET_HEADS):
        plan.append((c, RET_DK, "rot", 0, c, RET_DK ** -0.5)); c += RET_DK
    while c < cols:
        plan.append((c, 512, "raw", 0, c, 1.0)); c += 512
    cos, sin = _rotary_tables(seq)
    (pr,) = _proj(x2d, norm_g, w_in.astype(BF16), jnp.zeros((1, cols), F32), cos, sin,
                  tuple(plan), [(cols, BF16)], seq)
    a = _retention(pr, _retention_tables(), batch, seq)
    return _out_proj(a, w_out, x2d, p2d, ple_gate_w, ple_w)


def kernel(x, p, norm_g, nsa_w_in, nsa_q_g, nsa_kc_g, nsa_ks_g, nsa_kw_g, nsa_cmp_pos_k, nsa_cmp_pos_v, nsa_cmp_k_w1, nsa_cmp_k_w2, nsa_cmp_v_w1, nsa_cmp_v_w2, nsa_w_out, ret_w_in, ret_w_out, ple_w, ple_gate_w):
    batch, seq, d_model = x.shape
    depth = p.shape[0]
    n = batch * seq
    x2d = x.reshape(n, d_model)
    for i in range(depth):
        p2d = p[i].reshape(n, PLE_DIM)
        j = i // 2
        if i % 2 == 0:
            x2d = _nsa_layer(x2d, p2d, batch, seq, norm_g[i], nsa_w_in[j], nsa_q_g[j], nsa_kc_g[j],
                             nsa_ks_g[j], nsa_kw_g[j], nsa_cmp_pos_k[j], nsa_cmp_pos_v[j],
                             nsa_cmp_k_w1[j], nsa_cmp_k_w2[j], nsa_cmp_v_w1[j], nsa_cmp_v_w2[j],
                             nsa_w_out[j], ple_w[i], ple_gate_w[i])
        else:
            x2d = _ret_layer(x2d, p2d, batch, seq, norm_g[i], ret_w_in[j], ret_w_out[j],
                             ple_w[i], ple_gate_w[i])
    return x2d.reshape(batch, seq, d_model)
```

```python
import functools

import numpy as np
import jax
import jax.numpy as jnp
from jax import lax
from jax.experimental import pallas as pl
from jax.experimental.pallas import tpu as pltpu

F32 = jnp.float32
BF16 = jnp.bfloat16

D_MODEL = 1024
PLE_DIM = 256
RMS_EPS = 1e-6
GN_EPS = 1e-5

NSA_HEADS = 16
NSA_DK = 64
NSA_G = 4
NSA_HPG = 4
CMP_BLOCK = 32
CMP_STRIDE = 16
CMP_HIDDEN = 256
SEL_BLOCK = 64
SEL_TOPN = 16
WINDOW = 512
FORCE_BONUS = 1e4
NEG = -1e30
SEL_MASK_BIAS = -30000.0

RET_HEADS = 4
RET_DK = 256
RET_DV = 512
RET_CHUNK = 128
ROPE_BASE = 10000.0

LANES = 128
VMEM_LIMIT = 56 * 1024 * 1024

ROW_TILE = 512
Q_TILE = 256
SEL_KEYS = 512
LOG2E = 1.4426950408889634
MAX_SAFE_BOUND = 50.0
V_ROWS = NSA_DK + 16


def _dot(a, b):
    return jnp.dot(a, b, preferred_element_type=F32)


def _dot_nt(a, b):
    return lax.dot_general(a, b, (((1,), (1,)), ((), ())), preferred_element_type=F32)


def _silu(x):
    return x * jax.nn.sigmoid(x)


def _const_spec(shape):
    nd = len(shape)
    return pl.BlockSpec(shape, lambda *_: (0,) * nd)


def _ret_proj_kernel(x_ref, g_ref, w_ref, cos_ref, sin_ref, o_ref):
    x = x_ref[...]
    h = (x * lax.rsqrt(jnp.mean(x * x, axis=-1, keepdims=True) + RMS_EPS) * g_ref[...]).astype(BF16)
    cos, sin = cos_ref[...], sin_ref[...]
    half = RET_DK // 2
    qk_w = RET_HEADS * RET_DK
    for t in range(2 * RET_HEADS):
        c0 = t * RET_DK
        acc = _dot(h, w_ref[:, c0:c0 + RET_DK])
        a1, a2 = acc[:, :half], acc[:, half:]
        rot = jnp.concatenate([a1 * cos - a2 * sin, a1 * sin + a2 * cos], axis=-1)
        if c0 >= qk_w:
            rot = rot * RET_DK ** -0.5
        o_ref[:, c0:c0 + RET_DK] = rot.astype(o_ref.dtype)
    for c0 in range(2 * qk_w, w_ref.shape[1], RET_DV):
        o_ref[:, c0:c0 + RET_DV] = _dot(h, w_ref[:, c0:c0 + RET_DV]).astype(o_ref.dtype)


def _ret_proj(x2d, norm_g, w, cos, sin, seq):
    n = x2d.shape[0]
    cols = w.shape[1]
    n_pos_blocks = seq // ROW_TILE
    return pl.pallas_call(
        _ret_proj_kernel,
        grid=(n // ROW_TILE,),
        in_specs=[
            pl.BlockSpec((ROW_TILE, D_MODEL), lambda i: (i, 0)),
            _const_spec((1, D_MODEL)),
            _const_spec((D_MODEL, cols)),
            pl.BlockSpec((ROW_TILE, LANES), lambda i: (i % n_pos_blocks, 0)),
            pl.BlockSpec((ROW_TILE, LANES), lambda i: (i % n_pos_blocks, 0)),
        ],
        out_specs=pl.BlockSpec((ROW_TILE, cols), lambda i: (i, 0)),
        out_shape=jax.ShapeDtypeStruct((n, cols), BF16),
        compiler_params=pltpu.CompilerParams(
            dimension_semantics=("arbitrary",), vmem_limit_bytes=VMEM_LIMIT),
        name="ret_norm_proj",
    )(x2d, norm_g.reshape(1, D_MODEL), w, cos, sin)


def _nsa_proj_kernel(x_ref, g_ref, w_ref, hg_ref, bd_ref, augs_ref, augw_ref, pad_ref,
                     q_ref, z_ref, cmp_ref, ks_ref, kw_ref, vst_ref, vwt_ref, glt_ref, *, offs):
    s = pl.program_id(1)
    tiles = ROW_TILE // Q_TILE

    @pl.when(s == 0)
    def _():
        for g in range(NSA_G):
            kw_ref[0, g] = pad_ref[...]
        vwt_ref[...] = jnp.zeros_like(vwt_ref)

    @pl.when(s > 0)
    def _():
        x = x_ref[...]
        h = (x * lax.rsqrt(jnp.mean(x * x, axis=-1, keepdims=True) + RMS_EPS) * g_ref[...]).astype(BF16)
        low = lax.broadcasted_iota(jnp.int32, (ROW_TILE, LANES), 1) < NSA_DK

        def proj(c0, width=256):
            return _dot(h, w_ref[:, c0:c0 + width])

        def head_norm(acc, c0):
            ss = _dot((acc * acc).astype(BF16), bd_ref[...])
            return acc * lax.rsqrt(ss * (1.0 / NSA_DK) + RMS_EPS) * hg_ref[:, c0:c0 + 256]

        for t in range(NSA_HEADS * NSA_DK // 256):
            c0 = offs["q"] + 256 * t
            q_ref[:, 256 * t:256 * (t + 1)] = head_norm(proj(c0), c0).astype(BF16)
            z_ref[:, 256 * t:256 * (t + 1)] = proj(offs["z"] + 256 * t).astype(BF16)

        for name, aug_ref, o_ref in (("ks", augs_ref, ks_ref), ("kw", augw_ref, kw_ref)):
            k = head_norm(proj(offs[name]), offs[name])
            aug = aug_ref[...].astype(F32)
            for pair in range(NSA_G // 2):
                two = k[:, LANES * pair:LANES * (pair + 1)]
                o_ref[0, 2 * pair] = jnp.where(low, two, aug).astype(BF16)
                o_ref[0, 2 * pair + 1] = jnp.where(low, pltpu.roll(two, NSA_DK, 1), aug).astype(BF16)

        ones = jnp.ones((V_ROWS - NSA_DK, Q_TILE), BF16)
        for name, o_ref in (("vs", vst_ref), ("vw", vwt_ref)):
            vt = proj(offs[name]).T
            for g in range(NSA_G):
                for j in range(tiles):
                    o_ref[0, g, j, 0:NSA_DK, :] = vt[g * NSA_DK:(g + 1) * NSA_DK,
                                                     j * Q_TILE:(j + 1) * Q_TILE].astype(BF16)
                    o_ref[0, g, j, NSA_DK:V_ROWS, :] = ones

        cmp_ref[:, 0:256] = proj(offs["kc"]).astype(BF16)
        cmp_ref[:, 256:512] = proj(offs["vc"]).astype(BF16)

        glt = proj(offs["gl"], LANES).T
        for g in range(NSA_G):
            glt_ref[0, g] = glt[16 * g:16 * (g + 1)]


def _nsa_proj(x2d, norm_g, w, head_gain, aug_sel, aug_win, pad_rows, offs, batch, seq):
    n = batch * seq
    nsb = seq // ROW_TILE
    tiles = ROW_TILE // Q_TILE
    qw = NSA_HEADS * NSA_DK
    kvw = NSA_G * NSA_DK
    bd = (np.arange(256)[:, None] // NSA_DK == np.arange(256)[None, :] // NSA_DK)
    bd = jnp.asarray(bd, BF16)
    prev = lambda s: jnp.maximum(s - 1, 0)
    rows = lambda b, s: (b * nsb + prev(s), 0)
    out_shape = [
        jax.ShapeDtypeStruct((n, qw), BF16), jax.ShapeDtypeStruct((n, qw), BF16),
        jax.ShapeDtypeStruct((n, 2 * kvw), BF16),
        jax.ShapeDtypeStruct((batch, NSA_G, seq, LANES), BF16),
        jax.ShapeDtypeStruct((batch, NSA_G, seq + WINDOW, LANES), BF16),
        jax.ShapeDtypeStruct((batch, NSA_G, seq // Q_TILE, V_ROWS, Q_TILE), BF16),
        jax.ShapeDtypeStruct((batch, NSA_G, (seq + WINDOW) // Q_TILE, V_ROWS, Q_TILE), BF16),
        jax.ShapeDtypeStruct((batch, NSA_G, 16, seq), F32),
    ]
    out_specs = [
        pl.BlockSpec((ROW_TILE, qw), rows), pl.BlockSpec((ROW_TILE, qw), rows),
        pl.BlockSpec((ROW_TILE, 2 * kvw), rows),
        pl.BlockSpec((1, NSA_G, ROW_TILE, LANES), lambda b, s: (b, 0, prev(s), 0)),
        pl.BlockSpec((1, NSA_G, ROW_TILE, LANES), lambda b, s: (b, 0, s, 0)),
        pl.BlockSpec((1, NSA_G, tiles, V_ROWS, Q_TILE), lambda b, s: (b, 0, prev(s), 0, 0)),
        pl.BlockSpec((1, NSA_G, tiles, V_ROWS, Q_TILE), lambda b, s: (b, 0, s, 0, 0)),
        pl.BlockSpec((1, NSA_G, 16, ROW_TILE), lambda b, s: (b, 0, 0, prev(s))),
    ]
    assert WINDOW == ROW_TILE
    return pl.pallas_call(
        functools.partial(_nsa_proj_kernel, offs=offs),
        grid=(batch, nsb + 1),
        in_specs=[
            pl.BlockSpec((ROW_TILE, D_MODEL), rows),
            _const_spec((1, D_MODEL)),
            _const_spec(w.shape),
            _const_spec(head_gain.shape),
            _const_spec((256, 256)),
            pl.BlockSpec((ROW_TILE, LANES), lambda b, s: (prev(s), 0)),
            pl.BlockSpec((ROW_TILE, LANES), lambda b, s: (prev(s), 0)),
            _const_spec(pad_rows.shape),
        ],
        out_specs=out_specs,
        out_shape=out_shape,
        compiler_params=pltpu.CompilerParams(
            dimension_semantics=("arbitrary", "arbitrary"), vmem_limit_bytes=VMEM_LIMIT),
        name="nsa_norm_proj",
    )(x2d, norm_g.reshape(1, D_MODEL), w, head_gain, bd, aug_sel, aug_win, pad_rows)


def _out_kernel(a_ref, wo_ref, x_ref, p_ref, wg_ref, wp_ref, o_ref):
    x1 = x_ref[...] + _dot(a_ref[...], wo_ref[...])
    gate = jax.nn.sigmoid(_dot(x1.astype(BF16), wg_ref[...]))
    emb = _dot(p_ref[...].astype(BF16), wp_ref[...])
    o_ref[...] = x1 + gate * emb


def _out_proj(a, w_out, x2d, p2d, w_gate, w_ple):
    n, k = a.shape
    return pl.pallas_call(
        _out_kernel,
        grid=(n // ROW_TILE,),
        in_specs=[
            pl.BlockSpec((ROW_TILE, k), lambda i: (i, 0)),
            _const_spec((k, D_MODEL)),
            pl.BlockSpec((ROW_TILE, D_MODEL), lambda i: (i, 0)),
            pl.BlockSpec((ROW_TILE, PLE_DIM), lambda i: (i, 0)),
            _const_spec((D_MODEL, D_MODEL)),
            _const_spec((PLE_DIM, D_MODEL)),
        ],
        out_specs=pl.BlockSpec((ROW_TILE, D_MODEL), lambda i: (i, 0)),
        out_shape=jax.ShapeDtypeStruct((n, D_MODEL), F32),
        compiler_params=pltpu.CompilerParams(
            dimension_semantics=("arbitrary",), vmem_limit_bytes=VMEM_LIMIT),
        name="out_proj_ple",
    )(a, w_out.astype(BF16), x2d, p2d, w_gate.astype(BF16), w_ple.astype(BF16))


def _cmp_kernel(xk_ref, xv_ref, w1k_ref, w1v_ref, posk_ref, posv_ref, w2k_ref, w2v_ref,
                gk_ref, kc_ref, vc_ref):
    n_sub = xk_ref.shape[2]
    half = xk_ref.shape[3]
    for is_k, x_ref, w1_ref, pos_ref, w2_ref, o_ref in (
            (True, xk_ref, w1k_ref, posk_ref, w2k_ref, kc_ref),
            (False, xv_ref, w1v_ref, posv_ref, w2v_ref, vc_ref)):
        w1 = w1_ref[...]
        x = x_ref[0].reshape(NSA_G * n_sub, half)
        ab = _dot(x, w1)
        pos = pos_ref[...]
        pos_term = (_dot(pos[:, :half], w1[:, :CMP_HIDDEN])
                    + _dot(pos[:, half:], w1[:, CMP_HIDDEN:]))[0:1]
        for g in range(NSA_G):
            first = ab[g * n_sub:(g + 1) * n_sub, :CMP_HIDDEN]
            second = ab[g * n_sub:(g + 1) * n_sub, CMP_HIDDEN:]
            hid = _silu(first + pltpu.roll(second, n_sub - 1, 0) + pos_term)
            c = _dot(hid.astype(BF16), w2_ref[...])
            if is_k:
                ss = jnp.sum(c * c, axis=-1, keepdims=True) * (1.0 / NSA_DK)
                c = c * lax.rsqrt(ss + RMS_EPS) * gk_ref[...]
                o_ref[0, g] = c.astype(BF16)
            else:
                ct = jnp.concatenate([c.T[:NSA_DK], jnp.ones((V_ROWS - NSA_DK, n_sub), F32)], axis=0)
                o_ref[0, g] = ct.astype(BF16)


def _compress(xk, xv, w1k, w1v, posk, posv, w2k, w2v, gk):
    b, g, n_sub, half = xk.shape
    blk = pl.BlockSpec((1, g, n_sub, half), lambda i: (i, 0, 0, 0))
    k_blk = pl.BlockSpec((1, g, n_sub, LANES), lambda i: (i, 0, 0, 0))
    k_sds = jax.ShapeDtypeStruct((b, g, n_sub, LANES), BF16)
    v_blk = pl.BlockSpec((1, g, V_ROWS, n_sub), lambda i: (i, 0, 0, 0))
    v_sds = jax.ShapeDtypeStruct((b, g, V_ROWS, n_sub), BF16)
    return pl.pallas_call(
        _cmp_kernel,
        grid=(b,),
        in_specs=[blk, blk,
                  _const_spec(w1k.shape), _const_spec(w1v.shape),
                  _const_spec(posk.shape), _const_spec(posv.shape),
                  _const_spec(w2k.shape), _const_spec(w2v.shape),
                  _const_spec(gk.shape)],
        out_specs=[k_blk, v_blk],
        out_shape=[k_sds, v_sds],
        compiler_params=pltpu.CompilerParams(
            dimension_semantics=("arbitrary",), vmem_limit_bytes=VMEM_LIMIT),
        name="nsa_compress",
    )(xk, xv, w1k, w1v, posk, posv, w2k, w2v, gk)


def _nsa_attn_kernel(fast_ref, q_ref, glt_ref, z_ref, ks_ref, vst_ref, kw_ref, vwt_ref, kc_ref, vct_ref,
                     ovt_ref, cthr_ref, kmq_ref, o_ref, acc_ref, accw_ref, rank_ref):
    tq = Q_TILE
    cols = NSA_HPG * tq
    qi = pl.program_id(2)
    t0 = qi * tq
    n_sel = SEL_BLOCK
    dk = NSA_DK

    qt = q_ref[...].astype(F32).T
    q_heads = [qt[h * dk:(h + 1) * dk] for h in range(NSA_HPG)]

    def stack_heads(extra_rows):
        return jnp.concatenate(
            [jnp.concatenate([qh, extra_rows], axis=0) for qh in q_heads], axis=1).astype(BF16)

    sc = _dot(kc_ref[0, 0], stack_heads(jnp.zeros((dk, tq), F32)))
    sc = jnp.where(cthr_ref[...] <= t0, sc, NEG)
    mcol = jnp.maximum(jnp.max(sc, axis=0, keepdims=True), 0.1 * NEG)
    ec = jnp.exp2(sc - mcol).astype(BF16)
    r = _dot(jnp.concatenate([vct_ref[0, 0], ovt_ref[...]], axis=0), ec)
    lc = r[dk:dk + 1]
    inv_lc = 1.0 / jnp.where(lc > 0.0, lc, 1.0)
    a_c = r[:dk] * inv_lc
    imp = r[V_ROWS:] * inv_lc
    imp_t = imp[:, 0:tq] + imp[:, tq:2 * tq] + imp[:, 2 * tq:3 * tq] + imp[:, 3 * tq:4 * tq]

    blk = lax.broadcasted_iota(jnp.int32, (n_sel, tq), 0)
    cur = (t0 + lax.broadcasted_iota(jnp.int32, (n_sel, tq), 1)) // SEL_BLOCK
    forced = (blk == 0) | (blk == cur) | (blk == cur - 1)
    score = jnp.where(blk <= cur, imp_t + jnp.where(forced, FORCE_BONUS, 0.0), NEG)
    sub = 8
    chunks = [score[c * sub:(c + 1) * sub] for c in range(n_sel // sub)]
    blk_sub = lax.broadcasted_iota(jnp.int32, (sub, tq), 0)
    last_blk = (t0 + tq - 1) // SEL_BLOCK
    rank_ref[...] = jnp.zeros_like(rank_ref)
    for jb in range(n_sel // sub):
        @pl.when(jb * sub <= last_blk)
        def _():
            ranks = [jnp.zeros((sub, tq), F32) for _ in chunks]
            for j in range(jb * sub, (jb + 1) * sub):
                row = jnp.broadcast_to(chunks[jb][j - jb * sub:j - jb * sub + 1, :], (sub, tq))
                for c, chunk in enumerate(chunks):
                    if c > jb:
                        one = jnp.where(row >= chunk, 1.0, 0.0)
                    elif c < jb:
                        one = jnp.where(row > chunk, 1.0, 0.0)
                    else:
                        one = jnp.where(blk_sub > j - c * sub,
                                        jnp.where(row >= chunk, 1.0, 0.0), jnp.where(row > chunk, 1.0, 0.0))
                    ranks[c] = ranks[c] + one
            rank_ref[...] += jnp.concatenate(ranks, axis=0)
    bias_t = jnp.where(rank_ref[...] < float(SEL_TOPN), 0.0, SEL_MASK_BIAS)
    q_aug = stack_heads(jnp.where(blk == 0, 1.0, bias_t))

    tiles_per_step = SEL_KEYS // tq
    n_full = t0 // SEL_KEYS
    n_wt = WINDOW // tq + 1
    diag_keep = kmq_ref[0:tq, :] <= 0

    def sel_scores(step, tail):
        base = pl.multiple_of(step * SEL_KEYS, SEL_KEYS)
        s = _dot(ks_ref[0, 0, pl.ds(base, SEL_KEYS), :], q_aug)
        if tail:
            s = jnp.where(kmq_ref[...] <= t0 - base, s, NEG)
        vt = vst_ref[0, 0, pl.ds(step * tiles_per_step, tiles_per_step)]
        return s, jnp.concatenate([vt[i] for i in range(tiles_per_step)], axis=1)

    def win_scores():
        sw = _dot(kw_ref[0, 0, pl.ds(pl.multiple_of(t0, tq), n_wt * tq), :], q_aug)
        parts = [jnp.where(diag_keep, NEG, sw[:tq])]
        if WINDOW > tq:
            parts.append(sw[tq:WINDOW])
        parts.append(jnp.where(diag_keep, sw[WINDOW:], NEG))
        sw = jnp.concatenate(parts, axis=0)
        vwt = vwt_ref[0, 0, pl.ds(qi, n_wt)]
        return sw, jnp.concatenate([vwt[i] for i in range(n_wt)], axis=1)

    fast = fast_ref[0] > 0

    @pl.when(fast)
    def _():
        acc_ref[...] = jnp.zeros_like(acc_ref)

        def step(i, carry):
            s, vt = sel_scores(i, False)
            acc_ref[...] += _dot(vt, jnp.exp2(s).astype(BF16))
            return carry

        lax.fori_loop(0, n_full, step, 0)
        s, vt = sel_scores(n_full, True)
        acc_ref[...] += _dot(vt, jnp.exp2(s).astype(BF16))
        sw, vwt = win_scores()
        accw_ref[...] = _dot(vwt, jnp.exp2(sw).astype(BF16))

    @pl.when(jnp.logical_not(fast))
    def _():
        acc_ref[...] = jnp.zeros_like(acc_ref)

        def online(s, vt, m):
            m_new = jnp.maximum(m, jnp.max(s, axis=0, keepdims=True))
            acc_ref[...] = jnp.exp2(m - m_new) * acc_ref[...] + _dot(vt, jnp.exp2(s - m_new).astype(BF16))
            return m_new

        m = jnp.full((1, cols), NEG, F32)
        m = lax.fori_loop(0, n_full, lambda i, m: online(*sel_scores(i, False), m), m)
        online(*sel_scores(n_full, True), m)
        sw, vwt = win_scores()
        accw_ref[...] = _dot(vwt, jnp.exp2(sw - jnp.max(sw, axis=0, keepdims=True)).astype(BF16))

    a_s = acc_ref[...]
    a_w = accw_ref[...]

    a_s = a_s[:dk] * (1.0 / a_s[dk:dk + 1])
    a_w = a_w[:dk] * (1.0 / a_w[dk:dk + 1])
    gates = jax.nn.sigmoid(glt_ref[0, 0])
    outs = []
    for h in range(NSA_HPG):
        sl = slice(h * tq, (h + 1) * tq)
        outs.append(gates[3 * h:3 * h + 1] * a_c[:, sl] + gates[3 * h + 1:3 * h + 2] * a_s[:, sl]
                    + gates[3 * h + 2:3 * h + 3] * a_w[:, sl])
    out = jnp.concatenate(outs, axis=0).T
    o_ref[...] = (out * _silu(z_ref[...].astype(F32))).astype(o_ref.dtype)


def _nsa_attention(fast, q, z, glt, ks_aug, vst, kw_pad, vwt, kc_pad, vct, overlap_t, batch, seq):
    cols = NSA_HPG * Q_TILE
    query = np.arange(cols)[None, :] % Q_TILE
    cmp_thr = jnp.asarray(np.arange(kc_pad.shape[2])[:, None] * CMP_STRIDE + (CMP_BLOCK - 1) - query, jnp.int32)
    key_minus_query = jnp.asarray(np.arange(SEL_KEYS)[:, None] - query, jnp.int32)
    n = batch * seq
    n_q = seq // Q_TILE
    gw = NSA_HPG * NSA_DK

    def per_group(arr):
        nd = arr.ndim - 2
        return pl.BlockSpec((1, 1) + arr.shape[2:], lambda b, g, i: (b, g) + (0,) * nd)

    return pl.pallas_call(
        _nsa_attn_kernel,
        grid=(batch, NSA_G, n_q),
        in_specs=[
            pl.BlockSpec(memory_space=pltpu.SMEM),
            pl.BlockSpec((Q_TILE, gw), lambda b, g, i: (b * n_q + i, g)),
            pl.BlockSpec((1, 1, glt.shape[2], Q_TILE), lambda b, g, i: (b, g, 0, i)),
            pl.BlockSpec((Q_TILE, gw), lambda b, g, i: (b * n_q + i, g)),
            per_group(ks_aug), per_group(vst), per_group(kw_pad), per_group(vwt),
            per_group(kc_pad), per_group(vct),
            _const_spec(overlap_t.shape), _const_spec(cmp_thr.shape), _const_spec(key_minus_query.shape),
        ],
        out_specs=pl.BlockSpec((Q_TILE, gw), lambda b, g, i: (b * n_q + i, g)),
        out_shape=jax.ShapeDtypeStruct((n, NSA_HEADS * NSA_DK), BF16),
        scratch_shapes=[pltpu.VMEM((V_ROWS, cols), F32)] * 2 + [pltpu.VMEM((SEL_BLOCK, Q_TILE), F32)],
        compiler_params=pltpu.CompilerParams(
            dimension_semantics=("arbitrary", "arbitrary", "arbitrary"),
            vmem_limit_bytes=VMEM_LIMIT),
        name="nsa_attention",
    )(fast, q, glt, z, ks_aug, vst, kw_pad, vwt, kc_pad, vct, overlap_t, cmp_thr, key_minus_query)


def _ret_kernel(q_ref, k_ref, v_ref, z_ref, dec_ref, qd_ref, kd_ref, cd_ref, o_ref, st_ref):
    @pl.when(pl.program_id(1) == 0)
    def _():
        st_ref[...] = jnp.zeros_like(st_ref)

    for h in range(RET_HEADS):
        q = q_ref[:, h * RET_DK:(h + 1) * RET_DK]
        k = k_ref[:, h * RET_DK:(h + 1) * RET_DK]
        v = v_ref[:, h * RET_DV:(h + 1) * RET_DV]
        att = _dot_nt(q, k) * dec_ref[h]
        state = st_ref[h]
        o = _dot(att.astype(BF16), v) + _dot(q, state.astype(BF16)) * qd_ref[h]
        kd_t = (k.astype(F32) * kd_ref[h]).T.astype(BF16)
        st_ref[h] = state * cd_ref[h] + _dot(kd_t, v)
        mu = jnp.mean(o, axis=-1, keepdims=True)
        d = o - mu
        var = jnp.mean(d * d, axis=-1, keepdims=True)
        z = z_ref[:, h * RET_DV:(h + 1) * RET_DV].astype(F32)
        o_ref[:, h * RET_DV:(h + 1) * RET_DV] = (d * lax.rsqrt(var + GN_EPS) * _silu(z)).astype(o_ref.dtype)


def _retention(pr, tables, batch, seq):
    n = batch * seq
    n_c = seq // RET_CHUNK
    dec, qd, kd, cd = tables
    qk_w = RET_HEADS * RET_DK
    v_w = RET_HEADS * RET_DV
    row = lambda b, c: b * n_c + c
    return pl.pallas_call(
        _ret_kernel,
        grid=(batch, n_c),
        in_specs=[
            pl.BlockSpec((RET_CHUNK, qk_w), lambda b, c: (row(b, c), 0)),
            pl.BlockSpec((RET_CHUNK, qk_w), lambda b, c: (row(b, c), 1)),
            pl.BlockSpec((RET_CHUNK, v_w), lambda b, c: (row(b, c), 1)),
            pl.BlockSpec((RET_CHUNK, v_w), lambda b, c: (row(b, c), 2)),
            _const_spec(dec.shape), _const_spec(qd.shape), _const_spec(kd.shape), _const_spec(cd.shape),
        ],
        out_specs=pl.BlockSpec((RET_CHUNK, v_w), lambda b, c: (row(b, c), 0)),
        out_shape=jax.ShapeDtypeStruct((n, v_w), BF16),
        scratch_shapes=[pltpu.VMEM((RET_HEADS, RET_DK, RET_DV), F32)],
        compiler_params=pltpu.CompilerParams(
            dimension_semantics=("arbitrary", "arbitrary"), vmem_limit_bytes=VMEM_LIMIT),
        name="retention",
    )(pr, pr, pr, pr, dec, qd, kd, cd)


def _overlap_matrix_t(n_cmp_pad):
    i = np.arange(n_cmp_pad)[None, :]
    j = np.arange(SEL_BLOCK)[:, None]
    ov = (i * CMP_STRIDE < (j + 1) * SEL_BLOCK) & (i * CMP_STRIDE + CMP_BLOCK > j * SEL_BLOCK)
    return jnp.asarray(ov, BF16)


def _rotary_tables(seq):
    half = RET_DK // 2
    inv = ROPE_BASE ** (-jnp.linspace(0.0, 1.0, half, dtype=F32))
    ang = jnp.arange(seq, dtype=F32)[:, None] * inv[None, :]
    return jnp.cos(ang), jnp.sin(ang)


def _retention_tables():
    c = RET_CHUNK
    log_g = jnp.log(1.0 - 2.0 ** (-5.0 - jnp.arange(RET_HEADS, dtype=F32)))
    ix = jnp.arange(c, dtype=F32)
    diff = ix[:, None] - ix[None, :]
    dec = jnp.where(diff >= 0, jnp.exp(log_g[:, None, None] * jnp.maximum(diff, 0.0)), 0.0)
    qd = jnp.exp(log_g[:, None] * (ix + 1.0))[:, :, None]
    kd = jnp.exp(log_g[:, None] * (c - 1.0 - ix))[:, :, None]
    cd = jnp.broadcast_to(jnp.exp(log_g * c)[:, None, None], (RET_HEADS, 1, RET_DV))
    return dec, qd, kd, cd


def _nsa_layer(x2d, p2d, batch, seq, norm_g, w_in, q_g, kc_g, ks_g, kw_g, pos_k, pos_v,
               ck_w1, ck_w2, cv_w1, cv_w2, w_out, ple_w, ple_gate_w):
    qw = NSA_HEADS * NSA_DK
    kvw = NSA_G * NSA_DK
    n_gate = 3 * NSA_HPG
    sizes = [qw] + [kvw] * 6 + [3 * NSA_HEADS, qw]
    splits = np.concatenate([[0], np.cumsum(sizes)])
    col = lambda i: w_in[:, splits[i]:splits[i + 1]]
    wq, wkc, wvc, wks, wvs, wkw, wvw, wgl, wz = [col(i) for i in range(9)]
    wgl = jnp.pad(wgl.reshape(D_MODEL, NSA_G, n_gate), ((0, 0), (0, 0), (0, 16 - n_gate)))
    wgl = jnp.pad(wgl.reshape(D_MODEL, NSA_G * 16), ((0, 0), (0, LANES - NSA_G * 16)))
    names = ["q", "ks", "kw", "vs", "vw", "z", "kc", "vc", "gl"]
    parts = [wq, wks, wkw, wvs, wvw, wz, wkc, wvc, wgl]
    starts = np.concatenate([[0], np.cumsum([p.shape[1] for p in parts])])
    offs = {nm: int(starts[i]) for i, nm in enumerate(names)}
    w = jnp.concatenate(parts, axis=1).astype(BF16)
    head_gain = jnp.concatenate([
        jnp.tile(q_g, NSA_HEADS) * (NSA_DK ** -0.5 * LOG2E), jnp.tile(ks_g, NSA_G), jnp.tile(kw_g, NSA_G),
        jnp.zeros((w.shape[1] - qw - 2 * kvw,), F32)]).reshape(1, -1)

    q_norm = LOG2E * jnp.max(jnp.abs(q_g)) * 1.02
    bound_sel = (q_norm * NSA_DK ** 0.5 * jnp.max(jnp.abs(ks_g))).astype(BF16).astype(F32) * 1.01
    bound_win = (q_norm * NSA_DK ** 0.5 * jnp.max(jnp.abs(kw_g))).astype(BF16).astype(F32) * 1.01
    fast = (jnp.maximum(bound_sel, bound_win) <= MAX_SAFE_BOUND).astype(jnp.int32).reshape(1)

    blk_id = np.arange(seq)[:, None] // SEL_BLOCK
    upper = np.arange(LANES)[None, :] - NSA_DK
    onehot = jnp.asarray((blk_id == upper) & (upper > 0), F32)
    lane64 = jnp.asarray(upper == 0)
    aug_sel = jnp.where(lane64, -bound_sel, onehot).astype(BF16)
    aug_win = jnp.where(lane64, -bound_win, jnp.zeros((seq, LANES), F32)).astype(BF16)
    pad_rows = jnp.where(lane64, SEL_MASK_BIAS, jnp.zeros((WINDOW, LANES), F32)).astype(BF16)

    q, z, cmp_in, ks_aug, kw_pad, vst, vwt, glt = _nsa_proj(
        x2d, norm_g, w, head_gain, aug_sel, aug_win, pad_rows, offs, batch, seq)

    n_sub = seq // CMP_STRIDE

    def sub_blocks(c0):
        a = cmp_in[:, c0:c0 + kvw].reshape(batch, n_sub, CMP_STRIDE, NSA_G, NSA_DK)
        return a.transpose(0, 3, 1, 2, 4).reshape(batch, NSA_G, n_sub, CMP_STRIDE * NSA_DK)

    half = CMP_STRIDE * NSA_DK
    w1cat = lambda w1: jnp.concatenate([w1[:half], w1[half:]], axis=1).astype(BF16)
    pos_flat = lambda pos: jnp.pad(pos.reshape(1, CMP_BLOCK * NSA_DK), ((0, 7), (0, 0))).astype(BF16)
    w2pad = lambda w2: jnp.pad(w2, ((0, 0), (0, LANES - NSA_DK))).astype(BF16)
    gk = jnp.pad(kc_g, (0, LANES - NSA_DK)).reshape(1, LANES)
    kc_pad, vct = _compress(sub_blocks(0), sub_blocks(kvw), w1cat(ck_w1), w1cat(cv_w1),
                            pos_flat(pos_k), pos_flat(pos_v), w2pad(ck_w2), w2pad(cv_w2), gk)

    a = _nsa_attention(fast, q, z, glt, ks_aug, vst, kw_pad, vwt, kc_pad, vct,
                       _overlap_matrix_t(n_sub), batch, seq)
    return _out_proj(a, w_out, x2d, p2d, ple_gate_w, ple_w)


def _ret_layer(x2d, p2d, batch, seq, norm_g, w_in, w_out, ple_w, ple_gate_w):
    cos, sin = _rotary_tables(seq)
    pr = _ret_proj(x2d, norm_g, w_in.astype(BF16), cos, sin, seq)
    a = _retention(pr, _retention_tables(), batch, seq)
    return _out_proj(a, w_out, x2d, p2d, ple_gate_w, ple_w)


def kernel(x, p, norm_g, nsa_w_in, nsa_q_g, nsa_kc_g, nsa_ks_g, nsa_kw_g, nsa_cmp_pos_k, nsa_cmp_pos_v, nsa_cmp_k_w1, nsa_cmp_k_w2, nsa_cmp_v_w1, nsa_cmp_v_w2, nsa_w_out, ret_w_in, ret_w_out, ple_w, ple_gate_w):
    batch, seq, d_model = x.shape
    depth = p.shape[0]
    n = batch * seq
    x2d = x.reshape(n, d_model)
    for i in range(depth):
        p2d = p[i].reshape(n, PLE_DIM)
        j = i // 2
        if i % 2 == 0:
            x2d = _nsa_layer(x2d, p2d, batch, seq, norm_g[i], nsa_w_in[j], nsa_q_g[j], nsa_kc_g[j],
                             nsa_ks_g[j], nsa_kw_g[j], nsa_cmp_pos_k[j], nsa_cmp_pos_v[j],
                             nsa_cmp_k_w1[j], nsa_cmp_k_w2[j], nsa_cmp_v_w1[j], nsa_cmp_v_w2[j],
                             nsa_w_out[j], ple_w[i], ple_gate_w[i])
        else:
            x2d = _ret_layer(x2d, p2d, batch, seq, norm_g[i], ret_w_in[j], ret_w_out[j],
                             ple_w[i], ple_gate_w[i])
    return x2d.reshape(batch, seq, d_model)
```

```python
import functools

import numpy as np
import jax
import jax.numpy as jnp
from jax import lax
from jax.experimental import pallas as pl
from jax.experimental.pallas import tpu as pltpu

F32 = jnp.float32
BF16 = jnp.bfloat16

D_MODEL = 1024
PLE_DIM = 256
RMS_EPS = 1e-6
GN_EPS = 1e-5

NSA_HEADS = 16
NSA_DK = 64
NSA_G = 4
NSA_HPG = 4
CMP_BLOCK = 32
CMP_STRIDE = 16
CMP_HIDDEN = 256
SEL_BLOCK = 64
SEL_TOPN = 16
WINDOW = 512
FORCE_BONUS = 1e4
NEG = -1e30
SEL_MASK_BIAS = -30000.0

RET_HEADS = 4
RET_DK = 256
RET_DV = 512
RET_CHUNK = 128
ROPE_BASE = 10000.0

LANES = 128
VMEM_LIMIT = 56 * 1024 * 1024

ROW_TILE = 512
Q_TILE = 256
SEL_KEYS = 512
LOG2E = 1.4426950408889634
MAX_SAFE_BOUND = 50.0
V_ROWS = NSA_DK + 16


def _dot(a, b):
    return jnp.dot(a, b, preferred_element_type=F32)


def _dot_nt(a, b):
    return lax.dot_general(a, b, (((1,), (1,)), ((), ())), preferred_element_type=F32)


def _silu(x):
    return x * jax.nn.sigmoid(x)


def _const_spec(shape):
    nd = len(shape)
    return pl.BlockSpec(shape, lambda *_: (0,) * nd)


def _ret_proj_kernel(x_ref, g_ref, w_ref, cos_ref, sin_ref, o_ref):
    x = x_ref[...]
    h = (x * lax.rsqrt(jnp.mean(x * x, axis=-1, keepdims=True) + RMS_EPS) * g_ref[...]).astype(BF16)
    cos, sin = cos_ref[...], sin_ref[...]
    half = RET_DK // 2
    qk_w = RET_HEADS * RET_DK
    for t in range(2 * RET_HEADS):
        c0 = t * RET_DK
        acc = _dot(h, w_ref[:, c0:c0 + RET_DK])
        a1, a2 = acc[:, :half], acc[:, half:]
        rot = jnp.concatenate([a1 * cos - a2 * sin, a1 * sin + a2 * cos], axis=-1)
        if c0 >= qk_w:
            rot = rot * RET_DK ** -0.5
        o_ref[:, c0:c0 + RET_DK] = rot.astype(o_ref.dtype)
    v_w = RET_HEADS * RET_DV
    for c0 in range(2 * qk_w, w_ref.shape[1], RET_DV):
        acc = _dot(h, w_ref[:, c0:c0 + RET_DV])
        o_ref[:, c0:c0 + RET_DV] = (acc if c0 < 2 * qk_w + v_w else _silu(acc)).astype(o_ref.dtype)


def _ret_proj(x2d, norm_g, w, cos, sin, seq):
    n = x2d.shape[0]
    cols = w.shape[1]
    n_pos_blocks = seq // ROW_TILE
    return pl.pallas_call(
        _ret_proj_kernel,
        grid=(n // ROW_TILE,),
        in_specs=[
            pl.BlockSpec((ROW_TILE, D_MODEL), lambda i: (i, 0)),
            _const_spec((1, D_MODEL)),
            _const_spec((D_MODEL, cols)),
            pl.BlockSpec((ROW_TILE, LANES), lambda i: (i % n_pos_blocks, 0)),
            pl.BlockSpec((ROW_TILE, LANES), lambda i: (i % n_pos_blocks, 0)),
        ],
        out_specs=pl.BlockSpec((ROW_TILE, cols), lambda i: (i, 0)),
        out_shape=jax.ShapeDtypeStruct((n, cols), BF16),
        compiler_params=pltpu.CompilerParams(
            dimension_semantics=("arbitrary",), vmem_limit_bytes=VMEM_LIMIT),
        name="ret_norm_proj",
    )(x2d, norm_g.reshape(1, D_MODEL), w, cos, sin)


def _nsa_proj_kernel(x_ref, g_ref, w_ref, hg_ref, bd_ref, augs_ref, augw_ref, pad_ref,
                     q_ref, z_ref, kcs_ref, vcs_ref, ks_ref, kw_ref, vst_ref, vwt_ref, glt_ref, cmp_scr, *, offs):
    s = pl.program_id(1)
    tiles = ROW_TILE // Q_TILE

    @pl.when(s == 0)
    def _():
        for g in range(NSA_G):
            kw_ref[0, g] = pad_ref[...]
        vwt_ref[...] = jnp.zeros_like(vwt_ref)

    @pl.when(s > 0)
    def _():
        x = x_ref[...]
        h = (x * lax.rsqrt(jnp.mean(x * x, axis=-1, keepdims=True) + RMS_EPS) * g_ref[...]).astype(BF16)
        low = lax.broadcasted_iota(jnp.int32, (ROW_TILE, LANES), 1) < NSA_DK

        def proj(c0, width=256):
            return _dot(h, w_ref[:, c0:c0 + width])

        def head_norm(acc, c0):
            ss = _dot((acc * acc).astype(BF16), bd_ref[...])
            return acc * lax.rsqrt(ss * (1.0 / NSA_DK) + RMS_EPS) * hg_ref[:, c0:c0 + 256]

        for t in range(NSA_HEADS * NSA_DK // 256):
            c0 = offs["q"] + 256 * t
            q_ref[:, 256 * t:256 * (t + 1)] = head_norm(proj(c0), c0).astype(BF16)
            z_ref[:, 256 * t:256 * (t + 1)] = proj(offs["z"] + 256 * t).astype(BF16)

        kvw = NSA_G * NSA_DK
        keys = proj(offs["ks"], 2 * kvw)
        for i, (name, aug_ref, o_ref) in enumerate((("ks", augs_ref, ks_ref), ("kw", augw_ref, kw_ref))):
            k = head_norm(keys[:, i * kvw:(i + 1) * kvw], offs[name])
            aug = aug_ref[...].astype(F32)
            for pair in range(NSA_G // 2):
                two = k[:, LANES * pair:LANES * (pair + 1)]
                o_ref[0, 2 * pair] = jnp.where(low, two, aug).astype(BF16)
                o_ref[0, 2 * pair + 1] = jnp.where(low, pltpu.roll(two, NSA_DK, 1), aug).astype(BF16)

        ones = jnp.ones((V_ROWS - NSA_DK, Q_TILE), BF16)
        values = proj(offs["vs"], 2 * kvw)
        for i, o_ref in enumerate((vst_ref, vwt_ref)):
            vt = values[:, i * kvw:(i + 1) * kvw].T
            for g in range(NSA_G):
                for j in range(tiles):
                    o_ref[0, g, j, 0:NSA_DK, :] = vt[g * NSA_DK:(g + 1) * NSA_DK,
                                                     j * Q_TILE:(j + 1) * Q_TILE].astype(BF16)
                    o_ref[0, g, j, NSA_DK:V_ROWS, :] = ones

        cmp_in = proj(offs["kc"], 2 * kvw)
        for t in range(2 * kvw // LANES):
            cmp_scr[t] = cmp_in[:, LANES * t:LANES * (t + 1)]
        n_sub = ROW_TILE // CMP_STRIDE
        low_sub = lax.broadcasted_iota(jnp.int32, (n_sub, LANES), 1) < NSA_DK
        for r in range(0, CMP_STRIDE, 2):
            for i, o_ref in enumerate((kcs_ref, vcs_ref)):
                for pair in range(NSA_G // 2):
                    t = i * (NSA_G // 2) + pair
                    e2 = cmp_scr[t, pl.ds(r, n_sub, stride=CMP_STRIDE), :]
                    o2 = cmp_scr[t, pl.ds(r + 1, n_sub, stride=CMP_STRIDE), :]
                    lanes = slice(LANES * (r // 2), LANES * (r // 2 + 1))
                    o_ref[0, 2 * pair, :, lanes] = jnp.where(low_sub, e2, pltpu.roll(o2, NSA_DK, 1)).astype(BF16)
                    o_ref[0, 2 * pair + 1, :, lanes] = jnp.where(low_sub, pltpu.roll(e2, NSA_DK, 1), o2).astype(BF16)

        glt = proj(offs["gl"], LANES).T
        for g in range(NSA_G):
            glt_ref[0, g] = glt[16 * g:16 * (g + 1)]


def _nsa_proj(x2d, norm_g, w, head_gain, aug_sel, aug_win, pad_rows, offs, batch, seq):
    n = batch * seq
    nsb = seq // ROW_TILE
    tiles = ROW_TILE // Q_TILE
    qw = NSA_HEADS * NSA_DK
    kvw = NSA_G * NSA_DK
    sub_w = CMP_STRIDE * NSA_DK
    bd = (np.arange(256)[:, None] // NSA_DK == np.arange(256)[None, :] // NSA_DK)
    bd = jnp.asarray(bd, BF16)
    prev = lambda s: jnp.maximum(s - 1, 0)
    rows = lambda b, s: (b * nsb + prev(s), 0)
    out_shape = [
        jax.ShapeDtypeStruct((n, qw), BF16), jax.ShapeDtypeStruct((n, qw), BF16),
        jax.ShapeDtypeStruct((batch, NSA_G, seq // CMP_STRIDE, sub_w), BF16),
        jax.ShapeDtypeStruct((batch, NSA_G, seq // CMP_STRIDE, sub_w), BF16),
        jax.ShapeDtypeStruct((batch, NSA_G, seq, LANES), BF16),
        jax.ShapeDtypeStruct((batch, NSA_G, seq + WINDOW, LANES), BF16),
        jax.ShapeDtypeStruct((batch, NSA_G, seq // Q_TILE, V_ROWS, Q_TILE), BF16),
        jax.ShapeDtypeStruct((batch, NSA_G, (seq + WINDOW) // Q_TILE, V_ROWS, Q_TILE), BF16),
        jax.ShapeDtypeStruct((batch, NSA_G, 16, seq), F32),
    ]
    out_specs = [
        pl.BlockSpec((ROW_TILE, qw), rows), pl.BlockSpec((ROW_TILE, qw), rows),
        pl.BlockSpec((1, NSA_G, ROW_TILE // CMP_STRIDE, sub_w), lambda b, s: (b, 0, prev(s), 0)),
        pl.BlockSpec((1, NSA_G, ROW_TILE // CMP_STRIDE, sub_w), lambda b, s: (b, 0, prev(s), 0)),
        pl.BlockSpec((1, NSA_G, ROW_TILE, LANES), lambda b, s: (b, 0, prev(s), 0)),
        pl.BlockSpec((1, NSA_G, ROW_TILE, LANES), lambda b, s: (b, 0, s, 0)),
        pl.BlockSpec((1, NSA_G, tiles, V_ROWS, Q_TILE), lambda b, s: (b, 0, prev(s), 0, 0)),
        pl.BlockSpec((1, NSA_G, tiles, V_ROWS, Q_TILE), lambda b, s: (b, 0, s, 0, 0)),
        pl.BlockSpec((1, NSA_G, 16, ROW_TILE), lambda b, s: (b, 0, 0, prev(s))),
    ]
    assert WINDOW == ROW_TILE
    return pl.pallas_call(
        functools.partial(_nsa_proj_kernel, offs=offs),
        grid=(batch, nsb + 1),
        in_specs=[
            pl.BlockSpec((ROW_TILE, D_MODEL), rows),
            _const_spec((1, D_MODEL)),
            _const_spec(w.shape),
            _const_spec(head_gain.shape),
            _const_spec((256, 256)),
            pl.BlockSpec((ROW_TILE, LANES), lambda b, s: (prev(s), 0)),
            pl.BlockSpec((ROW_TILE, LANES), lambda b, s: (prev(s), 0)),
            _const_spec(pad_rows.shape),
        ],
        out_specs=out_specs,
        out_shape=out_shape,
        scratch_shapes=[pltpu.VMEM((2 * kvw // LANES, ROW_TILE, LANES), F32)],
        compiler_params=pltpu.CompilerParams(
            dimension_semantics=("arbitrary", "arbitrary"), vmem_limit_bytes=VMEM_LIMIT),
        name="nsa_norm_proj",
    )(x2d, norm_g.reshape(1, D_MODEL), w, head_gain, bd, aug_sel, aug_win, pad_rows)


def _out_kernel(a_ref, wo_ref, x_ref, p_ref, wg_ref, wp_ref, o_ref):
    x1 = x_ref[...] + _dot(a_ref[...], wo_ref[...])
    gate = jax.nn.sigmoid(_dot(x1.astype(BF16), wg_ref[...]))
    emb = _dot(p_ref[...].astype(BF16), wp_ref[...])
    o_ref[...] = x1 + gate * emb


def _out_proj(a, w_out, x2d, p2d, w_gate, w_ple):
    n, k = a.shape
    return pl.pallas_call(
        _out_kernel,
        grid=(n // ROW_TILE,),
        in_specs=[
            pl.BlockSpec((ROW_TILE, k), lambda i: (i, 0)),
            _const_spec((k, D_MODEL)),
            pl.BlockSpec((ROW_TILE, D_MODEL), lambda i: (i, 0)),
            pl.BlockSpec((ROW_TILE, PLE_DIM), lambda i: (i, 0)),
            _const_spec((D_MODEL, D_MODEL)),
            _const_spec((PLE_DIM, D_MODEL)),
        ],
        out_specs=pl.BlockSpec((ROW_TILE, D_MODEL), lambda i: (i, 0)),
        out_shape=jax.ShapeDtypeStruct((n, D_MODEL), F32),
        compiler_params=pltpu.CompilerParams(
            dimension_semantics=("arbitrary",), vmem_limit_bytes=VMEM_LIMIT),
        name="out_proj_ple",
    )(a, w_out.astype(BF16), x2d, p2d, w_gate.astype(BF16), w_ple.astype(BF16))


def _cmp_kernel(xk_ref, xv_ref, w1k_ref, w1v_ref, posk_ref, posv_ref, w2k_ref, w2v_ref,
                gk_ref, kc_ref, vc_ref):
    n_sub = xk_ref.shape[2]
    half = xk_ref.shape[3]
    for is_k, x_ref, w1_ref, pos_ref, w2_ref, o_ref in (
            (True, xk_ref, w1k_ref, posk_ref, w2k_ref, kc_ref),
            (False, xv_ref, w1v_ref, posv_ref, w2v_ref, vc_ref)):
        w1 = w1_ref[...]
        x = x_ref[0].reshape(NSA_G * n_sub, half)
        ab = _dot(x, w1)
        pos = pos_ref[...]
        pos_term = (_dot(pos[:, :half], w1[:, :CMP_HIDDEN])
                    + _dot(pos[:, half:], w1[:, CMP_HIDDEN:]))[0:1]
        for g in range(NSA_G):
            first = ab[g * n_sub:(g + 1) * n_sub, :CMP_HIDDEN]
            second = ab[g * n_sub:(g + 1) * n_sub, CMP_HIDDEN:]
            hid = _silu(first + pltpu.roll(second, n_sub - 1, 0) + pos_term)
            c = _dot(hid.astype(BF16), w2_ref[...])
            if is_k:
                ss = jnp.sum(c * c, axis=-1, keepdims=True) * (1.0 / NSA_DK)
                c = c * lax.rsqrt(ss + RMS_EPS) * gk_ref[...]
                o_ref[0, g] = c.astype(BF16)
            else:
                ct = jnp.concatenate([c.T[:NSA_DK], jnp.ones((V_ROWS - NSA_DK, n_sub), F32)], axis=0)
                o_ref[0, g] = ct.astype(BF16)


def _compress(xk, xv, w1k, w1v, posk, posv, w2k, w2v, gk):
    b, g, n_sub, half = xk.shape
    blk = pl.BlockSpec((1, g, n_sub, half), lambda i: (i, 0, 0, 0))
    k_blk = pl.BlockSpec((1, g, n_sub, LANES), lambda i: (i, 0, 0, 0))
    k_sds = jax.ShapeDtypeStruct((b, g, n_sub, LANES), BF16)
    v_blk = pl.BlockSpec((1, g, V_ROWS, n_sub), lambda i: (i, 0, 0, 0))
    v_sds = jax.ShapeDtypeStruct((b, g, V_ROWS, n_sub), BF16)
    return pl.pallas_call(
        _cmp_kernel,
        grid=(b,),
        in_specs=[blk, blk,
                  _const_spec(w1k.shape), _const_spec(w1v.shape),
                  _const_spec(posk.shape), _const_spec(posv.shape),
                  _const_spec(w2k.shape), _const_spec(w2v.shape),
                  _const_spec(gk.shape)],
        out_specs=[k_blk, v_blk],
        out_shape=[k_sds, v_sds],
        compiler_params=pltpu.CompilerParams(
            dimension_semantics=("arbitrary",), vmem_limit_bytes=VMEM_LIMIT),
        name="nsa_compress",
    )(xk, xv, w1k, w1v, posk, posv, w2k, w2v, gk)


def _nsa_attn_kernel(fast_ref, q_ref, glt_ref, z_ref, ks_ref, vst_ref, kw_ref, vwt_ref, kc_ref, vct_ref,
                     ovt_ref, cthr_ref, kmq_ref, o_ref, acc_ref, accw_ref, rank_ref, p0_ref, p1_ref):
    tq = Q_TILE
    cols = NSA_HPG * tq
    qi = pl.program_id(2)
    t0 = qi * tq
    n_sel = SEL_BLOCK
    dk = NSA_DK

    qt = q_ref[...].astype(F32).T
    q_heads = [qt[h * dk:(h + 1) * dk] for h in range(NSA_HPG)]

    def stack_heads(extra_rows):
        return jnp.concatenate(
            [jnp.concatenate([qh, extra_rows], axis=0) for qh in q_heads], axis=1).astype(BF16)

    sc = _dot(kc_ref[0, 0], stack_heads(jnp.zeros((dk, tq), F32)))
    sc = jnp.where(cthr_ref[...] <= t0, sc, NEG)
    mcol = jnp.maximum(jnp.max(sc, axis=0, keepdims=True), 0.1 * NEG)
    ec = jnp.exp2(sc - mcol).astype(BF16)
    r = _dot(jnp.concatenate([vct_ref[0, 0], ovt_ref[...]], axis=0), ec)
    lc = r[dk:dk + 1]
    inv_lc = 1.0 / jnp.where(lc > 0.0, lc, 1.0)
    a_c = r[:dk] * inv_lc
    imp = r[V_ROWS:] * inv_lc
    imp_t = imp[:, 0:tq] + imp[:, tq:2 * tq] + imp[:, 2 * tq:3 * tq] + imp[:, 3 * tq:4 * tq]

    blk = lax.broadcasted_iota(jnp.int32, (n_sel, tq), 0)
    cur = (t0 + lax.broadcasted_iota(jnp.int32, (n_sel, tq), 1)) // SEL_BLOCK
    forced = (blk == 0) | (blk == cur) | (blk == cur - 1)
    score = jnp.where(blk <= cur, imp_t + jnp.where(forced, FORCE_BONUS, 0.0), NEG)
    sub = 8
    chunks = [score[c * sub:(c + 1) * sub] for c in range(n_sel // sub)]
    blk_sub = lax.broadcasted_iota(jnp.int32, (sub, tq), 0)
    last_blk = (t0 + tq - 1) // SEL_BLOCK
    rank_ref[...] = jnp.zeros_like(rank_ref)
    for jb in range(n_sel // sub):
        @pl.when(jb * sub <= last_blk)
        def _():
            ranks = [jnp.zeros((sub, tq), F32) for _ in chunks]
            for j in range(jb * sub, (jb + 1) * sub):
                row = jnp.broadcast_to(chunks[jb][j - jb * sub:j - jb * sub + 1, :], (sub, tq))
                for c, chunk in enumerate(chunks):
                    if c > jb:
                        one = jnp.where(row >= chunk, 1.0, 0.0)
                    elif c < jb:
                        one = jnp.where(row > chunk, 1.0, 0.0)
                    else:
                        one = jnp.where(blk_sub > j - c * sub,
                                        jnp.where(row >= chunk, 1.0, 0.0), jnp.where(row > chunk, 1.0, 0.0))
                    ranks[c] = ranks[c] + one
            rank_ref[...] += jnp.concatenate(ranks, axis=0)
    bias_t = jnp.where(rank_ref[...] < float(SEL_TOPN), 0.0, SEL_MASK_BIAS)
    q_aug = stack_heads(jnp.where(blk == 0, 1.0, bias_t))

    tiles_per_step = SEL_KEYS // tq
    n_full = t0 // SEL_KEYS
    n_wt = WINDOW // tq + 1
    diag_keep = kmq_ref[0:tq, :] <= 0

    def sel_scores(step, tail):
        base = pl.multiple_of(step * SEL_KEYS, SEL_KEYS)
        s = _dot(ks_ref[0, 0, pl.ds(base, SEL_KEYS), :], q_aug)
        if tail:
            s = jnp.where(kmq_ref[...] <= t0 - base, s, NEG)
        vt = vst_ref[0, 0, pl.ds(step * tiles_per_step, tiles_per_step)]
        return s, jnp.concatenate([vt[i] for i in range(tiles_per_step)], axis=1)

    def win_scores():
        sw = _dot(kw_ref[0, 0, pl.ds(pl.multiple_of(t0, tq), n_wt * tq), :], q_aug)
        parts = [jnp.where(diag_keep, NEG, sw[:tq])]
        if WINDOW > tq:
            parts.append(sw[tq:WINDOW])
        parts.append(jnp.where(diag_keep, sw[WINDOW:], NEG))
        sw = jnp.concatenate(parts, axis=0)
        vwt = vwt_ref[0, 0, pl.ds(qi, n_wt)]
        return sw, jnp.concatenate([vwt[i] for i in range(n_wt)], axis=1)

    fast = fast_ref[0] > 0

    @pl.when(fast)
    def _():
        acc_ref[...] = jnp.zeros_like(acc_ref)

        def values(step):
            vt = vst_ref[0, 0, pl.ds(step * tiles_per_step, tiles_per_step)]
            return jnp.concatenate([vt[i] for i in range(tiles_per_step)], axis=1)

        def probs(step, tail):
            return jnp.exp2(sel_scores(step, tail)[0]).astype(BF16)

        odd = n_full % 2
        p1_ref[...] = probs(n_full, True)

        @pl.when(odd == 1)
        def _():
            p = probs(n_full - 1, False)
            acc_ref[...] += _dot(values(n_full), p1_ref[...])
            p1_ref[...] = p

        first_pending = n_full - odd

        def pair(j, carry):
            pending = jnp.where(j == 0, first_pending, 2 * j - 1)
            pa = probs(2 * j, False)
            acc_ref[...] += _dot(values(pending), p1_ref[...])
            p0_ref[...] = pa
            pb = probs(2 * j + 1, False)
            acc_ref[...] += _dot(values(2 * j), p0_ref[...])
            p1_ref[...] = pb
            return carry

        n_pairs = n_full // 2
        lax.fori_loop(0, n_pairs, pair, 0)
        pending = jnp.where(n_pairs == 0, first_pending, 2 * n_pairs - 1)
        sw, vwt = win_scores()
        pw = jnp.exp2(sw).astype(BF16)
        acc_ref[...] += _dot(values(pending), p1_ref[...])
        accw_ref[...] = _dot(vwt, pw)

    @pl.when(jnp.logical_not(fast))
    def _():
        acc_ref[...] = jnp.zeros_like(acc_ref)

        def online(s, vt, m):
            m_new = jnp.maximum(m, jnp.max(s, axis=0, keepdims=True))
            acc_ref[...] = jnp.exp2(m - m_new) * acc_ref[...] + _dot(vt, jnp.exp2(s - m_new).astype(BF16))
            return m_new

        m = jnp.full((1, cols), NEG, F32)
        m = lax.fori_loop(0, n_full, lambda i, m: online(*sel_scores(i, False), m), m)
        online(*sel_scores(n_full, True), m)
        sw, vwt = win_scores()
        accw_ref[...] = _dot(vwt, jnp.exp2(sw - jnp.max(sw, axis=0, keepdims=True)).astype(BF16))

    a_s = acc_ref[...]
    a_w = accw_ref[...]

    a_s = a_s[:dk] * (1.0 / a_s[dk:dk + 1])
    a_w = a_w[:dk] * (1.0 / a_w[dk:dk + 1])
    gates = jax.nn.sigmoid(glt_ref[0, 0])
    outs = []
    for h in range(NSA_HPG):
        sl = slice(h * tq, (h + 1) * tq)
        outs.append(gates[3 * h:3 * h + 1] * a_c[:, sl] + gates[3 * h + 1:3 * h + 2] * a_s[:, sl]
                    + gates[3 * h + 2:3 * h + 3] * a_w[:, sl])
    out = jnp.concatenate(outs, axis=0).T
    o_ref[...] = (out * _silu(z_ref[...].astype(F32))).astype(o_ref.dtype)


def _nsa_attention(fast, q, z, glt, ks_aug, vst, kw_pad, vwt, kc_pad, vct, overlap_t, batch, seq):
    cols = NSA_HPG * Q_TILE
    query = np.arange(cols)[None, :] % Q_TILE
    cmp_thr = jnp.asarray(np.arange(kc_pad.shape[2])[:, None] * CMP_STRIDE + (CMP_BLOCK - 1) - query, jnp.int32)
    key_minus_query = jnp.asarray(np.arange(SEL_KEYS)[:, None] - query, jnp.int32)
    n = batch * seq
    n_q = seq // Q_TILE
    gw = NSA_HPG * NSA_DK

    def per_group(arr):
        nd = arr.ndim - 2
        return pl.BlockSpec((1, 1) + arr.shape[2:], lambda b, g, i: (b, g) + (0,) * nd)

    return pl.pallas_call(
        _nsa_attn_kernel,
        grid=(batch, NSA_G, n_q),
        in_specs=[
            pl.BlockSpec(memory_space=pltpu.SMEM),
            pl.BlockSpec((Q_TILE, gw), lambda b, g, i: (b * n_q + i, g)),
            pl.BlockSpec((1, 1, glt.shape[2], Q_TILE), lambda b, g, i: (b, g, 0, i)),
            pl.BlockSpec((Q_TILE, gw), lambda b, g, i: (b * n_q + i, g)),
            per_group(ks_aug), per_group(vst), per_group(kw_pad), per_group(vwt),
            per_group(kc_pad), per_group(vct),
            _const_spec(overlap_t.shape), _const_spec(cmp_thr.shape), _const_spec(key_minus_query.shape),
        ],
        out_specs=pl.BlockSpec((Q_TILE, gw), lambda b, g, i: (b * n_q + i, g)),
        out_shape=jax.ShapeDtypeStruct((n, NSA_HEADS * NSA_DK), BF16),
        scratch_shapes=([pltpu.VMEM((V_ROWS, cols), F32)] * 2 + [pltpu.VMEM((SEL_BLOCK, Q_TILE), F32)]
                        + [pltpu.VMEM((SEL_KEYS, cols), BF16)] * 2),
        compiler_params=pltpu.CompilerParams(
            dimension_semantics=("arbitrary", "arbitrary", "arbitrary"),
            vmem_limit_bytes=VMEM_LIMIT),
        name="nsa_attention",
    )(fast, q, glt, z, ks_aug, vst, kw_pad, vwt, kc_pad, vct, overlap_t, cmp_thr, key_minus_query)


def _ret_kernel(q_ref, k_ref, v_ref, z_ref, dec_ref, qd_ref, kd_ref, cd_ref, o_ref, st_ref):
    @pl.when(pl.program_id(1) == 0)
    def _():
        st_ref[...] = jnp.zeros_like(st_ref)

    for h in range(RET_HEADS):
        q = q_ref[:, h * RET_DK:(h + 1) * RET_DK]
        k = k_ref[:, h * RET_DK:(h + 1) * RET_DK]
        v = v_ref[:, h * RET_DV:(h + 1) * RET_DV]
        att = _dot_nt(q, k) * dec_ref[h]
        state = st_ref[h]
        o = _dot(att.astype(BF16), v) + _dot(q, state.astype(BF16)) * qd_ref[h]
        kd_t = (k.astype(F32) * kd_ref[h]).T.astype(BF16)
        st_ref[h] = state * cd_ref[h] + _dot(kd_t, v)
        mu = jnp.mean(o, axis=-1, keepdims=True)
        d = o - mu
        var = jnp.mean(d * d, axis=-1, keepdims=True)
        gate = z_ref[:, h * RET_DV:(h + 1) * RET_DV].astype(F32)
        o_ref[:, h * RET_DV:(h + 1) * RET_DV] = (d * lax.rsqrt(var + GN_EPS) * gate).astype(o_ref.dtype)


def _retention(pr, tables, batch, seq):
    n = batch * seq
    n_c = seq // RET_CHUNK
    dec, qd, kd, cd = tables
    qk_w = RET_HEADS * RET_DK
    v_w = RET_HEADS * RET_DV
    row = lambda b, c: b * n_c + c
    return pl.pallas_call(
        _ret_kernel,
        grid=(batch, n_c),
        in_specs=[
            pl.BlockSpec((RET_CHUNK, qk_w), lambda b, c: (row(b, c), 0)),
            pl.BlockSpec((RET_CHUNK, qk_w), lambda b, c: (row(b, c), 1)),
            pl.BlockSpec((RET_CHUNK, v_w), lambda b, c: (row(b, c), 1)),
            pl.BlockSpec((RET_CHUNK, v_w), lambda b, c: (row(b, c), 2)),
            _const_spec(dec.shape), _const_spec(qd.shape), _const_spec(kd.shape), _const_spec(cd.shape),
        ],
        out_specs=pl.BlockSpec((RET_CHUNK, v_w), lambda b, c: (row(b, c), 0)),
        out_shape=jax.ShapeDtypeStruct((n, v_w), BF16),
        scratch_shapes=[pltpu.VMEM((RET_HEADS, RET_DK, RET_DV), F32)],
        compiler_params=pltpu.CompilerParams(
            dimension_semantics=("arbitrary", "arbitrary"), vmem_limit_bytes=VMEM_LIMIT),
        name="retention",
    )(pr, pr, pr, pr, dec, qd, kd, cd)


def _overlap_matrix_t(n_cmp_pad):
    i = np.arange(n_cmp_pad)[None, :]
    j = np.arange(SEL_BLOCK)[:, None]
    ov = (i * CMP_STRIDE < (j + 1) * SEL_BLOCK) & (i * CMP_STRIDE + CMP_BLOCK > j * SEL_BLOCK)
    return jnp.asarray(ov, BF16)


def _rotary_tables(seq):
    half = RET_DK // 2
    inv = ROPE_BASE ** (-jnp.linspace(0.0, 1.0, half, dtype=F32))
    ang = jnp.arange(seq, dtype=F32)[:, None] * inv[None, :]
    return jnp.cos(ang), jnp.sin(ang)


def _retention_tables():
    c = RET_CHUNK
    log_g = jnp.log(1.0 - 2.0 ** (-5.0 - jnp.arange(RET_HEADS, dtype=F32)))
    ix = jnp.arange(c, dtype=F32)
    diff = ix[:, None] - ix[None, :]
    dec = jnp.where(diff >= 0, jnp.exp(log_g[:, None, None] * jnp.maximum(diff, 0.0)), 0.0)
    qd = jnp.exp(log_g[:, None] * (ix + 1.0))[:, :, None]
    kd = jnp.exp(log_g[:, None] * (c - 1.0 - ix))[:, :, None]
    cd = jnp.broadcast_to(jnp.exp(log_g * c)[:, None, None], (RET_HEADS, 1, RET_DV))
    return dec, qd, kd, cd


def _nsa_layer(x2d, p2d, batch, seq, norm_g, w_in, q_g, kc_g, ks_g, kw_g, pos_k, pos_v,
               ck_w1, ck_w2, cv_w1, cv_w2, w_out, ple_w, ple_gate_w):
    qw = NSA_HEADS * NSA_DK
    kvw = NSA_G * NSA_DK
    n_gate = 3 * NSA_HPG
    sizes = [qw] + [kvw] * 6 + [3 * NSA_HEADS, qw]
    splits = np.concatenate([[0], np.cumsum(sizes)])
    col = lambda i: w_in[:, splits[i]:splits[i + 1]]
    wq, wkc, wvc, wks, wvs, wkw, wvw, wgl, wz = [col(i) for i in range(9)]
    wgl = jnp.pad(wgl.reshape(D_MODEL, NSA_G, n_gate), ((0, 0), (0, 0), (0, 16 - n_gate)))
    wgl = jnp.pad(wgl.reshape(D_MODEL, NSA_G * 16), ((0, 0), (0, LANES - NSA_G * 16)))
    names = ["q", "ks", "kw", "vs", "vw", "z", "kc", "vc", "gl"]
    parts = [wq, wks, wkw, wvs, wvw, wz, wkc, wvc, wgl]
    starts = np.concatenate([[0], np.cumsum([p.shape[1] for p in parts])])
    offs = {nm: int(starts[i]) for i, nm in enumerate(names)}
    w = jnp.concatenate(parts, axis=1).astype(BF16)
    head_gain = jnp.concatenate([
        jnp.tile(q_g, NSA_HEADS) * (NSA_DK ** -0.5 * LOG2E), jnp.tile(ks_g, NSA_G), jnp.tile(kw_g, NSA_G),
        jnp.zeros((w.shape[1] - qw - 2 * kvw,), F32)]).reshape(1, -1)

    q_norm = LOG2E * jnp.max(jnp.abs(q_g)) * 1.02
    bound_sel = (q_norm * NSA_DK ** 0.5 * jnp.max(jnp.abs(ks_g))).astype(BF16).astype(F32) * 1.01
    bound_win = (q_norm * NSA_DK ** 0.5 * jnp.max(jnp.abs(kw_g))).astype(BF16).astype(F32) * 1.01
    fast = (jnp.maximum(bound_sel, bound_win) <= MAX_SAFE_BOUND).astype(jnp.int32).reshape(1)

    blk_id = np.arange(seq)[:, None] // SEL_BLOCK
    upper = np.arange(LANES)[None, :] - NSA_DK
    onehot = jnp.asarray((blk_id == upper) & (upper > 0), F32)
    lane64 = jnp.asarray(upper == 0)
    aug_sel = jnp.where(lane64, -bound_sel, onehot).astype(BF16)
    aug_win = jnp.where(lane64, -bound_win, jnp.zeros((seq, LANES), F32)).astype(BF16)
    pad_rows = jnp.where(lane64, SEL_MASK_BIAS, jnp.zeros((WINDOW, LANES), F32)).astype(BF16)

    q, z, kc_sub, vc_sub, ks_aug, kw_pad, vst, vwt, glt = _nsa_proj(
        x2d, norm_g, w, head_gain, aug_sel, aug_win, pad_rows, offs, batch, seq)

    n_sub = seq // CMP_STRIDE
    half = CMP_STRIDE * NSA_DK
    w1cat = lambda w1: jnp.concatenate([w1[:half], w1[half:]], axis=1).astype(BF16)
    pos_flat = lambda pos: jnp.pad(pos.reshape(1, CMP_BLOCK * NSA_DK), ((0, 7), (0, 0))).astype(BF16)
    w2pad = lambda w2: jnp.pad(w2, ((0, 0), (0, LANES - NSA_DK))).astype(BF16)
    gk = jnp.pad(kc_g, (0, LANES - NSA_DK)).reshape(1, LANES)
    kc_pad, vct = _compress(kc_sub, vc_sub, w1cat(ck_w1), w1cat(cv_w1),
                            pos_flat(pos_k), pos_flat(pos_v), w2pad(ck_w2), w2pad(cv_w2), gk)

    a = _nsa_attention(fast, q, z, glt, ks_aug, vst, kw_pad, vwt, kc_pad, vct,
                       _overlap_matrix_t(n_sub), batch, seq)
    return _out_proj(a, w_out, x2d, p2d, ple_gate_w, ple_w)


def _ret_layer(x2d, p2d, batch, seq, norm_g, w_in, w_out, ple_w, ple_gate_w):
    cos, sin = _rotary_tables(seq)
    pr = _ret_proj(x2d, norm_g, w_in.astype(BF16), cos, sin, seq)
    a = _retention(pr, _retention_tables(), batch, seq)
    return _out_proj(a, w_out, x2d, p2d, ple_gate_w, ple_w)


def kernel(x, p, norm_g, nsa_w_in, nsa_q_g, nsa_kc_g, nsa_ks_g, nsa_kw_g, nsa_cmp_pos_k, nsa_cmp_pos_v, nsa_cmp_k_w1, nsa_cmp_k_w2, nsa_cmp_v_w1, nsa_cmp_v_w2, nsa_w_out, ret_w_in, ret_w_out, ple_w, ple_gate_w):
    batch, seq, d_model = x.shape
    depth = p.shape[0]
    n = batch * seq
    x2d = x.reshape(n, d_model)
    for i in range(depth):
        p2d = p[i].reshape(n, PLE_DIM)
        j = i // 2
        if i % 2 == 0:
            x2d = _nsa_layer(x2d, p2d, batch, seq, norm_g[i], nsa_w_in[j], nsa_q_g[j], nsa_kc_g[j],
                             nsa_ks_g[j], nsa_kw_g[j], nsa_cmp_pos_k[j], nsa_cmp_pos_v[j],
                             nsa_cmp_k_w1[j], nsa_cmp_k_w2[j], nsa_cmp_v_w1[j], nsa_cmp_v_w2[j],
                             nsa_w_out[j], ple_w[i], ple_gate_w[i])
        else:
            x2d = _ret_layer(x2d, p2d, batch, seq, norm_g[i], ret_w_in[j], ret_w_out[j],
                             ple_w[i], ple_gate_w[i])
    return x2d.reshape(batch, seq, d_model)
```

```python
import functools

import numpy as np
import jax
import jax.numpy as jnp
from jax import lax
from jax.experimental import pallas as pl
from jax.experimental.pallas import tpu as pltpu

F32 = jnp.float32
BF16 = jnp.bfloat16

D_MODEL = 1024
PLE_DIM = 256
RMS_EPS = 1e-6
GN_EPS = 1e-5

NSA_HEADS = 16
NSA_DK = 64
NSA_G = 4
NSA_HPG = 4
CMP_BLOCK = 32
CMP_STRIDE = 16
CMP_HIDDEN = 256
SEL_BLOCK = 64
SEL_TOPN = 16
WINDOW = 512
FORCE_BONUS = 1e4
NEG = -1e30
SEL_MASK_BIAS = -30000.0

RET_HEADS = 4
RET_DK = 256
RET_DV = 512
RET_CHUNK = 256
ROPE_BASE = 10000.0

LANES = 128
VMEM_LIMIT = 56 * 1024 * 1024

ROW_TILE = 512
Q_TILE = 256
SEL_KEYS = 512
LOG2E = 1.4426950408889634
MAX_SAFE_BOUND = 50.0
V_ROWS = NSA_DK + 16


def _dot(a, b):
    return jnp.dot(a, b, preferred_element_type=F32)


def _dot_nt(a, b):
    return lax.dot_general(a, b, (((1,), (1,)), ((), ())), preferred_element_type=F32)


def _silu(x):
    return x * jax.nn.sigmoid(x)


def _const_spec(shape):
    nd = len(shape)
    return pl.BlockSpec(shape, lambda *_: (0,) * nd)


def _ret_proj_kernel(x_ref, g_ref, w_ref, cos_ref, sin_ref, o_ref):
    x = x_ref[...]
    h = (x * lax.rsqrt(jnp.mean(x * x, axis=-1, keepdims=True) + RMS_EPS) * g_ref[...]).astype(BF16)
    cos, sin = cos_ref[...], sin_ref[...]
    half = RET_DK // 2
    qk_w = RET_HEADS * RET_DK
    for t in range(2 * RET_HEADS):
        c0 = t * RET_DK
        acc = _dot(h, w_ref[:, c0:c0 + RET_DK])
        a1, a2 = acc[:, :half], acc[:, half:]
        rot = jnp.concatenate([a1 * cos - a2 * sin, a1 * sin + a2 * cos], axis=-1)
        if c0 >= qk_w:
            rot = rot * RET_DK ** -0.5
        o_ref[:, c0:c0 + RET_DK] = rot.astype(o_ref.dtype)
    for c0 in range(2 * qk_w, w_ref.shape[1], RET_DV):
        o_ref[:, c0:c0 + RET_DV] = _dot(h, w_ref[:, c0:c0 + RET_DV]).astype(o_ref.dtype)


def _ret_proj(x2d, norm_g, w, cos, sin, seq):
    n = x2d.shape[0]
    cols = w.shape[1]
    n_pos_blocks = seq // ROW_TILE
    return pl.pallas_call(
        _ret_proj_kernel,
        grid=(n // ROW_TILE,),
        in_specs=[
            pl.BlockSpec((ROW_TILE, D_MODEL), lambda i: (i, 0)),
            _const_spec((1, D_MODEL)),
            _const_spec((D_MODEL, cols)),
            pl.BlockSpec((ROW_TILE, LANES), lambda i: (i % n_pos_blocks, 0)),
            pl.BlockSpec((ROW_TILE, LANES), lambda i: (i % n_pos_blocks, 0)),
        ],
        out_specs=pl.BlockSpec((ROW_TILE, cols), lambda i: (i, 0)),
        out_shape=jax.ShapeDtypeStruct((n, cols), BF16),
        compiler_params=pltpu.CompilerParams(
            dimension_semantics=("arbitrary",), vmem_limit_bytes=VMEM_LIMIT),
        name="ret_norm_proj",
    )(x2d, norm_g.reshape(1, D_MODEL), w, cos, sin)


def _nsa_proj_kernel(x_ref, g_ref, w_ref, hg_ref, bd_ref, augs_ref, augw_ref, pad_ref,
                     q_ref, z_ref, kcs_ref, vcs_ref, ks_ref, kw_ref, vst_ref, vwt_ref, glt_ref, cmp_scr, *, offs):
    s = pl.program_id(1)
    tiles = ROW_TILE // Q_TILE

    @pl.when(s == 0)
    def _():
        for g in range(NSA_G):
            kw_ref[0, g] = pad_ref[...]
        vwt_ref[...] = jnp.zeros_like(vwt_ref)

    @pl.when(s > 0)
    def _():
        x = x_ref[...]
        h = (x * lax.rsqrt(jnp.mean(x * x, axis=-1, keepdims=True) + RMS_EPS) * g_ref[...]).astype(BF16)
        low = lax.broadcasted_iota(jnp.int32, (ROW_TILE, LANES), 1) < NSA_DK

        def proj(c0, width=256):
            return _dot(h, w_ref[:, c0:c0 + width])

        def head_norm(acc, c0):
            ss = _dot((acc * acc).astype(BF16), bd_ref[...])
            return acc * lax.rsqrt(ss * (1.0 / NSA_DK) + RMS_EPS) * hg_ref[:, c0:c0 + 256]

        for t in range(NSA_HEADS * NSA_DK // 256):
            c0 = offs["q"] + 256 * t
            q_ref[:, 256 * t:256 * (t + 1)] = head_norm(proj(c0), c0).astype(BF16)
            z_ref[:, 256 * t:256 * (t + 1)] = proj(offs["z"] + 256 * t).astype(BF16)

        kvw = NSA_G * NSA_DK
        keys = proj(offs["ks"], 2 * kvw)
        for i, (name, aug_ref, o_ref) in enumerate((("ks", augs_ref, ks_ref), ("kw", augw_ref, kw_ref))):
            k = head_norm(keys[:, i * kvw:(i + 1) * kvw], offs[name])
            aug = aug_ref[...].astype(F32)
            for pair in range(NSA_G // 2):
                two = k[:, LANES * pair:LANES * (pair + 1)]
                o_ref[0, 2 * pair] = jnp.where(low, two, aug).astype(BF16)
                o_ref[0, 2 * pair + 1] = jnp.where(low, pltpu.roll(two, NSA_DK, 1), aug).astype(BF16)

        ones = jnp.ones((V_ROWS - NSA_DK, Q_TILE), BF16)
        values = proj(offs["vs"], 2 * kvw)
        for i, o_ref in enumerate((vst_ref, vwt_ref)):
            vt = values[:, i * kvw:(i + 1) * kvw].T
            for g in range(NSA_G):
                for j in range(tiles):
                    o_ref[0, g, j, 0:NSA_DK, :] = vt[g * NSA_DK:(g + 1) * NSA_DK,
                                                     j * Q_TILE:(j + 1) * Q_TILE].astype(BF16)
                    o_ref[0, g, j, NSA_DK:V_ROWS, :] = ones

        cmp_in = proj(offs["kc"], 2 * kvw)
        for t in range(2 * kvw // LANES):
            cmp_scr[t] = cmp_in[:, LANES * t:LANES * (t + 1)]
        n_sub = ROW_TILE // CMP_STRIDE
        low_sub = lax.broadcasted_iota(jnp.int32, (n_sub, LANES), 1) < NSA_DK
        for r in range(0, CMP_STRIDE, 2):
            for i, o_ref in enumerate((kcs_ref, vcs_ref)):
                for pair in range(NSA_G // 2):
                    t = i * (NSA_G // 2) + pair
                    e2 = cmp_scr[t, pl.ds(r, n_sub, stride=CMP_STRIDE), :]
                    o2 = cmp_scr[t, pl.ds(r + 1, n_sub, stride=CMP_STRIDE), :]
                    lanes = slice(LANES * (r // 2), LANES * (r // 2 + 1))
                    o_ref[0, 2 * pair, :, lanes] = jnp.where(low_sub, e2, pltpu.roll(o2, NSA_DK, 1)).astype(BF16)
                    o_ref[0, 2 * pair + 1, :, lanes] = jnp.where(low_sub, pltpu.roll(e2, NSA_DK, 1), o2).astype(BF16)

        glt = proj(offs["gl"], LANES).T
        for g in range(NSA_G):
            glt_ref[0, g] = glt[16 * g:16 * (g + 1)]


def _nsa_proj(x2d, norm_g, w, head_gain, aug_sel, aug_win, pad_rows, offs, batch, seq):
    n = batch * seq
    nsb = seq // ROW_TILE
    tiles = ROW_TILE // Q_TILE
    qw = NSA_HEADS * NSA_DK
    kvw = NSA_G * NSA_DK
    sub_w = CMP_STRIDE * NSA_DK
    bd = (np.arange(256)[:, None] // NSA_DK == np.arange(256)[None, :] // NSA_DK)
    bd = jnp.asarray(bd, BF16)
    prev = lambda s: jnp.maximum(s - 1, 0)
    rows = lambda b, s: (b * nsb + prev(s), 0)
    out_shape = [
        jax.ShapeDtypeStruct((n, qw), BF16), jax.ShapeDtypeStruct((n, qw), BF16),
        jax.ShapeDtypeStruct((batch, NSA_G, seq // CMP_STRIDE, sub_w), BF16),
        jax.ShapeDtypeStruct((batch, NSA_G, seq // CMP_STRIDE, sub_w), BF16),
        jax.ShapeDtypeStruct((batch, NSA_G, seq, LANES), BF16),
        jax.ShapeDtypeStruct((batch, NSA_G, seq + WINDOW, LANES), BF16),
        jax.ShapeDtypeStruct((batch, NSA_G, seq // Q_TILE, V_ROWS, Q_TILE), BF16),
        jax.ShapeDtypeStruct((batch, NSA_G, (seq + WINDOW) // Q_TILE, V_ROWS, Q_TILE), BF16),
        jax.ShapeDtypeStruct((batch, NSA_G, 16, seq), F32),
    ]
    out_specs = [
        pl.BlockSpec((ROW_TILE, qw), rows), pl.BlockSpec((ROW_TILE, qw), rows),
        pl.BlockSpec((1, NSA_G, ROW_TILE // CMP_STRIDE, sub_w), lambda b, s: (b, 0, prev(s), 0)),
        pl.BlockSpec((1, NSA_G, ROW_TILE // CMP_STRIDE, sub_w), lambda b, s: (b, 0, prev(s), 0)),
        pl.BlockSpec((1, NSA_G, ROW_TILE, LANES), lambda b, s: (b, 0, prev(s), 0)),
        pl.BlockSpec((1, NSA_G, ROW_TILE, LANES), lambda b, s: (b, 0, s, 0)),
        pl.BlockSpec((1, NSA_G, tiles, V_ROWS, Q_TILE), lambda b, s: (b, 0, prev(s), 0, 0)),
        pl.BlockSpec((1, NSA_G, tiles, V_ROWS, Q_TILE), lambda b, s: (b, 0, s, 0, 0)),
        pl.BlockSpec((1, NSA_G, 16, ROW_TILE), lambda b, s: (b, 0, 0, prev(s))),
    ]
    assert WINDOW == ROW_TILE
    return pl.pallas_call(
        functools.partial(_nsa_proj_kernel, offs=offs),
        grid=(batch, nsb + 1),
        in_specs=[
            pl.BlockSpec((ROW_TILE, D_MODEL), rows),
            _const_spec((1, D_MODEL)),
            _const_spec(w.shape),
            _const_spec(head_gain.shape),
            _const_spec((256, 256)),
            pl.BlockSpec((ROW_TILE, LANES), lambda b, s: (prev(s), 0)),
            pl.BlockSpec((ROW_TILE, LANES), lambda b, s: (prev(s), 0)),
            _const_spec(pad_rows.shape),
        ],
        out_specs=out_specs,
        out_shape=out_shape,
        scratch_shapes=[pltpu.VMEM((2 * kvw // LANES, ROW_TILE, LANES), F32)],
        compiler_params=pltpu.CompilerParams(
            dimension_semantics=("arbitrary", "arbitrary"), vmem_limit_bytes=VMEM_LIMIT),
        name="nsa_norm_proj",
    )(x2d, norm_g.reshape(1, D_MODEL), w, head_gain, bd, aug_sel, aug_win, pad_rows)


def _out_kernel(a_ref, wo_ref, x_ref, p_ref, wg_ref, wp_ref, o_ref):
    x1 = x_ref[...] + _dot(a_ref[...], wo_ref[...])
    gate = jax.nn.sigmoid(_dot(x1.astype(BF16), wg_ref[...]))
    emb = _dot(p_ref[...].astype(BF16), wp_ref[...])
    o_ref[...] = x1 + gate * emb


def _out_proj(a, w_out, x2d, p2d, w_gate, w_ple):
    n, k = a.shape
    return pl.pallas_call(
        _out_kernel,
        grid=(n // ROW_TILE,),
        in_specs=[
            pl.BlockSpec((ROW_TILE, k), lambda i: (i, 0)),
            _const_spec((k, D_MODEL)),
            pl.BlockSpec((ROW_TILE, D_MODEL), lambda i: (i, 0)),
            pl.BlockSpec((ROW_TILE, PLE_DIM), lambda i: (i, 0)),
            _const_spec((D_MODEL, D_MODEL)),
            _const_spec((PLE_DIM, D_MODEL)),
        ],
        out_specs=pl.BlockSpec((ROW_TILE, D_MODEL), lambda i: (i, 0)),
        out_shape=jax.ShapeDtypeStruct((n, D_MODEL), F32),
        compiler_params=pltpu.CompilerParams(
            dimension_semantics=("arbitrary",), vmem_limit_bytes=VMEM_LIMIT),
        name="out_proj_ple",
    )(a, w_out.astype(BF16), x2d, p2d, w_gate.astype(BF16), w_ple.astype(BF16))


def _cmp_kernel(xk_ref, xv_ref, w1k_ref, w1v_ref, posk_ref, posv_ref, w2k_ref, w2v_ref,
                gk_ref, shift_ref, kc_ref, vc_ref):
    n_sub = xk_ref.shape[2]
    half = xk_ref.shape[3]
    for is_k, x_ref, w1_ref, pos_ref, w2_ref, o_ref in (
            (True, xk_ref, w1k_ref, posk_ref, w2k_ref, kc_ref),
            (False, xv_ref, w1v_ref, posv_ref, w2v_ref, vc_ref)):
        w1 = w1_ref[...]
        x = x_ref[0].reshape(NSA_G * n_sub, half)
        ab = _dot(x, w1)
        pos = pos_ref[...]
        pos_term = (_dot(pos[:, :half], w1[:, :CMP_HIDDEN])
                    + _dot(pos[:, half:], w1[:, CMP_HIDDEN:]))[0:1]
        for g in range(NSA_G):
            first = ab[g * n_sub:(g + 1) * n_sub, :CMP_HIDDEN]
            second = ab[g * n_sub:(g + 1) * n_sub, CMP_HIDDEN:]
            hid = _silu(first + pltpu.roll(second, n_sub - 1, 0) + pos_term)
            c = _dot(hid.astype(BF16), w2_ref[...])
            if is_k:
                ss = jnp.sum(c * c, axis=-1, keepdims=True) * (1.0 / NSA_DK)
                c = c * lax.rsqrt(ss + RMS_EPS) * gk_ref[...] + shift_ref[...]
                o_ref[0, g] = c.astype(BF16)
            else:
                ct = jnp.concatenate([c.T[:NSA_DK], jnp.ones((V_ROWS - NSA_DK, n_sub), F32)], axis=0)
                o_ref[0, g] = ct.astype(BF16)


def _compress(xk, xv, w1k, w1v, posk, posv, w2k, w2v, gk, k_shift):
    b, g, n_sub, half = xk.shape
    blk = pl.BlockSpec((1, g, n_sub, half), lambda i: (i, 0, 0, 0))
    k_blk = pl.BlockSpec((1, g, n_sub, LANES), lambda i: (i, 0, 0, 0))
    k_sds = jax.ShapeDtypeStruct((b, g, n_sub, LANES), BF16)
    v_blk = pl.BlockSpec((1, g, V_ROWS, n_sub), lambda i: (i, 0, 0, 0))
    v_sds = jax.ShapeDtypeStruct((b, g, V_ROWS, n_sub), BF16)
    return pl.pallas_call(
        _cmp_kernel,
        grid=(b,),
        in_specs=[blk, blk,
                  _const_spec(w1k.shape), _const_spec(w1v.shape),
                  _const_spec(posk.shape), _const_spec(posv.shape),
                  _const_spec(w2k.shape), _const_spec(w2v.shape),
                  _const_spec(gk.shape), _const_spec(k_shift.shape)],
        out_specs=[k_blk, v_blk],
        out_shape=[k_sds, v_sds],
        compiler_params=pltpu.CompilerParams(
            dimension_semantics=("arbitrary",), vmem_limit_bytes=VMEM_LIMIT),
        name="nsa_compress",
    )(xk, xv, w1k, w1v, posk, posv, w2k, w2v, gk, k_shift)


def _nsa_attn_kernel(fast_ref, q_ref, glt_ref, z_ref, ks_ref, vst_ref, kw_ref, vwt_ref, kc_ref, vct_ref,
                     ovt_ref, cthr_ref, kmq_ref, o_ref, acc_ref, accw_ref, rank_ref, p0_ref, p1_ref, ac_ref, impt_ref):
    tq = Q_TILE
    cols = NSA_HPG * tq
    qi = pl.program_id(2)
    t0 = qi * tq
    n_sel = SEL_BLOCK
    dk = NSA_DK

    qt = q_ref[...].astype(F32).T
    q_heads = [qt[h * dk:(h + 1) * dk] for h in range(NSA_HPG)]

    def stack_heads(extra_rows):
        return jnp.concatenate(
            [jnp.concatenate([qh, extra_rows], axis=0) for qh in q_heads], axis=1).astype(BF16)

    blk = lax.broadcasted_iota(jnp.int32, (n_sel, tq), 0)
    ones_row = jnp.where(blk == 0, 1.0, 0.0)
    fast = fast_ref[0] > 0

    def compressed(subtract_max):
        sc = _dot(kc_ref[0, 0], stack_heads(ones_row))
        sc = jnp.where(cthr_ref[...] <= t0, sc, NEG)
        if subtract_max:
            sc = sc - jnp.maximum(jnp.max(sc, axis=0, keepdims=True), 0.1 * NEG)
        ec = jnp.exp2(sc).astype(BF16)
        r = _dot(jnp.concatenate([vct_ref[0, 0], ovt_ref[...]], axis=0), ec)
        lc = r[dk:dk + 1]
        inv_lc = 1.0 / jnp.where(lc > 0.0, lc, 1.0)
        ac_ref[...] = r[:dk] * inv_lc
        imp = r[V_ROWS:] * inv_lc
        impt_ref[...] = imp[:, 0:tq] + imp[:, tq:2 * tq] + imp[:, 2 * tq:3 * tq] + imp[:, 3 * tq:4 * tq]

    pl.when(fast)(lambda: compressed(False))
    pl.when(jnp.logical_not(fast))(lambda: compressed(True))
    imp_t = impt_ref[...]

    cur =(t0 + lax.broadcasted_iota(jnp.int32, (n_sel, tq), 1)) // SEL_BLOCK
    forced = (blk == 0) | (blk == cur) | (blk == cur - 1)
    score = jnp.where(blk <= cur, imp_t + jnp.where(forced, FORCE_BONUS, 0.0), NEG)
    sub = 8
    chunks = [score[c * sub:(c + 1) * sub] for c in range(n_sel // sub)]
    blk_sub = lax.broadcasted_iota(jnp.int32, (sub, tq), 0)
    last_blk = (t0 + tq - 1) // SEL_BLOCK
    rank_ref[...] = jnp.zeros_like(rank_ref)
    for jb in range(n_sel // sub):
        @pl.when(jb * sub <= last_blk)
        def _():
            ranks = [jnp.zeros((sub, tq), F32) for _ in chunks]
            for j in range(jb * sub, (jb + 1) * sub):
                row = jnp.broadcast_to(chunks[jb][j - jb * sub:j - jb * sub + 1, :], (sub, tq))
                for c, chunk in enumerate(chunks):
                    if c > jb:
                        one = jnp.where(row >= chunk, 1.0, 0.0)
                    elif c < jb:
                        one = jnp.where(row > chunk, 1.0, 0.0)
                    else:
                        one = jnp.where(blk_sub > j - c * sub,
                                        jnp.where(row >= chunk, 1.0, 0.0), jnp.where(row > chunk, 1.0, 0.0))
                    ranks[c] = ranks[c] + one
            rank_ref[...] += jnp.concatenate(ranks, axis=0)
    bias_t = jnp.where(rank_ref[...] < float(SEL_TOPN), 0.0, SEL_MASK_BIAS)
    q_aug = stack_heads(jnp.where(blk == 0, 1.0, bias_t))

    tiles_per_step = SEL_KEYS // tq
    n_full = t0 // SEL_KEYS
    n_wt = WINDOW // tq + 1
    diag_keep = kmq_ref[0:tq, :] <= 0

    def sel_scores(step, tail):
        base = pl.multiple_of(step * SEL_KEYS, SEL_KEYS)
        s = _dot(ks_ref[0, 0, pl.ds(base, SEL_KEYS), :], q_aug)
        if tail:
            s = jnp.where(kmq_ref[...] <= t0 - base, s, NEG)
        vt = vst_ref[0, 0, pl.ds(step * tiles_per_step, tiles_per_step)]
        return s, jnp.concatenate([vt[i] for i in range(tiles_per_step)], axis=1)

    def win_scores():
        sw = _dot(kw_ref[0, 0, pl.ds(pl.multiple_of(t0, tq), n_wt * tq), :], q_aug)
        parts = [jnp.where(diag_keep, NEG, sw[:tq])]
        if WINDOW > tq:
            parts.append(sw[tq:WINDOW])
        parts.append(jnp.where(diag_keep, sw[WINDOW:], NEG))
        sw = jnp.concatenate(parts, axis=0)
        vwt = vwt_ref[0, 0, pl.ds(qi, n_wt)]
        return sw, jnp.concatenate([vwt[i] for i in range(n_wt)], axis=1)

    @pl.when(fast)
    def _():
        acc_ref[...] = jnp.zeros_like(acc_ref)

        def values(step):
            vt = vst_ref[0, 0, pl.ds(step * tiles_per_step, tiles_per_step)]
            return jnp.concatenate([vt[i] for i in range(tiles_per_step)], axis=1)

        def probs(step, tail):
            return jnp.exp2(sel_scores(step, tail)[0]).astype(BF16)

        odd = n_full % 2
        half = SEL_KEYS // 2
        assert half == tq

        @pl.when(t0 % SEL_KEYS != 0)
        def _():
            p1_ref[...] = probs(n_full, True)

        @pl.when(t0 % SEL_KEYS == 0)
        def _():
            s = _dot(ks_ref[0, 0, pl.ds(pl.multiple_of(t0, SEL_KEYS), half), :], q_aug)
            p1_ref[0:half, :] = jnp.exp2(jnp.where(diag_keep, s, NEG)).astype(BF16)
            p1_ref[half:, :] = jnp.zeros((half, cols), BF16)

        @pl.when(odd == 1)
        def _():
            p = probs(n_full - 1, False)
            acc_ref[...] += _dot(values(n_full), p1_ref[...])
            p1_ref[...] = p

        first_pending = n_full - odd

        def pair(j, carry):
            pending = jnp.where(j == 0, first_pending, 2 * j - 1)
            pa = probs(2 * j, False)
            acc_ref[...] += _dot(values(pending), p1_ref[...])
            p0_ref[...] = pa
            pb = probs(2 * j + 1, False)
            acc_ref[...] += _dot(values(2 * j), p0_ref[...])
            p1_ref[...] = pb
            return carry

        n_pairs = n_full // 2
        lax.fori_loop(0, n_pairs, pair, 0)
        pending = jnp.where(n_pairs == 0, first_pending, 2 * n_pairs - 1)
        sw, vwt = win_scores()
        pw = jnp.exp2(sw).astype(BF16)
        acc_ref[...] += _dot(values(pending), p1_ref[...])
        accw_ref[...] = _dot(vwt, pw)

    @pl.when(jnp.logical_not(fast))
    def _():
        acc_ref[...] = jnp.zeros_like(acc_ref)

        def online(s, vt, m):
            m_new = jnp.maximum(m, jnp.max(s, axis=0, keepdims=True))
            acc_ref[...] = jnp.exp2(m - m_new) * acc_ref[...] + _dot(vt, jnp.exp2(s - m_new).astype(BF16))
            return m_new

        m = jnp.full((1, cols), NEG, F32)
        m = lax.fori_loop(0, n_full, lambda i, m: online(*sel_scores(i, False), m), m)
        online(*sel_scores(n_full, True), m)
        sw, vwt = win_scores()
        accw_ref[...] = _dot(vwt, jnp.exp2(sw - jnp.max(sw, axis=0, keepdims=True)).astype(BF16))

    a_s = acc_ref[...]
    a_w = accw_ref[...]

    a_s = a_s[:dk] * (1.0 / a_s[dk:dk + 1])
    a_w = a_w[:dk] * (1.0 / a_w[dk:dk + 1])
    gates = jax.nn.sigmoid(glt_ref[0, 0])
    outs = []
    for h in range(NSA_HPG):
        sl = slice(h * tq, (h + 1) * tq)
        outs.append(gates[3 * h:3 * h + 1] * ac_ref[:, sl] + gates[3 * h + 1:3 * h + 2] * a_s[:, sl]
                    + gates[3 * h + 2:3 * h + 3] * a_w[:, sl])
    out = jnp.concatenate(outs, axis=0).T
    o_ref[...] = (out * _silu(z_ref[...].astype(F32))).astype(o_ref.dtype)


def _nsa_attention(fast, q, z, glt, ks_aug, vst, kw_pad, vwt, kc_pad, vct, overlap_t, batch, seq):
    cols = NSA_HPG * Q_TILE
    query = np.arange(cols)[None, :] % Q_TILE
    cmp_thr = jnp.asarray(np.arange(kc_pad.shape[2])[:, None] * CMP_STRIDE + (CMP_BLOCK - 1) - query, jnp.int32)
    key_minus_query = jnp.asarray(np.arange(SEL_KEYS)[:, None] - query, jnp.int32)
    n = batch * seq
    n_q = seq // Q_TILE
    gw = NSA_HPG * NSA_DK

    def per_group(arr):
        nd = arr.ndim - 2
        return pl.BlockSpec((1, 1) + arr.shape[2:], lambda b, g, i: (b, g) + (0,) * nd)

    return pl.pallas_call(
        _nsa_attn_kernel,
        grid=(batch, NSA_G, n_q),
        in_specs=[
            pl.BlockSpec(memory_space=pltpu.SMEM),
            pl.BlockSpec((Q_TILE, gw), lambda b, g, i: (b * n_q + i, g)),
            pl.BlockSpec((1, 1, glt.shape[2], Q_TILE), lambda b, g, i: (b, g, 0, i)),
            pl.BlockSpec((Q_TILE, gw), lambda b, g, i: (b * n_q + i, g)),
            per_group(ks_aug), per_group(vst), per_group(kw_pad), per_group(vwt),
            per_group(kc_pad), per_group(vct),
            _const_spec(overlap_t.shape), _const_spec(cmp_thr.shape), _const_spec(key_minus_query.shape),
        ],
        out_specs=pl.BlockSpec((Q_TILE, gw), lambda b, g, i: (b * n_q + i, g)),
        out_shape=jax.ShapeDtypeStruct((n, NSA_HEADS * NSA_DK), BF16),
        scratch_shapes=([pltpu.VMEM((V_ROWS, cols), F32)] * 2 + [pltpu.VMEM((SEL_BLOCK, Q_TILE), F32)]
                        + [pltpu.VMEM((SEL_KEYS, cols), BF16)] * 2
                        + [pltpu.VMEM((NSA_DK, cols), F32), pltpu.VMEM((SEL_BLOCK, Q_TILE), F32)]),
        compiler_params=pltpu.CompilerParams(
            dimension_semantics=("arbitrary", "arbitrary", "arbitrary"),
            vmem_limit_bytes=VMEM_LIMIT),
        name="nsa_attention",
    )(fast, q, glt, z, ks_aug, vst, kw_pad, vwt, kc_pad, vct, overlap_t, cmp_thr, key_minus_query)


def _ret_kernel(q_ref, k_ref, v_ref, z_ref, dec_ref, qd_ref, kd_ref, cd_ref, o_ref, st_ref):
    @pl.when(pl.program_id(1) == 0)
    def _():
        st_ref[...] = jnp.zeros_like(st_ref)

    for h in range(RET_HEADS):
        q = q_ref[:, h * RET_DK:(h + 1) * RET_DK]
        k = k_ref[:, h * RET_DK:(h + 1) * RET_DK]
        v = v_ref[:, h * RET_DV:(h + 1) * RET_DV]
        att = _dot_nt(q, k) * dec_ref[h]
        state = st_ref[h]
        o = _dot(att.astype(BF16), v) + _dot(q, state.astype(BF16)) * qd_ref[h]
        kd_t = (k.astype(F32) * kd_ref[h]).T.astype(BF16)
        st_ref[h] = state * cd_ref[h] + _dot(kd_t, v)
        mu = jnp.mean(o, axis=-1, keepdims=True)
        d = o - mu
        var = jnp.mean(d * d, axis=-1, keepdims=True)
        z = z_ref[:, h * RET_DV:(h + 1) * RET_DV].astype(F32)
        o_ref[:, h * RET_DV:(h + 1) * RET_DV] = (d * lax.rsqrt(var + GN_EPS) * _silu(z)).astype(o_ref.dtype)


def _retention(pr, tables, batch, seq):
    n = batch * seq
    n_c = seq // RET_CHUNK
    dec, qd, kd, cd = tables
    qk_w = RET_HEADS * RET_DK
    v_w = RET_HEADS * RET_DV
    row = lambda b, c: b * n_c + c
    return pl.pallas_call(
        _ret_kernel,
        grid=(batch, n_c),
        in_specs=[
            pl.BlockSpec((RET_CHUNK, qk_w), lambda b, c: (row(b, c), 0)),
            pl.BlockSpec((RET_CHUNK, qk_w), lambda b, c: (row(b, c), 1)),
            pl.BlockSpec((RET_CHUNK, v_w), lambda b, c: (row(b, c), 1)),
            pl.BlockSpec((RET_CHUNK, v_w), lambda b, c: (row(b, c), 2)),
            _const_spec(dec.shape), _const_spec(qd.shape), _const_spec(kd.shape), _const_spec(cd.shape),
        ],
        out_specs=pl.BlockSpec((RET_CHUNK, v_w), lambda b, c: (row(b, c), 0)),
        out_shape=jax.ShapeDtypeStruct((n, v_w), BF16),
        scratch_shapes=[pltpu.VMEM((RET_HEADS, RET_DK, RET_DV), F32)],
        compiler_params=pltpu.CompilerParams(
            dimension_semantics=("arbitrary", "arbitrary"), vmem_limit_bytes=VMEM_LIMIT),
        name="retention",
    )(pr, pr, pr, pr, dec, qd, kd, cd)


def _overlap_matrix_t(n_cmp_pad):
    i = np.arange(n_cmp_pad)[None, :]
    j = np.arange(SEL_BLOCK)[:, None]
    ov = (i * CMP_STRIDE < (j + 1) * SEL_BLOCK) & (i * CMP_STRIDE + CMP_BLOCK > j * SEL_BLOCK)
    return jnp.asarray(ov, BF16)


def _rotary_tables(seq):
    half = RET_DK // 2
    inv = ROPE_BASE ** (-jnp.linspace(0.0, 1.0, half, dtype=F32))
    ang = jnp.arange(seq, dtype=F32)[:, None] * inv[None, :]
    return jnp.cos(ang), jnp.sin(ang)


def _retention_tables():
    c = RET_CHUNK
    log_g = jnp.log(1.0 - 2.0 ** (-5.0 - jnp.arange(RET_HEADS, dtype=F32)))
    ix = jnp.arange(c, dtype=F32)
    diff = ix[:, None] - ix[None, :]
    dec = jnp.where(diff >= 0, jnp.exp(log_g[:, None, None] * jnp.maximum(diff, 0.0)), 0.0)
    qd = jnp.exp(log_g[:, None] * (ix + 1.0))[:, :, None]
    kd = jnp.exp(log_g[:, None] * (c - 1.0 - ix))[:, :, None]
    cd = jnp.broadcast_to(jnp.exp(log_g * c)[:, None, None], (RET_HEADS, 1, RET_DV))
    return dec, qd, kd, cd


def _nsa_layer(x2d, p2d, batch, seq, norm_g, w_in, q_g, kc_g, ks_g, kw_g, pos_k, pos_v,
               ck_w1, ck_w2, cv_w1, cv_w2, w_out, ple_w, ple_gate_w):
    qw = NSA_HEADS * NSA_DK
    kvw = NSA_G * NSA_DK
    n_gate = 3 * NSA_HPG
    sizes = [qw] + [kvw] * 6 + [3 * NSA_HEADS, qw]
    splits = np.concatenate([[0], np.cumsum(sizes)])
    col = lambda i: w_in[:, splits[i]:splits[i + 1]]
    wq, wkc, wvc, wks, wvs, wkw, wvw, wgl, wz = [col(i) for i in range(9)]
    wgl = jnp.pad(wgl.reshape(D_MODEL, NSA_G, n_gate), ((0, 0), (0, 0), (0, 16 - n_gate)))
    wgl = jnp.pad(wgl.reshape(D_MODEL, NSA_G * 16), ((0, 0), (0, LANES - NSA_G * 16)))
    names = ["q", "ks", "kw", "vs", "vw", "z", "kc", "vc", "gl"]
    parts = [wq, wks, wkw, wvs, wvw, wz, wkc, wvc, wgl]
    starts = np.concatenate([[0], np.cumsum([p.shape[1] for p in parts])])
    offs = {nm: int(starts[i]) for i, nm in enumerate(names)}
    w = jnp.concatenate(parts, axis=1).astype(BF16)
    head_gain = jnp.concatenate([
        jnp.tile(q_g, NSA_HEADS) * (NSA_DK ** -0.5 * LOG2E), jnp.tile(ks_g, NSA_G), jnp.tile(kw_g, NSA_G),
        jnp.zeros((w.shape[1] - qw - 2 * kvw,), F32)]).reshape(1, -1)

    q_norm = LOG2E * jnp.max(jnp.abs(q_g)) * 1.02
    bound_sel = (q_norm * NSA_DK ** 0.5 * jnp.max(jnp.abs(ks_g))).astype(BF16).astype(F32) * 1.01
    bound_win = (q_norm * NSA_DK ** 0.5 * jnp.max(jnp.abs(kw_g))).astype(BF16).astype(F32) * 1.01
    bound_cmp = (q_norm * NSA_DK ** 0.5 * jnp.max(jnp.abs(kc_g))).astype(BF16).astype(F32) * 1.01
    bound_max = jnp.maximum(jnp.maximum(bound_sel, bound_win), bound_cmp)
    fast = (bound_max <= MAX_SAFE_BOUND).astype(jnp.int32).reshape(1)

    blk_id = np.arange(seq)[:, None] // SEL_BLOCK
    upper = np.arange(LANES)[None, :] - NSA_DK
    onehot = jnp.asarray((blk_id == upper) & (upper > 0), F32)
    lane64 = jnp.asarray(upper == 0)
    aug_sel = jnp.where(lane64, -bound_sel, onehot).astype(BF16)
    aug_win = jnp.where(lane64, -bound_win, jnp.zeros((seq, LANES), F32)).astype(BF16)
    pad_rows = jnp.where(lane64, SEL_MASK_BIAS, jnp.zeros((WINDOW, LANES), F32)).astype(BF16)

    q, z, kc_sub, vc_sub, ks_aug, kw_pad, vst, vwt, glt = _nsa_proj(
        x2d, norm_g, w, head_gain, aug_sel, aug_win, pad_rows, offs, batch, seq)

    n_sub = seq // CMP_STRIDE
    half = CMP_STRIDE * NSA_DK
    w1cat = lambda w1: jnp.concatenate([w1[:half], w1[half:]], axis=1).astype(BF16)
    pos_flat = lambda pos: jnp.pad(pos.reshape(1, CMP_BLOCK * NSA_DK), ((0, 7), (0, 0))).astype(BF16)
    w2pad = lambda w2: jnp.pad(w2, ((0, 0), (0, LANES - NSA_DK))).astype(BF16)
    gk = jnp.pad(kc_g, (0, LANES - NSA_DK)).reshape(1, LANES)
    k_shift = jnp.where(lane64, -bound_cmp, 0.0)
    kc_pad, vct = _compress(kc_sub, vc_sub, w1cat(ck_w1), w1cat(cv_w1),
                            pos_flat(pos_k), pos_flat(pos_v), w2pad(ck_w2), w2pad(cv_w2), gk, k_shift)

    a = _nsa_attention(fast, q, z, glt, ks_aug, vst, kw_pad, vwt, kc_pad, vct,
                       _overlap_matrix_t(n_sub), batch, seq)
    return _out_proj(a, w_out, x2d, p2d, ple_gate_w, ple_w)


def _ret_layer(x2d, p2d, batch, seq, norm_g, w_in, w_out, ple_w, ple_gate_w):
    cos, sin = _rotary_tables(seq)
    pr = _ret_proj(x2d, norm_g, w_in.astype(BF16), cos, sin, seq)
    a = _retention(pr, _retention_tables(), batch, seq)
    return _out_proj(a, w_out, x2d, p2d, ple_gate_w, ple_w)


def kernel(x, p, norm_g, nsa_w_in, nsa_q_g, nsa_kc_g, nsa_ks_g, nsa_kw_g, nsa_cmp_pos_k, nsa_cmp_pos_v, nsa_cmp_k_w1, nsa_cmp_k_w2, nsa_cmp_v_w1, nsa_cmp_v_w2, nsa_w_out, ret_w_in, ret_w_out, ple_w, ple_gate_w):
    batch, seq, d_model = x.shape
    depth = p.shape[0]
    n = batch * seq
    x2d = x.reshape(n, d_model)
    for i in range(depth):
        p2d = p[i].reshape(n, PLE_DIM)
        j = i // 2
        if i % 2 == 0:
            x2d = _nsa_layer(x2d, p2d, batch, seq, norm_g[i], nsa_w_in[j], nsa_q_g[j], nsa_kc_g[j],
                             nsa_ks_g[j], nsa_kw_g[j], nsa_cmp_pos_k[j], nsa_cmp_pos_v[j],
                             nsa_cmp_k_w1[j], nsa_cmp_k_w2[j], nsa_cmp_v_w1[j], nsa_cmp_v_w2[j],
                             nsa_w_out[j], ple_w[i], ple_gate_w[i])
        else:
            x2d = _ret_layer(x2d, p2d, batch, seq, norm_g[i], ret_w_in[j], ret_w_out[j],
                             ple_w[i], ple_gate_w[i])
    return x2d.reshape(batch, seq, d_model)
```

```python
import functools

import numpy as np
import jax
import jax.numpy as jnp
from jax import lax
from jax.experimental import pallas as pl
from jax.experimental.pallas import tpu as pltpu

F32 = jnp.float32
BF16 = jnp.bfloat16

D_MODEL = 1024
PLE_DIM = 256
RMS_EPS = 1e-6
GN_EPS = 1e-5

NSA_HEADS = 16
NSA_DK = 64
NSA_G = 4
NSA_HPG = 4
CMP_BLOCK = 32
CMP_STRIDE = 16
CMP_HIDDEN = 256
SEL_BLOCK = 64
SEL_TOPN = 16
WINDOW = 512
FORCE_BONUS = 1e4
NEG = -1e30
SEL_MASK_BIAS = -30000.0

RET_HEADS = 4
RET_DK = 256
RET_DV = 512
RET_CHUNK = 256
ROPE_BASE = 10000.0

LANES = 128
VMEM_LIMIT = 56 * 1024 * 1024

ROW_TILE = 512
GROUP_W = NSA_HPG * NSA_DK
GATE_ROWS = 16
Q_TILE = 256
SEL_KEYS = 512
LOG2E = 1.4426950408889634
MAX_SAFE_BOUND = 50.0
V_ROWS = NSA_DK + 16


def _dot(a, b):
    return jnp.dot(a, b, preferred_element_type=F32)


def _dot_nt(a, b):
    return lax.dot_general(a, b, (((1,), (1,)), ((), ())), preferred_element_type=F32)


def _silu(x):
    return x * jax.nn.sigmoid(x)


def _const_spec(shape):
    nd = len(shape)
    return pl.BlockSpec(shape, lambda *_: (0,) * nd)


def _ret_proj_kernel(x_ref, g_ref, w_ref, cos_ref, sin_ref, o_ref):
    x = x_ref[...]
    h = (x * lax.rsqrt(jnp.mean(x * x, axis=-1, keepdims=True) + RMS_EPS) * g_ref[...]).astype(BF16)
    cos, sin = cos_ref[...], sin_ref[...]
    half = RET_DK // 2
    qk_w = RET_HEADS * RET_DK
    for t in range(2 * RET_HEADS):
        c0 = t * RET_DK
        acc = _dot(h, w_ref[:, c0:c0 + RET_DK])
        a1, a2 = acc[:, :half], acc[:, half:]
        rot = jnp.concatenate([a1 * cos - a2 * sin, a1 * sin + a2 * cos], axis=-1)
        if c0 >= qk_w:
            rot = rot * RET_DK ** -0.5
        o_ref[:, c0:c0 + RET_DK] = rot.astype(o_ref.dtype)
    for c0 in range(2 * qk_w, w_ref.shape[1], RET_DV):
        o_ref[:, c0:c0 + RET_DV] = _dot(h, w_ref[:, c0:c0 + RET_DV]).astype(o_ref.dtype)


def _ret_proj(x2d, norm_g, w, cos, sin, seq):
    n = x2d.shape[0]
    cols = w.shape[1]
    n_pos_blocks = seq // ROW_TILE
    return pl.pallas_call(
        _ret_proj_kernel,
        grid=(n // ROW_TILE,),
        in_specs=[
            pl.BlockSpec((ROW_TILE, D_MODEL), lambda i: (i, 0)),
            _const_spec((1, D_MODEL)),
            _const_spec((D_MODEL, cols)),
            pl.BlockSpec((ROW_TILE, LANES), lambda i: (i % n_pos_blocks, 0)),
            pl.BlockSpec((ROW_TILE, LANES), lambda i: (i % n_pos_blocks, 0)),
        ],
        out_specs=pl.BlockSpec((ROW_TILE, cols), lambda i: (i, 0)),
        out_shape=jax.ShapeDtypeStruct((n, cols), BF16),
        compiler_params=pltpu.CompilerParams(
            dimension_semantics=("arbitrary",), vmem_limit_bytes=VMEM_LIMIT),
        name="ret_norm_proj",
    )(x2d, norm_g.reshape(1, D_MODEL), w, cos, sin)


def _nsa_proj_kernel(x_ref, g_ref, w_ref, hg_ref, bd_ref, augs_ref, augw_ref, pad_ref,
                     q_ref, z_ref, kcs_ref, vcs_ref, ks_ref, kw_ref, vst_ref, vwt_ref, glt_ref, cmp_scr, *, offs):
    s = pl.program_id(1)
    tiles = ROW_TILE // Q_TILE

    @pl.when(s == 0)
    def _():
        for g in range(NSA_G):
            kw_ref[0, g] = pad_ref[...]
        vwt_ref[...] = jnp.zeros_like(vwt_ref)

    @pl.when(s > 0)
    def _():
        x = x_ref[...]
        h = (x * lax.rsqrt(jnp.mean(x * x, axis=-1, keepdims=True) + RMS_EPS) * g_ref[...]).astype(BF16)
        low = lax.broadcasted_iota(jnp.int32, (ROW_TILE, LANES), 1) < NSA_DK

        def proj(c0, width=GROUP_W):
            return _dot(h, w_ref[:, c0:c0 + width])

        def head_norm(acc, c0):
            ss = _dot((acc * acc).astype(BF16), bd_ref[...])
            return acc * lax.rsqrt(ss * (1.0 / NSA_DK) + RMS_EPS) * hg_ref[:, c0:c0 + GROUP_W]

        for t in range(NSA_G):
            lanes = slice(GROUP_W * t, GROUP_W * (t + 1))
            c0 = offs["q"] + GROUP_W * t
            q_ref[:, lanes] = head_norm(proj(c0), c0).astype(BF16)
            z_ref[:, lanes] = proj(offs["z"] + GROUP_W * t).astype(BF16)

        kvw = NSA_G * NSA_DK
        ones = jnp.ones((V_ROWS - NSA_DK, Q_TILE), BF16)
        for name, aug_ref, k_ref, v_ref in (("ks", augs_ref, ks_ref, vst_ref), ("kw", augw_ref, kw_ref, vwt_ref)):
            kv = proj(offs[name], 2 * kvw)
            k = head_norm(kv[:, :kvw], offs[name])
            aug = aug_ref[...].astype(F32)
            for pair in range(NSA_G // 2):
                two = k[:, LANES * pair:LANES * (pair + 1)]
                k_ref[0, 2 * pair] = jnp.where(low, two, aug).astype(BF16)
                k_ref[0, 2 * pair + 1] = jnp.where(low, pltpu.roll(two, NSA_DK, 1), aug).astype(BF16)
            vt = kv[:, kvw:].T
            for g in range(NSA_G):
                for j in range(tiles):
                    v_ref[0, g, j, 0:NSA_DK, :] = vt[g * NSA_DK:(g + 1) * NSA_DK,
                                                     j * Q_TILE:(j + 1) * Q_TILE].astype(BF16)
                    v_ref[0, g, j, NSA_DK:V_ROWS, :] = ones

        cmp_in = proj(offs["kc"], 2 * kvw)
        for t in range(2 * kvw // LANES):
            cmp_scr[t] = cmp_in[:, LANES * t:LANES * (t + 1)]
        n_sub = ROW_TILE // CMP_STRIDE
        low_sub = lax.broadcasted_iota(jnp.int32, (n_sub, LANES), 1) < NSA_DK
        for r in range(0, CMP_STRIDE, 2):
            for i, o_ref in enumerate((kcs_ref, vcs_ref)):
                for pair in range(NSA_G // 2):
                    t = i * (NSA_G // 2) + pair
                    e2 = cmp_scr[t, pl.ds(r, n_sub, stride=CMP_STRIDE), :]
                    o2 = cmp_scr[t, pl.ds(r + 1, n_sub, stride=CMP_STRIDE), :]
                    lanes = slice(LANES * (r // 2), LANES * (r // 2 + 1))
                    o_ref[0, 2 * pair, :, lanes] = jnp.where(low_sub, e2, pltpu.roll(o2, NSA_DK, 1)).astype(BF16)
                    o_ref[0, 2 * pair + 1, :, lanes] = jnp.where(low_sub, pltpu.roll(e2, NSA_DK, 1), o2).astype(BF16)

        glt = proj(offs["gl"], LANES).T
        for g in range(NSA_G):
            glt_ref[0, g] = glt[GATE_ROWS * g:GATE_ROWS * (g + 1)]


def _nsa_proj(x2d, norm_g, w, head_gain, aug_sel, aug_win, pad_rows, offs, batch, seq):
    n = batch * seq
    nsb = seq // ROW_TILE
    tiles = ROW_TILE // Q_TILE
    qw = NSA_HEADS * NSA_DK
    kvw = NSA_G * NSA_DK
    sub_w = CMP_STRIDE * NSA_DK
    lane_head = np.arange(GROUP_W) // NSA_DK
    bd = jnp.asarray(lane_head[:, None] == lane_head[None, :], BF16)
    prev = lambda s: jnp.maximum(s - 1, 0)
    rows = lambda b, s: (b * nsb + prev(s), 0)
    out_shape = [
        jax.ShapeDtypeStruct((n, qw), BF16), jax.ShapeDtypeStruct((n, qw), BF16),
        jax.ShapeDtypeStruct((batch, NSA_G, seq // CMP_STRIDE, sub_w), BF16),
        jax.ShapeDtypeStruct((batch, NSA_G, seq // CMP_STRIDE, sub_w), BF16),
        jax.ShapeDtypeStruct((batch, NSA_G, seq, LANES), BF16),
        jax.ShapeDtypeStruct((batch, NSA_G, seq + WINDOW, LANES), BF16),
        jax.ShapeDtypeStruct((batch, NSA_G, seq // Q_TILE, V_ROWS, Q_TILE), BF16),
        jax.ShapeDtypeStruct((batch, NSA_G, (seq + WINDOW) // Q_TILE, V_ROWS, Q_TILE), BF16),
        jax.ShapeDtypeStruct((batch, NSA_G, GATE_ROWS, seq), F32),
    ]
    out_specs = [
        pl.BlockSpec((ROW_TILE, qw), rows), pl.BlockSpec((ROW_TILE, qw), rows),
        pl.BlockSpec((1, NSA_G, ROW_TILE // CMP_STRIDE, sub_w), lambda b, s: (b, 0, prev(s), 0)),
        pl.BlockSpec((1, NSA_G, ROW_TILE // CMP_STRIDE, sub_w), lambda b, s: (b, 0, prev(s), 0)),
        pl.BlockSpec((1, NSA_G, ROW_TILE, LANES), lambda b, s: (b, 0, prev(s), 0)),
        pl.BlockSpec((1, NSA_G, ROW_TILE, LANES), lambda b, s: (b, 0, s, 0)),
        pl.BlockSpec((1, NSA_G, tiles, V_ROWS, Q_TILE), lambda b, s: (b, 0, prev(s), 0, 0)),
        pl.BlockSpec((1, NSA_G, tiles, V_ROWS, Q_TILE), lambda b, s: (b, 0, s, 0, 0)),
        pl.BlockSpec((1, NSA_G, GATE_ROWS, ROW_TILE), lambda b, s: (b, 0, 0, prev(s))),
    ]
    assert WINDOW == ROW_TILE
    return pl.pallas_call(
        functools.partial(_nsa_proj_kernel, offs=offs),
        grid=(batch, nsb + 1),
        in_specs=[
            pl.BlockSpec((ROW_TILE, D_MODEL), rows),
            _const_spec((1, D_MODEL)),
            _const_spec(w.shape),
            _const_spec(head_gain.shape),
            _const_spec(bd.shape),
            pl.BlockSpec((ROW_TILE, LANES), lambda b, s: (prev(s), 0)),
            pl.BlockSpec((ROW_TILE, LANES), lambda b, s: (prev(s), 0)),
            _const_spec(pad_rows.shape),
        ],
        out_specs=out_specs,
        out_shape=out_shape,
        scratch_shapes=[pltpu.VMEM((2 * kvw // LANES, ROW_TILE, LANES), F32)],
        compiler_params=pltpu.CompilerParams(
            dimension_semantics=("arbitrary", "arbitrary"), vmem_limit_bytes=VMEM_LIMIT),
        name="nsa_norm_proj",
    )(x2d, norm_g.reshape(1, D_MODEL), w, head_gain, bd, aug_sel, aug_win, pad_rows)


def _out_kernel(a_ref, wo_ref, x_ref, p_ref, wg_ref, wp_ref, o_ref):
    x1 = x_ref[...] + _dot(a_ref[...], wo_ref[...])
    gate = jax.nn.sigmoid(_dot(x1.astype(BF16), wg_ref[...]))
    emb = _dot(p_ref[...].astype(BF16), wp_ref[...])
    o_ref[...] = x1 + gate * emb


def _out_proj(a, w_out, x2d, p2d, w_gate, w_ple):
    n, k = a.shape
    return pl.pallas_call(
        _out_kernel,
        grid=(n // ROW_TILE,),
        in_specs=[
            pl.BlockSpec((ROW_TILE, k), lambda i: (i, 0)),
            _const_spec((k, D_MODEL)),
            pl.BlockSpec((ROW_TILE, D_MODEL), lambda i: (i, 0)),
            pl.BlockSpec((ROW_TILE, PLE_DIM), lambda i: (i, 0)),
            _const_spec((D_MODEL, D_MODEL)),
            _const_spec((PLE_DIM, D_MODEL)),
        ],
        out_specs=pl.BlockSpec((ROW_TILE, D_MODEL), lambda i: (i, 0)),
        out_shape=jax.ShapeDtypeStruct((n, D_MODEL), F32),
        compiler_params=pltpu.CompilerParams(
            dimension_semantics=("arbitrary",), vmem_limit_bytes=VMEM_LIMIT),
        name="out_proj_ple",
    )(a, w_out.astype(BF16), x2d, p2d, w_gate.astype(BF16), w_ple.astype(BF16))


def _cmp_kernel(xk_ref, xv_ref, w1k_ref, w1v_ref, posk_ref, posv_ref, w2k_ref, w2v_ref,
                gk_ref, shift_ref, kc_ref, vc_ref):
    n_sub = xk_ref.shape[2]
    half = xk_ref.shape[3]
    for is_k, x_ref, w1_ref, pos_ref, w2_ref, o_ref in (
            (True, xk_ref, w1k_ref, posk_ref, w2k_ref, kc_ref),
            (False, xv_ref, w1v_ref, posv_ref, w2v_ref, vc_ref)):
        w1 = w1_ref[...]
        x = x_ref[0].reshape(NSA_G * n_sub, half)
        ab = _dot(x, w1)
        pos = pos_ref[...]
        pos_term = (_dot(pos[:, :half], w1[:, :CMP_HIDDEN])
                    + _dot(pos[:, half:], w1[:, CMP_HIDDEN:]))[0:1]
        for g in range(NSA_G):
            first = ab[g * n_sub:(g + 1) * n_sub, :CMP_HIDDEN]
            second = ab[g * n_sub:(g + 1) * n_sub, CMP_HIDDEN:]
            hid = _silu(first + pltpu.roll(second, n_sub - 1, 0) + pos_term)
            c = _dot(hid.astype(BF16), w2_ref[...])
            if is_k:
                ss = jnp.sum(c * c, axis=-1, keepdims=True) * (1.0 / NSA_DK)
                c = c * lax.rsqrt(ss + RMS_EPS) * gk_ref[...] + shift_ref[...]
                o_ref[0, g] = c.astype(BF16)
            else:
                ct = jnp.concatenate([c.T[:NSA_DK], jnp.ones((V_ROWS - NSA_DK, n_sub), F32)], axis=0)
                o_ref[0, g] = ct.astype(BF16)


def _compress(xk, xv, w1k, w1v, posk, posv, w2k, w2v, gk, k_shift):
    b, g, n_sub, half = xk.shape
    blk = pl.BlockSpec((1, g, n_sub, half), lambda i: (i, 0, 0, 0))
    k_blk = pl.BlockSpec((1, g, n_sub, LANES), lambda i: (i, 0, 0, 0))
    k_sds = jax.ShapeDtypeStruct((b, g, n_sub, LANES), BF16)
    v_blk = pl.BlockSpec((1, g, V_ROWS, n_sub), lambda i: (i, 0, 0, 0))
    v_sds = jax.ShapeDtypeStruct((b, g, V_ROWS, n_sub), BF16)
    return pl.pallas_call(
        _cmp_kernel,
        grid=(b,),
        in_specs=[blk, blk,
                  _const_spec(w1k.shape), _const_spec(w1v.shape),
                  _const_spec(posk.shape), _const_spec(posv.shape),
                  _const_spec(w2k.shape), _const_spec(w2v.shape),
                  _const_spec(gk.shape), _const_spec(k_shift.shape)],
        out_specs=[k_blk, v_blk],
        out_shape=[k_sds, v_sds],
        compiler_params=pltpu.CompilerParams(
            dimension_semantics=("arbitrary",), vmem_limit_bytes=VMEM_LIMIT),
        name="nsa_compress",
    )(xk, xv, w1k, w1v, posk, posv, w2k, w2v, gk, k_shift)


def _nsa_attn_kernel(fast_ref, q_ref, glt_ref, z_ref, ks_ref, vst_ref, kw_ref, vwt_ref, kc_ref, vct_ref,
                     ovt_ref, cthr_ref, kmq_ref, o_ref, acc_ref, accw_ref, rank_ref, p0_ref, p1_ref, ac_ref, impt_ref):
    tq = Q_TILE
    cols = NSA_HPG * tq
    qi = pl.program_id(2)
    t0 = qi * tq
    n_sel = SEL_BLOCK
    dk = NSA_DK

    qt = q_ref[...].astype(F32).T
    q_heads = [qt[h * dk:(h + 1) * dk] for h in range(NSA_HPG)]

    def stack_heads(extra_rows):
        return jnp.concatenate(
            [jnp.concatenate([qh, extra_rows], axis=0) for qh in q_heads], axis=1).astype(BF16)

    blk = lax.broadcasted_iota(jnp.int32, (n_sel, tq), 0)
    ones_row = jnp.where(blk == 0, 1.0, 0.0)
    fast = fast_ref[0] > 0

    def compressed(subtract_max):
        sc = _dot(kc_ref[0, 0], stack_heads(ones_row))
        sc = jnp.where(cthr_ref[...] <= t0, sc, NEG)
        if subtract_max:
            sc = sc - jnp.maximum(jnp.max(sc, axis=0, keepdims=True), 0.1 * NEG)
        ec = jnp.exp2(sc).astype(BF16)
        r = _dot(jnp.concatenate([vct_ref[0, 0], ovt_ref[...]], axis=0), ec)
        lc = r[dk:dk + 1]
        inv_lc = 1.0 / jnp.where(lc > 0.0, lc, 1.0)
        ac_ref[...] = r[:dk] * inv_lc
        imp = r[V_ROWS:] * inv_lc
        impt_ref[...] = imp[:, 0:tq] + imp[:, tq:2 * tq] + imp[:, 2 * tq:3 * tq] + imp[:, 3 * tq:4 * tq]

    pl.when(fast)(lambda: compressed(False))
    pl.when(jnp.logical_not(fast))(lambda: compressed(True))
    imp_t = impt_ref[...]

    cur = (t0 + lax.broadcasted_iota(jnp.int32, (n_sel, tq), 1)) // SEL_BLOCK
    forced = (blk == 0) | (blk == cur) | (blk == cur - 1)
    score = jnp.where(blk <= cur, imp_t + jnp.where(forced, FORCE_BONUS, 0.0), NEG)
    sub = 8
    chunks = [score[c * sub:(c + 1) * sub] for c in range(n_sel // sub)]
    blk_sub = lax.broadcasted_iota(jnp.int32, (sub, tq), 0)
    last_blk = (t0 + tq - 1) // SEL_BLOCK
    rank_ref[...] = jnp.zeros_like(rank_ref)
    for jb in range(n_sel // sub):
        @pl.when(jb * sub <= last_blk)
        def _():
            ranks = [jnp.zeros((sub, tq), F32) for _ in chunks]
            for j in range(jb * sub, (jb + 1) * sub):
                row = jnp.broadcast_to(chunks[jb][j - jb * sub:j - jb * sub + 1, :], (sub, tq))
                for c, chunk in enumerate(chunks):
                    if c > jb:
                        one = jnp.where(row >= chunk, 1.0, 0.0)
                    elif c < jb:
                        one = jnp.where(row > chunk, 1.0, 0.0)
                    else:
                        one = jnp.where(blk_sub > j - c * sub,
                                        jnp.where(row >= chunk, 1.0, 0.0), jnp.where(row > chunk, 1.0, 0.0))
                    ranks[c] = ranks[c] + one
            rank_ref[...] += jnp.concatenate(ranks, axis=0)
    bias_t = jnp.where(rank_ref[...] < float(SEL_TOPN), 0.0, SEL_MASK_BIAS)
    q_aug = stack_heads(jnp.where(blk == 0, 1.0, bias_t))

    tiles_per_step = SEL_KEYS // tq
    n_full = t0 // SEL_KEYS
    n_wt = WINDOW // tq + 1
    diag_keep = kmq_ref[0:tq, :] <= 0

    def sel_scores(step, tail):
        base = pl.multiple_of(step * SEL_KEYS, SEL_KEYS)
        s = _dot(ks_ref[0, 0, pl.ds(base, SEL_KEYS), :], q_aug)
        if tail:
            s = jnp.where(kmq_ref[...] <= t0 - base, s, NEG)
        vt = vst_ref[0, 0, pl.ds(step * tiles_per_step, tiles_per_step)]
        return s, jnp.concatenate([vt[i] for i in range(tiles_per_step)], axis=1)

    def win_scores():
        sw = _dot(kw_ref[0, 0, pl.ds(pl.multiple_of(t0, tq), n_wt * tq), :], q_aug)
        parts = [jnp.where(diag_keep, NEG, sw[:tq])]
        if WINDOW > tq:
            parts.append(sw[tq:WINDOW])
        parts.append(jnp.where(diag_keep, sw[WINDOW:], NEG))
        sw = jnp.concatenate(parts, axis=0)
        vwt = vwt_ref[0, 0, pl.ds(qi, n_wt)]
        return sw, jnp.concatenate([vwt[i] for i in range(n_wt)], axis=1)

    @pl.when(fast)
    def _():
        acc_ref[...] = jnp.zeros_like(acc_ref)

        def values(step):
            vt = vst_ref[0, 0, pl.ds(step * tiles_per_step, tiles_per_step)]
            return jnp.concatenate([vt[i] for i in range(tiles_per_step)], axis=1)

        def probs(step, tail):
            return jnp.exp2(sel_scores(step, tail)[0]).astype(BF16)

        odd = n_full % 2
        half = SEL_KEYS // 2
        assert half == tq

        @pl.when(t0 % SEL_KEYS != 0)
        def _():
            p1_ref[...] = probs(n_full, True)

        @pl.when(t0 % SEL_KEYS == 0)
        def _():
            s = _dot(ks_ref[0, 0, pl.ds(pl.multiple_of(t0, SEL_KEYS), half), :], q_aug)
            p1_ref[0:half, :] = jnp.exp2(jnp.where(diag_keep, s, NEG)).astype(BF16)
            p1_ref[half:, :] = jnp.zeros((half, cols), BF16)

        @pl.when(odd == 1)
        def _():
            p = probs(n_full - 1, False)
            acc_ref[...] += _dot(values(n_full), p1_ref[...])
            p1_ref[...] = p

        first_pending = n_full - odd

        def pair(j, carry):
            pending = jnp.where(j == 0, first_pending, 2 * j - 1)
            pa = probs(2 * j, False)
            acc_ref[...] += _dot(values(pending), p1_ref[...])
            p0_ref[...] = pa
            pb = probs(2 * j + 1, False)
            acc_ref[...] += _dot(values(2 * j), p0_ref[...])
            p1_ref[...] = pb
            return carry

        n_pairs = n_full // 2
        lax.fori_loop(0, n_pairs, pair, 0)
        pending = jnp.where(n_pairs == 0, first_pending, 2 * n_pairs - 1)
        sw, vwt = win_scores()
        pw = jnp.exp2(sw).astype(BF16)
        acc_ref[...] += _dot(values(pending), p1_ref[...])
        accw_ref[...] = _dot(vwt, pw)

    @pl.when(jnp.logical_not(fast))
    def _():
        acc_ref[...] = jnp.zeros_like(acc_ref)

        def online(s, vt, m):
            m_new = jnp.maximum(m, jnp.max(s, axis=0, keepdims=True))
            acc_ref[...] = jnp.exp2(m - m_new) * acc_ref[...] + _dot(vt, jnp.exp2(s - m_new).astype(BF16))
            return m_new

        m = jnp.full((1, cols), NEG, F32)
        m = lax.fori_loop(0, n_full, lambda i, m: online(*sel_scores(i, False), m), m)
        online(*sel_scores(n_full, True), m)
        sw, vwt = win_scores()
        accw_ref[...] = _dot(vwt, jnp.exp2(sw - jnp.max(sw, axis=0, keepdims=True)).astype(BF16))

    a_s = acc_ref[...]
    a_w = accw_ref[...]

    a_s = a_s[:dk] * (1.0 / a_s[dk:dk + 1])
    a_w = a_w[:dk] * (1.0 / a_w[dk:dk + 1])
    gates = jax.nn.sigmoid(glt_ref[0, 0])
    outs = []
    for h in range(NSA_HPG):
        sl = slice(h * tq, (h + 1) * tq)
        outs.append(gates[3 * h:3 * h + 1] * ac_ref[:, sl] + gates[3 * h + 1:3 * h + 2] * a_s[:, sl]
                    + gates[3 * h + 2:3 * h + 3] * a_w[:, sl])
    out = jnp.concatenate(outs, axis=0).T
    o_ref[...] = (out * _silu(z_ref[...].astype(F32))).astype(o_ref.dtype)


def _nsa_attention(fast, q, z, glt, ks_aug, vst, kw_pad, vwt, kc_pad, vct, overlap_t, batch, seq):
    cols = NSA_HPG * Q_TILE
    query = np.arange(cols)[None, :] % Q_TILE
    cmp_thr = jnp.asarray(np.arange(kc_pad.shape[2])[:, None] * CMP_STRIDE + (CMP_BLOCK - 1) - query, jnp.int32)
    key_minus_query = jnp.asarray(np.arange(SEL_KEYS)[:, None] - query, jnp.int32)
    n = batch * seq
    n_q = seq // Q_TILE
    gw = NSA_HPG * NSA_DK

    def per_group(arr):
        nd = arr.ndim - 2
        return pl.BlockSpec((1, 1) + arr.shape[2:], lambda b, g, i: (b, g) + (0,) * nd)

    return pl.pallas_call(
        _nsa_attn_kernel,
        grid=(batch, NSA_G, n_q),
        in_specs=[
            pl.BlockSpec(memory_space=pltpu.SMEM),
            pl.BlockSpec((Q_TILE, gw), lambda b, g, i: (b * n_q + i, g)),
            pl.BlockSpec((1, 1, glt.shape[2], Q_TILE), lambda b, g, i: (b, g, 0, i)),
            pl.BlockSpec((Q_TILE, gw), lambda b, g, i: (b * n_q + i, g)),
            per_group(ks_aug), per_group(vst), per_group(kw_pad), per_group(vwt),
            per_group(kc_pad), per_group(vct),
            _const_spec(overlap_t.shape), _const_spec(cmp_thr.shape), _const_spec(key_minus_query.shape),
        ],
        out_specs=pl.BlockSpec((Q_TILE, gw), lambda b, g, i: (b * n_q + i, g)),
        out_shape=jax.ShapeDtypeStruct((n, NSA_HEADS * NSA_DK), BF16),
        scratch_shapes=([pltpu.VMEM((V_ROWS, cols), F32)] * 2 + [pltpu.VMEM((SEL_BLOCK, Q_TILE), F32)]
                        + [pltpu.VMEM((SEL_KEYS, cols), BF16)] * 2
                        + [pltpu.VMEM((NSA_DK, cols), F32), pltpu.VMEM((SEL_BLOCK, Q_TILE), F32)]),
        compiler_params=pltpu.CompilerParams(
            dimension_semantics=("arbitrary", "arbitrary", "arbitrary"),
            vmem_limit_bytes=VMEM_LIMIT),
        name="nsa_attention",
    )(fast, q, glt, z, ks_aug, vst, kw_pad, vwt, kc_pad, vct, overlap_t, cmp_thr, key_minus_query)


def _ret_kernel(q_ref, k_ref, v_ref, z_ref, dec_ref, qd_ref, kd_ref, cd_ref, o_ref, st_ref):
    @pl.when(pl.program_id(1) == 0)
    def _():
        st_ref[...] = jnp.zeros_like(st_ref)

    for h in range(RET_HEADS):
        q = q_ref[:, h * RET_DK:(h + 1) * RET_DK]
        k = k_ref[:, h * RET_DK:(h + 1) * RET_DK]
        v = v_ref[:, h * RET_DV:(h + 1) * RET_DV]
        att = _dot_nt(q, k) * dec_ref[h]
        state = st_ref[h]
        o = _dot(att.astype(BF16), v) + _dot(q, state.astype(BF16)) * qd_ref[h]
        kd_t = (k.astype(F32) * kd_ref[h]).T.astype(BF16)
        st_ref[h] = state * cd_ref[h] + _dot(kd_t, v)
        mu = jnp.mean(o, axis=-1, keepdims=True)
        d = o - mu
        var = jnp.mean(d * d, axis=-1, keepdims=True)
        z = z_ref[:, h * RET_DV:(h + 1) * RET_DV].astype(F32)
        o_ref[:, h * RET_DV:(h + 1) * RET_DV] = (d * lax.rsqrt(var + GN_EPS) * _silu(z)).astype(o_ref.dtype)


def _retention(pr, tables, batch, seq):
    n = batch * seq
    n_c = seq // RET_CHUNK
    dec, qd, kd, cd = tables
    qk_w = RET_HEADS * RET_DK
    v_w = RET_HEADS * RET_DV
    row = lambda b, c: b * n_c + c
    return pl.pallas_call(
        _ret_kernel,
        grid=(batch, n_c),
        in_specs=[
            pl.BlockSpec((RET_CHUNK, qk_w), lambda b, c: (row(b, c), 0)),
            pl.BlockSpec((RET_CHUNK, qk_w), lambda b, c: (row(b, c), 1)),
            pl.BlockSpec((RET_CHUNK, v_w), lambda b, c: (row(b, c), 1)),
            pl.BlockSpec((RET_CHUNK, v_w), lambda b, c: (row(b, c), 2)),
            _const_spec(dec.shape), _const_spec(qd.shape), _const_spec(kd.shape), _const_spec(cd.shape),
        ],
        out_specs=pl.BlockSpec((RET_CHUNK, v_w), lambda b, c: (row(b, c), 0)),
        out_shape=jax.ShapeDtypeStruct((n, v_w), BF16),
        scratch_shapes=[pltpu.VMEM((RET_HEADS, RET_DK, RET_DV), F32)],
        compiler_params=pltpu.CompilerParams(
            dimension_semantics=("arbitrary", "arbitrary"), vmem_limit_bytes=VMEM_LIMIT),
        name="retention",
    )(pr, pr, pr, pr, dec, qd, kd, cd)


def _overlap_matrix_t(n_cmp_pad):
    i = np.arange(n_cmp_pad)[None, :]
    j = np.arange(SEL_BLOCK)[:, None]
    ov = (i * CMP_STRIDE < (j + 1) * SEL_BLOCK) & (i * CMP_STRIDE + CMP_BLOCK > j * SEL_BLOCK)
    return jnp.asarray(ov, BF16)


def _rotary_tables(seq):
    half = RET_DK // 2
    inv = ROPE_BASE ** (-jnp.linspace(0.0, 1.0, half, dtype=F32))
    ang = jnp.arange(seq, dtype=F32)[:, None] * inv[None, :]
    return jnp.cos(ang), jnp.sin(ang)


def _retention_tables():
    c = RET_CHUNK
    log_g = jnp.log(1.0 - 2.0 ** (-5.0 - jnp.arange(RET_HEADS, dtype=F32)))
    ix = jnp.arange(c, dtype=F32)
    diff = ix[:, None] - ix[None, :]
    dec = jnp.where(diff >= 0, jnp.exp(log_g[:, None, None] * jnp.maximum(diff, 0.0)), 0.0)
    qd = jnp.exp(log_g[:, None] * (ix + 1.0))[:, :, None]
    kd = jnp.exp(log_g[:, None] * (c - 1.0 - ix))[:, :, None]
    cd = jnp.broadcast_to(jnp.exp(log_g * c)[:, None, None], (RET_HEADS, 1, RET_DV))
    return dec, qd, kd, cd


def _nsa_layer(x2d, p2d, batch, seq, norm_g, w_in, q_g, kc_g, ks_g, kw_g, pos_k, pos_v,
               ck_w1, ck_w2, cv_w1, cv_w2, w_out, ple_w, ple_gate_w):
    qw = NSA_HEADS * NSA_DK
    kvw = NSA_G * NSA_DK
    n_gate = 3 * NSA_HPG
    sizes = [qw] + [kvw] * 6 + [3 * NSA_HEADS, qw]
    splits = np.concatenate([[0], np.cumsum(sizes)])
    names = ["q", "kc", "vc", "ks", "vs", "kw", "vw"]
    offs = {nm: int(splits[i]) for i, nm in enumerate(names)}
    front = int(splits[7])
    offs["z"], offs["gl"] = front, front + qw
    wgl = w_in[:, splits[7]:splits[8]].reshape(D_MODEL, NSA_G, n_gate)
    wgl = jnp.pad(wgl, ((0, 0), (0, 0), (0, GATE_ROWS - n_gate))).reshape(D_MODEL, NSA_G * GATE_ROWS)
    wgl = jnp.pad(wgl, ((0, 0), (0, LANES - NSA_G * GATE_ROWS)))
    w = jnp.concatenate([w_in[:, :front], w_in[:, splits[8]:], wgl], axis=1).astype(BF16)
    zeros = lambda width: jnp.zeros((width,), F32)
    head_gain = jnp.concatenate([
        jnp.tile(q_g, NSA_HEADS) * (NSA_DK ** -0.5 * LOG2E), zeros(2 * kvw),
        jnp.tile(ks_g, NSA_G), zeros(kvw), jnp.tile(kw_g, NSA_G), zeros(w.shape[1] - offs["vw"])]).reshape(1, -1)

    q_norm = LOG2E * jnp.max(jnp.abs(q_g)) * 1.02
    bound_sel = (q_norm * NSA_DK ** 0.5 * jnp.max(jnp.abs(ks_g))).astype(BF16).astype(F32) * 1.01
    bound_win = (q_norm * NSA_DK ** 0.5 * jnp.max(jnp.abs(kw_g))).astype(BF16).astype(F32) * 1.01
    bound_cmp = (q_norm * NSA_DK ** 0.5 * jnp.max(jnp.abs(kc_g))).astype(BF16).astype(F32) * 1.01
    bound_max = jnp.maximum(jnp.maximum(bound_sel, bound_win), bound_cmp)
    fast = (bound_max <= MAX_SAFE_BOUND).astype(jnp.int32).reshape(1)

    blk_id = np.arange(seq)[:, None] // SEL_BLOCK
    upper = np.arange(LANES)[None, :] - NSA_DK
    onehot = jnp.asarray((blk_id == upper) & (upper > 0), F32)
    lane64 = jnp.asarray(upper == 0)
    aug_sel = jnp.where(lane64, -bound_sel, onehot).astype(BF16)
    aug_win = jnp.where(lane64, -bound_win, jnp.zeros((seq, LANES), F32)).astype(BF16)
    pad_rows = jnp.where(lane64, SEL_MASK_BIAS, jnp.zeros((WINDOW, LANES), F32)).astype(BF16)

    q, z, kc_sub, vc_sub, ks_aug, kw_pad, vst, vwt, glt = _nsa_proj(
        x2d, norm_g, w, head_gain, aug_sel, aug_win, pad_rows, offs, batch, seq)

    n_sub = seq // CMP_STRIDE
    half = CMP_STRIDE * NSA_DK
    w1cat = lambda w1: jnp.concatenate([w1[:half], w1[half:]], axis=1).astype(BF16)
    pos_flat = lambda pos: jnp.pad(pos.reshape(1, CMP_BLOCK * NSA_DK), ((0, 7), (0, 0))).astype(BF16)
    w2pad = lambda w2: jnp.pad(w2, ((0, 0), (0, LANES - NSA_DK))).astype(BF16)
    gk = jnp.pad(kc_g, (0, LANES - NSA_DK)).reshape(1, LANES)
    k_shift = jnp.where(lane64, -bound_cmp, 0.0)
    kc_pad, vct = _compress(kc_sub, vc_sub, w1cat(ck_w1), w1cat(cv_w1),
                            pos_flat(pos_k), pos_flat(pos_v), w2pad(ck_w2), w2pad(cv_w2), gk, k_shift)

    a = _nsa_attention(fast, q, z, glt, ks_aug, vst, kw_pad, vwt, kc_pad, vct,
                       _overlap_matrix_t(n_sub), batch, seq)
    return _out_proj(a, w_out, x2d, p2d, ple_gate_w, ple_w)


def _ret_layer(x2d, p2d, batch, seq, norm_g, w_in, w_out, ple_w, ple_gate_w):
    cos, sin = _rotary_tables(seq)
    pr = _ret_proj(x2d, norm_g, w_in.astype(BF16), cos, sin, seq)
    a = _retention(pr, _retention_tables(), batch, seq)
    return _out_proj(a, w_out, x2d, p2d, ple_gate_w, ple_w)


def kernel(x, p, norm_g, nsa_w_in, nsa_q_g, nsa_kc_g, nsa_ks_g, nsa_kw_g, nsa_cmp_pos_k, nsa_cmp_pos_v, nsa_cmp_k_w1, nsa_cmp_k_w2, nsa_cmp_v_w1, nsa_cmp_v_w2, nsa_w_out, ret_w_in, ret_w_out, ple_w, ple_gate_w):
    batch, seq, d_model = x.shape
    depth = p.shape[0]
    n = batch * seq
    x2d = x.reshape(n, d_model)
    for i in range(depth):
        p2d = p[i].reshape(n, PLE_DIM)
        j = i // 2
        if i % 2 == 0:
            x2d = _nsa_layer(x2d, p2d, batch, seq, norm_g[i], nsa_w_in[j], nsa_q_g[j], nsa_kc_g[j],
                             nsa_ks_g[j], nsa_kw_g[j], nsa_cmp_pos_k[j], nsa_cmp_pos_v[j],
                             nsa_cmp_k_w1[j], nsa_cmp_k_w2[j], nsa_cmp_v_w1[j], nsa_cmp_v_w2[j],
                             nsa_w_out[j], ple_w[i], ple_gate_w[i])
        else:
            x2d = _ret_layer(x2d, p2d, batch, seq, norm_g[i], ret_w_in[j], ret_w_out[j],
                             ple_w[i], ple_gate_w[i])
    return x2d.reshape(batch, seq, d_model)
```

```python
import functools

import numpy as np
import jax
import jax.numpy as jnp
from jax import lax
from jax.experimental import pallas as pl
from jax.experimental.pallas import tpu as pltpu

F32 = jnp.float32
BF16 = jnp.bfloat16

D_MODEL = 1024
PLE_DIM = 256
RMS_EPS = 1e-6
GN_EPS = 1e-5

NSA_HEADS = 16
NSA_DK = 64
NSA_G = 4
NSA_HPG = 4
CMP_BLOCK = 32
CMP_STRIDE = 16
CMP_HIDDEN = 256
SEL_BLOCK = 64
SEL_TOPN = 16
WINDOW = 512
FORCE_BONUS = 1e4
NEG = -1e30
SEL_MASK_BIAS = -30000.0

RET_HEADS = 4
RET_DK = 256
RET_DV = 512
RET_CHUNK = 256
ROPE_BASE = 10000.0

LANES = 128
VMEM_LIMIT = 56 * 1024 * 1024

ROW_TILE = 512
GROUP_W = NSA_HPG * NSA_DK
GATE_ROWS = 16
Q_TILE = 256
Q_TILES_PER_STEP = 2
SEL_KEYS = 512
LOG2E = 1.4426950408889634
MAX_SAFE_BOUND = 50.0
V_ROWS = NSA_DK + 16


def _dot(a, b):
    return jnp.dot(a, b, preferred_element_type=F32)


def _dot_nt(a, b):
    return lax.dot_general(a, b, (((1,), (1,)), ((), ())), preferred_element_type=F32)


def _silu(x):
    return x * jax.nn.sigmoid(x)


def _const_spec(shape):
    nd = len(shape)
    return pl.BlockSpec(shape, lambda *_: (0,) * nd)


def _ret_proj_kernel(x_ref, g_ref, w_ref, cos_ref, sin_ref, o_ref):
    x = x_ref[...]
    h = (x * lax.rsqrt(jnp.mean(x * x, axis=-1, keepdims=True) + RMS_EPS) * g_ref[...]).astype(BF16)
    cos, sin = cos_ref[...], sin_ref[...]
    half = RET_DK // 2
    qk_w = RET_HEADS * RET_DK
    for t in range(2 * RET_HEADS):
        c0 = t * RET_DK
        acc = _dot(h, w_ref[:, c0:c0 + RET_DK])
        a1, a2 = acc[:, :half], acc[:, half:]
        rot = jnp.concatenate([a1 * cos - a2 * sin, a1 * sin + a2 * cos], axis=-1)
        if c0 >= qk_w:
            rot = rot * RET_DK ** -0.5
        o_ref[:, c0:c0 + RET_DK] = rot.astype(o_ref.dtype)
    for c0 in range(2 * qk_w, w_ref.shape[1], RET_DV):
        o_ref[:, c0:c0 + RET_DV] = _dot(h, w_ref[:, c0:c0 + RET_DV]).astype(o_ref.dtype)


def _ret_proj(x2d, norm_g, w, cos, sin, seq):
    n = x2d.shape[0]
    cols = w.shape[1]
    n_pos_blocks = seq // ROW_TILE
    return pl.pallas_call(
        _ret_proj_kernel,
        grid=(n // ROW_TILE,),
        in_specs=[
            pl.BlockSpec((ROW_TILE, D_MODEL), lambda i: (i, 0)),
            _const_spec((1, D_MODEL)),
            _const_spec((D_MODEL, cols)),
            pl.BlockSpec((ROW_TILE, LANES), lambda i: (i % n_pos_blocks, 0)),
            pl.BlockSpec((ROW_TILE, LANES), lambda i: (i % n_pos_blocks, 0)),
        ],
        out_specs=pl.BlockSpec((ROW_TILE, cols), lambda i: (i, 0)),
        out_shape=jax.ShapeDtypeStruct((n, cols), BF16),
        compiler_params=pltpu.CompilerParams(
            dimension_semantics=("arbitrary",), vmem_limit_bytes=VMEM_LIMIT),
        name="ret_norm_proj",
    )(x2d, norm_g.reshape(1, D_MODEL), w, cos, sin)


def _nsa_proj_kernel(x_ref, g_ref, w_ref, hg_ref, bd_ref, augs_ref, augw_ref, pad_ref,
                     q_ref, z_ref, kcs_ref, vcs_ref, ks_ref, kw_ref, vst_ref, vwt_ref, glt_ref, cmp_scr, *, offs):
    s = pl.program_id(1)
    tiles = ROW_TILE // Q_TILE

    @pl.when(s == 0)
    def _():
        for g in range(NSA_G):
            kw_ref[0, g] = pad_ref[...]
        vwt_ref[...] = jnp.zeros_like(vwt_ref)

    @pl.when(s > 0)
    def _():
        x = x_ref[...]
        h = (x * lax.rsqrt(jnp.mean(x * x, axis=-1, keepdims=True) + RMS_EPS) * g_ref[...]).astype(BF16)
        low = lax.broadcasted_iota(jnp.int32, (ROW_TILE, LANES), 1) < NSA_DK

        def proj(c0, width=GROUP_W):
            return _dot(h, w_ref[:, c0:c0 + width])

        def head_norm(acc, c0):
            ss = _dot((acc * acc).astype(BF16), bd_ref[...])
            return acc * lax.rsqrt(ss * (1.0 / NSA_DK) + RMS_EPS) * hg_ref[:, c0:c0 + GROUP_W]

        for t in range(NSA_G):
            lanes = slice(GROUP_W * t, GROUP_W * (t + 1))
            c0 = offs["q"] + GROUP_W * t
            q_ref[:, lanes] = head_norm(proj(c0), c0).astype(BF16)
            z_ref[:, lanes] = proj(offs["z"] + GROUP_W * t).astype(BF16)

        kvw = NSA_G * NSA_DK
        ones = jnp.ones((V_ROWS - NSA_DK, Q_TILE), BF16)
        for name, aug_ref, k_ref, v_ref in (("ks", augs_ref, ks_ref, vst_ref), ("kw", augw_ref, kw_ref, vwt_ref)):
            kv = proj(offs[name], 2 * kvw)
            k = head_norm(kv[:, :kvw], offs[name])
            aug = aug_ref[...].astype(F32)
            for pair in range(NSA_G // 2):
                two = k[:, LANES * pair:LANES * (pair + 1)]
                k_ref[0, 2 * pair] = jnp.where(low, two, aug).astype(BF16)
                k_ref[0, 2 * pair + 1] = jnp.where(low, pltpu.roll(two, NSA_DK, 1), aug).astype(BF16)
            vt = kv[:, kvw:].T
            for g in range(NSA_G):
                for j in range(tiles):
                    v_ref[0, g, j, 0:NSA_DK, :] = vt[g * NSA_DK:(g + 1) * NSA_DK,
                                                     j * Q_TILE:(j + 1) * Q_TILE].astype(BF16)
                    v_ref[0, g, j, NSA_DK:V_ROWS, :] = ones

        cmp_in = proj(offs["kc"], 2 * kvw)
        for t in range(2 * kvw // LANES):
            cmp_scr[t] = cmp_in[:, LANES * t:LANES * (t + 1)]
        n_sub = ROW_TILE // CMP_STRIDE
        low_sub = lax.broadcasted_iota(jnp.int32, (n_sub, LANES), 1) < NSA_DK
        for r in range(0, CMP_STRIDE, 2):
            for i, o_ref in enumerate((kcs_ref, vcs_ref)):
                for pair in range(NSA_G // 2):
                    t = i * (NSA_G // 2) + pair
                    e2 = cmp_scr[t, pl.ds(r, n_sub, stride=CMP_STRIDE), :]
                    o2 = cmp_scr[t, pl.ds(r + 1, n_sub, stride=CMP_STRIDE), :]
                    lanes = slice(LANES * (r // 2), LANES * (r // 2 + 1))
                    o_ref[0, 2 * pair, :, lanes] = jnp.where(low_sub, e2, pltpu.roll(o2, NSA_DK, 1)).astype(BF16)
                    o_ref[0, 2 * pair + 1, :, lanes] = jnp.where(low_sub, pltpu.roll(e2, NSA_DK, 1), o2).astype(BF16)

        glt = proj(offs["gl"], LANES).T
        for g in range(NSA_G):
            glt_ref[0, g] = glt[GATE_ROWS * g:GATE_ROWS * (g + 1)]


def _nsa_proj(x2d, norm_g, w, head_gain, aug_sel, aug_win, pad_rows, offs, batch, seq):
    n = batch * seq
    nsb = seq // ROW_TILE
    tiles = ROW_TILE // Q_TILE
    qw = NSA_HEADS * NSA_DK
    kvw = NSA_G * NSA_DK
    sub_w = CMP_STRIDE * NSA_DK
    lane_head = np.arange(GROUP_W) // NSA_DK
    bd = jnp.asarray(lane_head[:, None] == lane_head[None, :], BF16)
    prev = lambda s: jnp.maximum(s - 1, 0)
    rows = lambda b, s: (b * nsb + prev(s), 0)
    out_shape = [
        jax.ShapeDtypeStruct((n, qw), BF16), jax.ShapeDtypeStruct((n, qw), BF16),
        jax.ShapeDtypeStruct((batch, NSA_G, seq // CMP_STRIDE, sub_w), BF16),
        jax.ShapeDtypeStruct((batch, NSA_G, seq // CMP_STRIDE, sub_w), BF16),
        jax.ShapeDtypeStruct((batch, NSA_G, seq, LANES), BF16),
        jax.ShapeDtypeStruct((batch, NSA_G, seq + WINDOW, LANES), BF16),
        jax.ShapeDtypeStruct((batch, NSA_G, seq // Q_TILE, V_ROWS, Q_TILE), BF16),
        jax.ShapeDtypeStruct((batch, NSA_G, (seq + WINDOW) // Q_TILE, V_ROWS, Q_TILE), BF16),
        jax.ShapeDtypeStruct((batch, NSA_G, GATE_ROWS, seq), F32),
    ]
    out_specs = [
        pl.BlockSpec((ROW_TILE, qw), rows), pl.BlockSpec((ROW_TILE, qw), rows),
        pl.BlockSpec((1, NSA_G, ROW_TILE // CMP_STRIDE, sub_w), lambda b, s: (b, 0, prev(s), 0)),
        pl.BlockSpec((1, NSA_G, ROW_TILE // CMP_STRIDE, sub_w), lambda b, s: (b, 0, prev(s), 0)),
        pl.BlockSpec((1, NSA_G, ROW_TILE, LANES), lambda b, s: (b, 0, prev(s), 0)),
        pl.BlockSpec((1, NSA_G, ROW_TILE, LANES), lambda b, s: (b, 0, s, 0)),
        pl.BlockSpec((1, NSA_G, tiles, V_ROWS, Q_TILE), lambda b, s: (b, 0, prev(s), 0, 0)),
        pl.BlockSpec((1, NSA_G, tiles, V_ROWS, Q_TILE), lambda b, s: (b, 0, s, 0, 0)),
        pl.BlockSpec((1, NSA_G, GATE_ROWS, ROW_TILE), lambda b, s: (b, 0, 0, prev(s))),
    ]
    assert WINDOW == ROW_TILE
    return pl.pallas_call(
        functools.partial(_nsa_proj_kernel, offs=offs),
        grid=(batch, nsb + 1),
        in_specs=[
            pl.BlockSpec((ROW_TILE, D_MODEL), rows),
            _const_spec((1, D_MODEL)),
            _const_spec(w.shape),
            _const_spec(head_gain.shape),
            _const_spec(bd.shape),
            pl.BlockSpec((ROW_TILE, LANES), lambda b, s: (prev(s), 0)),
            pl.BlockSpec((ROW_TILE, LANES), lambda b, s: (prev(s), 0)),
            _const_spec(pad_rows.shape),
        ],
        out_specs=out_specs,
        out_shape=out_shape,
        scratch_shapes=[pltpu.VMEM((2 * kvw // LANES, ROW_TILE, LANES), F32)],
        compiler_params=pltpu.CompilerParams(
            dimension_semantics=("arbitrary", "arbitrary"), vmem_limit_bytes=VMEM_LIMIT),
        name="nsa_norm_proj",
    )(x2d, norm_g.reshape(1, D_MODEL), w, head_gain, bd, aug_sel, aug_win, pad_rows)


def _out_kernel(a_ref, wo_ref, x_ref, p_ref, wg_ref, wp_ref, o_ref):
    x1 = x_ref[...] + _dot(a_ref[...], wo_ref[...])
    gate = jax.nn.sigmoid(_dot(x1.astype(BF16), wg_ref[...]))
    emb = _dot(p_ref[...].astype(BF16), wp_ref[...])
    o_ref[...] = x1 + gate * emb


def _out_proj(a, w_out, x2d, p2d, w_gate, w_ple):
    n, k = a.shape
    return pl.pallas_call(
        _out_kernel,
        grid=(n // ROW_TILE,),
        in_specs=[
            pl.BlockSpec((ROW_TILE, k), lambda i: (i, 0)),
            _const_spec((k, D_MODEL)),
            pl.BlockSpec((ROW_TILE, D_MODEL), lambda i: (i, 0)),
            pl.BlockSpec((ROW_TILE, PLE_DIM), lambda i: (i, 0)),
            _const_spec((D_MODEL, D_MODEL)),
            _const_spec((PLE_DIM, D_MODEL)),
        ],
        out_specs=pl.BlockSpec((ROW_TILE, D_MODEL), lambda i: (i, 0)),
        out_shape=jax.ShapeDtypeStruct((n, D_MODEL), F32),
        compiler_params=pltpu.CompilerParams(
            dimension_semantics=("arbitrary",), vmem_limit_bytes=VMEM_LIMIT),
        name="out_proj_ple",
    )(a, w_out.astype(BF16), x2d, p2d, w_gate.astype(BF16), w_ple.astype(BF16))


def _cmp_kernel(xk_ref, xv_ref, w1k_ref, w1v_ref, posk_ref, posv_ref, w2k_ref, w2v_ref,
                gk_ref, shift_ref, kc_ref, vc_ref):
    n_sub = xk_ref.shape[2]
    half = xk_ref.shape[3]
    for is_k, x_ref, w1_ref, pos_ref, w2_ref, o_ref in (
            (True, xk_ref, w1k_ref, posk_ref, w2k_ref, kc_ref),
            (False, xv_ref, w1v_ref, posv_ref, w2v_ref, vc_ref)):
        w1 = w1_ref[...]
        x = x_ref[0].reshape(NSA_G * n_sub, half)
        ab = _dot(x, w1)
        pos = pos_ref[...]
        pos_term = (_dot(pos[:, :half], w1[:, :CMP_HIDDEN])
                    + _dot(pos[:, half:], w1[:, CMP_HIDDEN:]))[0:1]
        for g in range(NSA_G):
            first = ab[g * n_sub:(g + 1) * n_sub, :CMP_HIDDEN]
            second = ab[g * n_sub:(g + 1) * n_sub, CMP_HIDDEN:]
            hid = _silu(first + pltpu.roll(second, n_sub - 1, 0) + pos_term)
            c = _dot(hid.astype(BF16), w2_ref[...])
            if is_k:
                ss = jnp.sum(c * c, axis=-1, keepdims=True) * (1.0 / NSA_DK)
                c = c * lax.rsqrt(ss + RMS_EPS) * gk_ref[...] + shift_ref[...]
                o_ref[0, g] = c.astype(BF16)
            else:
                ct = jnp.concatenate([c.T[:NSA_DK], jnp.ones((V_ROWS - NSA_DK, n_sub), F32)], axis=0)
                o_ref[0, g] = ct.astype(BF16)


def _compress(xk, xv, w1k, w1v, posk, posv, w2k, w2v, gk, k_shift):
    b, g, n_sub, half = xk.shape
    blk = pl.BlockSpec((1, g, n_sub, half), lambda i: (i, 0, 0, 0))
    k_blk = pl.BlockSpec((1, g, n_sub, LANES), lambda i: (i, 0, 0, 0))
    k_sds = jax.ShapeDtypeStruct((b, g, n_sub, LANES), BF16)
    v_blk = pl.BlockSpec((1, g, V_ROWS, n_sub), lambda i: (i, 0, 0, 0))
    v_sds = jax.ShapeDtypeStruct((b, g, V_ROWS, n_sub), BF16)
    return pl.pallas_call(
        _cmp_kernel,
        grid=(b,),
        in_specs=[blk, blk,
                  _const_spec(w1k.shape), _const_spec(w1v.shape),
                  _const_spec(posk.shape), _const_spec(posv.shape),
                  _const_spec(w2k.shape), _const_spec(w2v.shape),
                  _const_spec(gk.shape), _const_spec(k_shift.shape)],
        out_specs=[k_blk, v_blk],
        out_shape=[k_sds, v_sds],
        compiler_params=pltpu.CompilerParams(
            dimension_semantics=("arbitrary",), vmem_limit_bytes=VMEM_LIMIT),
        name="nsa_compress",
    )(xk, xv, w1k, w1v, posk, posv, w2k, w2v, gk, k_shift)


def _nsa_attn_tile(sub, fast_ref, q_ref, glt_ref, z_ref, ks_ref, vst_ref, kw_ref, vwt_ref, kc_ref, vct_ref,
                   ovt_ref, cthr_ref, kmq_ref, o_ref, acc_ref, accw_ref, rank_ref, p0_ref, p1_ref, ac_ref, impt_ref):
    tq = Q_TILE
    cols = NSA_HPG * tq
    qi = pl.program_id(2) * Q_TILES_PER_STEP + sub
    t0 = qi * tq
    rows = slice(sub * tq, (sub + 1) * tq)
    n_sel = SEL_BLOCK
    dk = NSA_DK

    qt = q_ref[rows, :].astype(F32).T
    q_heads = [qt[h * dk:(h + 1) * dk] for h in range(NSA_HPG)]

    def stack_heads(extra_rows):
        return jnp.concatenate(
            [jnp.concatenate([qh, extra_rows], axis=0) for qh in q_heads], axis=1).astype(BF16)

    blk = lax.broadcasted_iota(jnp.int32, (n_sel, tq), 0)
    ones_row = jnp.where(blk == 0, 1.0, 0.0)
    fast = fast_ref[0] > 0

    def compressed(subtract_max):
        sc = _dot(kc_ref[0, 0], stack_heads(ones_row))
        sc = jnp.where(cthr_ref[...] <= t0, sc, NEG)
        if subtract_max:
            sc = sc - jnp.maximum(jnp.max(sc, axis=0, keepdims=True), 0.1 * NEG)
        ec = jnp.exp2(sc).astype(BF16)
        r = _dot(jnp.concatenate([vct_ref[0, 0], ovt_ref[...]], axis=0), ec)
        lc = r[dk:dk + 1]
        inv_lc = 1.0 / jnp.where(lc > 0.0, lc, 1.0)
        ac_ref[...] = r[:dk] * inv_lc
        imp = r[V_ROWS:] * inv_lc
        impt_ref[...] = imp[:, 0:tq] + imp[:, tq:2 * tq] + imp[:, 2 * tq:3 * tq] + imp[:, 3 * tq:4 * tq]

    pl.when(fast)(lambda: compressed(False))
    pl.when(jnp.logical_not(fast))(lambda: compressed(True))
    imp_t = impt_ref[...]

    cur = (t0 + lax.broadcasted_iota(jnp.int32, (n_sel, tq), 1)) // SEL_BLOCK
    forced = (blk == 0) | (blk == cur) | (blk == cur - 1)
    score = jnp.where(blk <= cur, imp_t + jnp.where(forced, FORCE_BONUS, 0.0), NEG)
    sub = 8
    chunks = [score[c * sub:(c + 1) * sub] for c in range(n_sel // sub)]
    blk_sub = lax.broadcasted_iota(jnp.int32, (sub, tq), 0)
    last_blk = (t0 + tq - 1) // SEL_BLOCK
    rank_ref[...] = jnp.zeros_like(rank_ref)
    for jb in range(n_sel // sub):
        @pl.when(jb * sub <= last_blk)
        def _():
            ranks = [jnp.zeros((sub, tq), F32) for _ in chunks]
            for j in range(jb * sub, (jb + 1) * sub):
                row = jnp.broadcast_to(chunks[jb][j - jb * sub:j - jb * sub + 1, :], (sub, tq))
                for c, chunk in enumerate(chunks):
                    if c > jb:
                        one = jnp.where(row >= chunk, 1.0, 0.0)
                    elif c < jb:
                        one = jnp.where(row > chunk, 1.0, 0.0)
                    else:
                        one = jnp.where(blk_sub > j - c * sub,
                                        jnp.where(row >= chunk, 1.0, 0.0), jnp.where(row > chunk, 1.0, 0.0))
                    ranks[c] = ranks[c] + one
            rank_ref[...] += jnp.concatenate(ranks, axis=0)
    bias_t = jnp.where(rank_ref[...] < float(SEL_TOPN), 0.0, SEL_MASK_BIAS)
    q_aug = stack_heads(jnp.where(blk == 0, 1.0, bias_t))

    tiles_per_step = SEL_KEYS // tq
    n_full = t0 // SEL_KEYS
    n_wt = WINDOW // tq + 1
    diag_keep = kmq_ref[0:tq, :] <= 0

    def sel_scores(step, tail):
        base = pl.multiple_of(step * SEL_KEYS, SEL_KEYS)
        s = _dot(ks_ref[0, 0, pl.ds(base, SEL_KEYS), :], q_aug)
        if tail:
            s = jnp.where(kmq_ref[...] <= t0 - base, s, NEG)
        vt = vst_ref[0, 0, pl.ds(step * tiles_per_step, tiles_per_step)]
        return s, jnp.concatenate([vt[i] for i in range(tiles_per_step)], axis=1)

    def win_scores():
        sw = _dot(kw_ref[0, 0, pl.ds(pl.multiple_of(t0, tq), n_wt * tq), :], q_aug)
        parts = [jnp.where(diag_keep, NEG, sw[:tq])]
        if WINDOW > tq:
            parts.append(sw[tq:WINDOW])
        parts.append(jnp.where(diag_keep, sw[WINDOW:], NEG))
        sw = jnp.concatenate(parts, axis=0)
        vwt = vwt_ref[0, 0, pl.ds(qi, n_wt)]
        return sw, jnp.concatenate([vwt[i] for i in range(n_wt)], axis=1)

    @pl.when(fast)
    def _():
        acc_ref[...] = jnp.zeros_like(acc_ref)

        def values(step):
            vt = vst_ref[0, 0, pl.ds(step * tiles_per_step, tiles_per_step)]
            return jnp.concatenate([vt[i] for i in range(tiles_per_step)], axis=1)

        def probs(step, tail):
            return jnp.exp2(sel_scores(step, tail)[0]).astype(BF16)

        odd = n_full % 2
        half = SEL_KEYS // 2
        assert half == tq

        @pl.when(t0 % SEL_KEYS != 0)
        def _():
            p1_ref[...] = probs(n_full, True)

        @pl.when(t0 % SEL_KEYS == 0)
        def _():
            s = _dot(ks_ref[0, 0, pl.ds(pl.multiple_of(t0, SEL_KEYS), half), :], q_aug)
            p1_ref[0:half, :] = jnp.exp2(jnp.where(diag_keep, s, NEG)).astype(BF16)
            p1_ref[half:, :] = jnp.zeros((half, cols), BF16)

        @pl.when(odd == 1)
        def _():
            p = probs(n_full - 1, False)
            acc_ref[...] += _dot(values(n_full), p1_ref[...])
            p1_ref[...] = p

        first_pending = n_full - odd

        def pair(j, carry):
            pending = jnp.where(j == 0, first_pending, 2 * j - 1)
            pa = probs(2 * j, False)
            acc_ref[...] += _dot(values(pending), p1_ref[...])
            p0_ref[...] = pa
            pb = probs(2 * j + 1, False)
            acc_ref[...] += _dot(values(2 * j), p0_ref[...])
            p1_ref[...] = pb
            return carry

        n_pairs = n_full // 2
        lax.fori_loop(0, n_pairs, pair, 0)
        pending = jnp.where(n_pairs == 0, first_pending, 2 * n_pairs - 1)
        sw, vwt = win_scores()
        pw = jnp.exp2(sw).astype(BF16)
        acc_ref[...] += _dot(values(pending), p1_ref[...])
        accw_ref[...] = _dot(vwt, pw)

    @pl.when(jnp.logical_not(fast))
    def _():
        acc_ref[...] = jnp.zeros_like(acc_ref)

        def online(s, vt, m):
            m_new = jnp.maximum(m, jnp.max(s, axis=0, keepdims=True))
            acc_ref[...] = jnp.exp2(m - m_new) * acc_ref[...] + _dot(vt, jnp.exp2(s - m_new).astype(BF16))
            return m_new

        m = jnp.full((1, cols), NEG, F32)
        m = lax.fori_loop(0, n_full, lambda i, m: online(*sel_scores(i, False), m), m)
        online(*sel_scores(n_full, True), m)
        sw, vwt = win_scores()
        accw_ref[...] = _dot(vwt, jnp.exp2(sw - jnp.max(sw, axis=0, keepdims=True)).astype(BF16))

    a_s = acc_ref[...]
    a_w = accw_ref[...]

    a_s = a_s[:dk] * (1.0 / a_s[dk:dk + 1])
    a_w = a_w[:dk] * (1.0 / a_w[dk:dk + 1])
    gates = jax.nn.sigmoid(glt_ref[0, 0, :, rows])
    outs = []
    for h in range(NSA_HPG):
        sl = slice(h * tq, (h + 1) * tq)
        outs.append(gates[3 * h:3 * h + 1] * ac_ref[:, sl] + gates[3 * h + 1:3 * h + 2] * a_s[:, sl]
                    + gates[3 * h + 2:3 * h + 3] * a_w[:, sl])
    out = jnp.concatenate(outs, axis=0).T
    o_ref[rows, :] = (out * _silu(z_ref[rows, :].astype(F32))).astype(o_ref.dtype)


def _nsa_attn_kernel(*refs):
    for sub in range(Q_TILES_PER_STEP):
        _nsa_attn_tile(sub, *refs)


def _nsa_attention(fast, q, z, glt, ks_aug, vst, kw_pad, vwt, kc_pad, vct, overlap_t, batch, seq):
    cols = NSA_HPG * Q_TILE
    query = np.arange(cols)[None, :] % Q_TILE
    cmp_thr = jnp.asarray(np.arange(kc_pad.shape[2])[:, None] * CMP_STRIDE + (CMP_BLOCK - 1) - query, jnp.int32)
    key_minus_query = jnp.asarray(np.arange(SEL_KEYS)[:, None] - query, jnp.int32)
    n = batch * seq
    step_q = Q_TILES_PER_STEP * Q_TILE
    n_q = seq // step_q
    gw = NSA_HPG * NSA_DK

    def per_group(arr):
        nd = arr.ndim - 2
        return pl.BlockSpec((1, 1) + arr.shape[2:], lambda b, g, i: (b, g) + (0,) * nd)

    return pl.pallas_call(
        _nsa_attn_kernel,
        grid=(batch, NSA_G, n_q),
        in_specs=[
            pl.BlockSpec(memory_space=pltpu.SMEM),
            pl.BlockSpec((step_q, gw), lambda b, g, i: (b * n_q + i, g)),
            pl.BlockSpec((1, 1, glt.shape[2], step_q), lambda b, g, i: (b, g, 0, i)),
            pl.BlockSpec((step_q, gw), lambda b, g, i: (b * n_q + i, g)),
            per_group(ks_aug), per_group(vst), per_group(kw_pad), per_group(vwt),
            per_group(kc_pad), per_group(vct),
            _const_spec(overlap_t.shape), _const_spec(cmp_thr.shape), _const_spec(key_minus_query.shape),
        ],
        out_specs=pl.BlockSpec((step_q, gw), lambda b, g, i: (b * n_q + i, g)),
        out_shape=jax.ShapeDtypeStruct((n, NSA_HEADS * NSA_DK), BF16),
        scratch_shapes=([pltpu.VMEM((V_ROWS, cols), F32)] * 2 + [pltpu.VMEM((SEL_BLOCK, Q_TILE), F32)]
                        + [pltpu.VMEM((SEL_KEYS, cols), BF16)] * 2
                        + [pltpu.VMEM((NSA_DK, cols), F32), pltpu.VMEM((SEL_BLOCK, Q_TILE), F32)]),
        compiler_params=pltpu.CompilerParams(
            dimension_semantics=("arbitrary", "arbitrary", "arbitrary"),
            vmem_limit_bytes=VMEM_LIMIT),
        name="nsa_attention",
    )(fast, q, glt, z, ks_aug, vst, kw_pad, vwt, kc_pad, vct, overlap_t, cmp_thr, key_minus_query)


def _ret_kernel(q_ref, k_ref, v_ref, z_ref, dec_ref, qd_ref, kd_ref, cd_ref, o_ref, st_ref):
    @pl.when(pl.program_id(1) == 0)
    def _():
        st_ref[...] = jnp.zeros_like(st_ref)

    for h in range(RET_HEADS):
        q = q_ref[:, h * RET_DK:(h + 1) * RET_DK]
        k = k_ref[:, h * RET_DK:(h + 1) * RET_DK]
        v = v_ref[:, h * RET_DV:(h + 1) * RET_DV]
        att = _dot_nt(q, k) * dec_ref[h]
        state = st_ref[h]
        o = _dot(att.astype(BF16), v) + _dot(q, state.astype(BF16)) * qd_ref[h]
        kd_t = (k.astype(F32) * kd_ref[h]).T.astype(BF16)
        st_ref[h] = state * cd_ref[h] + _dot(kd_t, v)
        mu = jnp.mean(o, axis=-1, keepdims=True)
        d = o - mu
        var = jnp.mean(d * d, axis=-1, keepdims=True)
        z = z_ref[:, h * RET_DV:(h + 1) * RET_DV].astype(F32)
        o_ref[:, h * RET_DV:(h + 1) * RET_DV] = (d * lax.rsqrt(var + GN_EPS) * _silu(z)).astype(o_ref.dtype)


def _retention(pr, tables, batch, seq):
    n = batch * seq
    n_c = seq // RET_CHUNK
    dec, qd, kd, cd = tables
    qk_w = RET_HEADS * RET_DK
    v_w = RET_HEADS * RET_DV
    row = lambda b, c: b * n_c + c
    return pl.pallas_call(
        _ret_kernel,
        grid=(batch, n_c),
        in_specs=[
            pl.BlockSpec((RET_CHUNK, qk_w), lambda b, c: (row(b, c), 0)),
            pl.BlockSpec((RET_CHUNK, qk_w), lambda b, c: (row(b, c), 1)),
            pl.BlockSpec((RET_CHUNK, v_w), lambda b, c: (row(b, c), 1)),
            pl.BlockSpec((RET_CHUNK, v_w), lambda b, c: (row(b, c), 2)),
            _const_spec(dec.shape), _const_spec(qd.shape), _const_spec(kd.shape), _const_spec(cd.shape),
        ],
        out_specs=pl.BlockSpec((RET_CHUNK, v_w), lambda b, c: (row(b, c), 0)),
        out_shape=jax.ShapeDtypeStruct((n, v_w), BF16),
        scratch_shapes=[pltpu.VMEM((RET_HEADS, RET_DK, RET_DV), F32)],
        compiler_params=pltpu.CompilerParams(
            dimension_semantics=("arbitrary", "arbitrary"), vmem_limit_bytes=VMEM_LIMIT),
        name="retention",
    )(pr, pr, pr, pr, dec, qd, kd, cd)


def _overlap_matrix_t(n_cmp_pad):
    i = np.arange(n_cmp_pad)[None, :]
    j = np.arange(SEL_BLOCK)[:, None]
    ov = (i * CMP_STRIDE < (j + 1) * SEL_BLOCK) & (i * CMP_STRIDE + CMP_BLOCK > j * SEL_BLOCK)
    return jnp.asarray(ov, BF16)


def _rotary_tables(seq):
    half = RET_DK // 2
    inv = ROPE_BASE ** (-jnp.linspace(0.0, 1.0, half, dtype=F32))
    ang = jnp.arange(seq, dtype=F32)[:, None] * inv[None, :]
    return jnp.cos(ang), jnp.sin(ang)


def _retention_tables():
    c = RET_CHUNK
    log_g = jnp.log(1.0 - 2.0 ** (-5.0 - jnp.arange(RET_HEADS, dtype=F32)))
    ix = jnp.arange(c, dtype=F32)
    diff = ix[:, None] - ix[None, :]
    dec = jnp.where(diff >= 0, jnp.exp(log_g[:, None, None] * jnp.maximum(diff, 0.0)), 0.0)
    qd = jnp.exp(log_g[:, None] * (ix + 1.0))[:, :, None]
    kd = jnp.exp(log_g[:, None] * (c - 1.0 - ix))[:, :, None]
    cd = jnp.broadcast_to(jnp.exp(log_g * c)[:, None, None], (RET_HEADS, 1, RET_DV))
    return dec, qd, kd, cd


def _nsa_layer(x2d, p2d, batch, seq, norm_g, w_in, q_g, kc_g, ks_g, kw_g, pos_k, pos_v,
               ck_w1, ck_w2, cv_w1, cv_w2, w_out, ple_w, ple_gate_w):
    qw = NSA_HEADS * NSA_DK
    kvw = NSA_G * NSA_DK
    n_gate = 3 * NSA_HPG
    sizes = [qw] + [kvw] * 6 + [3 * NSA_HEADS, qw]
    splits = np.concatenate([[0], np.cumsum(sizes)])
    names = ["q", "kc", "vc", "ks", "vs", "kw", "vw"]
    offs = {nm: int(splits[i]) for i, nm in enumerate(names)}
    front = int(splits[7])
    offs["z"], offs["gl"] = front, front + qw
    wgl = w_in[:, splits[7]:splits[8]].reshape(D_MODEL, NSA_G, n_gate)
    wgl = jnp.pad(wgl, ((0, 0), (0, 0), (0, GATE_ROWS - n_gate))).reshape(D_MODEL, NSA_G * GATE_ROWS)
    wgl = jnp.pad(wgl, ((0, 0), (0, LANES - NSA_G * GATE_ROWS)))
    w = jnp.concatenate([w_in[:, :front], w_in[:, splits[8]:], wgl], axis=1).astype(BF16)
    zeros = lambda width: jnp.zeros((width,), F32)
    head_gain = jnp.concatenate([
        jnp.tile(q_g, NSA_HEADS) * (NSA_DK ** -0.5 * LOG2E), zeros(2 * kvw),
        jnp.tile(ks_g, NSA_G), zeros(kvw), jnp.tile(kw_g, NSA_G), zeros(w.shape[1] - offs["vw"])]).reshape(1, -1)

    q_norm = LOG2E * jnp.max(jnp.abs(q_g)) * 1.02
    bound_sel = (q_norm * NSA_DK ** 0.5 * jnp.max(jnp.abs(ks_g))).astype(BF16).astype(F32) * 1.01
    bound_win = (q_norm * NSA_DK ** 0.5 * jnp.max(jnp.abs(kw_g))).astype(BF16).astype(F32) * 1.01
    bound_cmp = (q_norm * NSA_DK ** 0.5 * jnp.max(jnp.abs(kc_g))).astype(BF16).astype(F32) * 1.01
    bound_max = jnp.maximum(jnp.maximum(bound_sel, bound_win), bound_cmp)
    fast = (bound_max <= MAX_SAFE_BOUND).astype(jnp.int32).reshape(1)

    blk_id = np.arange(seq)[:, None] // SEL_BLOCK
    upper = np.arange(LANES)[None, :] - NSA_DK
    onehot = jnp.asarray((blk_id == upper) & (upper > 0), F32)
    lane64 = jnp.asarray(upper == 0)
    aug_sel = jnp.where(lane64, -bound_sel, onehot).astype(BF16)
    aug_win = jnp.where(lane64, -bound_win, jnp.zeros((seq, LANES), F32)).astype(BF16)
    pad_rows = jnp.where(lane64, SEL_MASK_BIAS, jnp.zeros((WINDOW, LANES), F32)).astype(BF16)

    q, z, kc_sub, vc_sub, ks_aug, kw_pad, vst, vwt, glt = _nsa_proj(
        x2d, norm_g, w, head_gain, aug_sel, aug_win, pad_rows, offs, batch, seq)

    n_sub = seq // CMP_STRIDE
    half = CMP_STRIDE * NSA_DK
    w1cat = lambda w1: jnp.concatenate([w1[:half], w1[half:]], axis=1).astype(BF16)
    pos_flat = lambda pos: jnp.pad(pos.reshape(1, CMP_BLOCK * NSA_DK), ((0, 7), (0, 0))).astype(BF16)
    w2pad = lambda w2: jnp.pad(w2, ((0, 0), (0, LANES - NSA_DK))).astype(BF16)
    gk = jnp.pad(kc_g, (0, LANES - NSA_DK)).reshape(1, LANES)
    k_shift = jnp.where(lane64, -bound_cmp, 0.0)
    kc_pad, vct = _compress(kc_sub, vc_sub, w1cat(ck_w1), w1cat(cv_w1),
                            pos_flat(pos_k), pos_flat(pos_v), w2pad(ck_w2), w2pad(cv_w2), gk, k_shift)

    a = _nsa_attention(fast, q, z, glt, ks_aug, vst, kw_pad, vwt, kc_pad, vct,
                       _overlap_matrix_t(n_sub), batch, seq)
    return _out_proj(a, w_out, x2d, p2d, ple_gate_w, ple_w)


def _ret_layer(x2d, p2d, batch, seq, norm_g, w_in, w_out, ple_w, ple_gate_w):
    cos, sin = _rotary_tables(seq)
    pr = _ret_proj(x2d, norm_g, w_in.astype(BF16), cos, sin, seq)
    a = _retention(pr, _retention_tables(), batch, seq)
    return _out_proj(a, w_out, x2d, p2d, ple_gate_w, ple_w)


def kernel(x, p, norm_g, nsa_w_in, nsa_q_g, nsa_kc_g, nsa_ks_g, nsa_kw_g, nsa_cmp_pos_k, nsa_cmp_pos_v, nsa_cmp_k_w1, nsa_cmp_k_w2, nsa_cmp_v_w1, nsa_cmp_v_w2, nsa_w_out, ret_w_in, ret_w_out, ple_w, ple_gate_w):
    batch, seq, d_model = x.shape
    depth = p.shape[0]
    n = batch * seq
    x2d = x.reshape(n, d_model)
    for i in range(depth):
        p2d = p[i].reshape(n, PLE_DIM)
        j = i // 2
        if i % 2 == 0:
            x2d = _nsa_layer(x2d, p2d, batch, seq, norm_g[i], nsa_w_in[j], nsa_q_g[j], nsa_kc_g[j],
                             nsa_ks_g[j], nsa_kw_g[j], nsa_cmp_pos_k[j], nsa_cmp_pos_v[j],
                             nsa_cmp_k_w1[j], nsa_cmp_k_w2[j], nsa_cmp_v_w1[j], nsa_cmp_v_w2[j],
                             nsa_w_out[j], ple_w[i], ple_gate_w[i])
        else:
            x2d = _ret_layer(x2d, p2d, batch, seq, norm_g[i], ret_w_in[j], ret_w_out[j],
                             ple_w[i], ple_gate_w[i])
    return x2d.reshape(batch, seq, d_model)
```

```python
import functools

import numpy as np
import jax
import jax.numpy as jnp
from jax import lax
from jax.experimental import pallas as pl
from jax.experimental.pallas import tpu as pltpu

F32 = jnp.float32
BF16 = jnp.bfloat16

D_MODEL = 1024
PLE_DIM = 256
RMS_EPS = 1e-6
GN_EPS = 1e-5

NSA_HEADS = 16
NSA_DK = 64
NSA_G = 4
NSA_HPG = 4
CMP_BLOCK = 32
CMP_STRIDE = 16
CMP_HIDDEN = 256
SEL_BLOCK = 64
SEL_TOPN = 16
WINDOW = 512
FORCE_BONUS = 1e4
NEG = -1e30
SEL_MASK_BIAS = -30000.0

RET_HEADS = 4
RET_DK = 256
RET_DV = 512
RET_CHUNK = 256
ROPE_BASE = 10000.0

LANES = 128
VMEM_LIMIT = 56 * 1024 * 1024

ROW_TILE = 512
GROUP_W = NSA_HPG * NSA_DK
GATE_ROWS = 16
Q_TILE = 256
SEL_KEYS = 512
LOG2E = 1.4426950408889634
MAX_SAFE_BOUND = 50.0
V_ROWS = NSA_DK + 16


def _dot(a, b):
    return jnp.dot(a, b, preferred_element_type=F32)


def _dot_nt(a, b):
    return lax.dot_general(a, b, (((1,), (1,)), ((), ())), preferred_element_type=F32)


def _silu(x):
    return x * jax.nn.sigmoid(x)


def _const_spec(shape):
    nd = len(shape)
    return pl.BlockSpec(shape, lambda *_: (0,) * nd)


def _ret_proj_kernel(x_ref, g_ref, w_ref, cos_ref, sin_ref, o_ref):
    x = x_ref[...]
    h = (x * lax.rsqrt(jnp.mean(x * x, axis=-1, keepdims=True) + RMS_EPS) * g_ref[...]).astype(BF16)
    cos, sin = cos_ref[...], sin_ref[...]
    half = RET_DK // 2
    qk_w = RET_HEADS * RET_DK
    for t in range(2 * RET_HEADS):
        c0 = t * RET_DK
        acc = _dot(h, w_ref[:, c0:c0 + RET_DK])
        a1, a2 = acc[:, :half], acc[:, half:]
        rot = jnp.concatenate([a1 * cos - a2 * sin, a1 * sin + a2 * cos], axis=-1)
        if c0 >= qk_w:
            rot = rot * RET_DK ** -0.5
        o_ref[:, c0:c0 + RET_DK] = rot.astype(o_ref.dtype)
    for c0 in range(2 * qk_w, w_ref.shape[1], RET_DV):
        o_ref[:, c0:c0 + RET_DV] = _dot(h, w_ref[:, c0:c0 + RET_DV]).astype(o_ref.dtype)


def _ret_proj(x2d, norm_g, w, cos, sin, seq):
    n = x2d.shape[0]
    cols = w.shape[1]
    n_pos_blocks = seq // ROW_TILE
    return pl.pallas_call(
        _ret_proj_kernel,
        grid=(n // ROW_TILE,),
        in_specs=[
            pl.BlockSpec((ROW_TILE, D_MODEL), lambda i: (i, 0)),
            _const_spec((1, D_MODEL)),
            _const_spec((D_MODEL, cols)),
            pl.BlockSpec((ROW_TILE, LANES), lambda i: (i % n_pos_blocks, 0)),
            pl.BlockSpec((ROW_TILE, LANES), lambda i: (i % n_pos_blocks, 0)),
        ],
        out_specs=pl.BlockSpec((ROW_TILE, cols), lambda i: (i, 0)),
        out_shape=jax.ShapeDtypeStruct((n, cols), BF16),
        compiler_params=pltpu.CompilerParams(
            dimension_semantics=("arbitrary",), vmem_limit_bytes=VMEM_LIMIT),
        name="ret_norm_proj",
    )(x2d, norm_g.reshape(1, D_MODEL), w, cos, sin)


def _nsa_proj_kernel(x_ref, g_ref, w_ref, hg_ref, bd_ref, augs_ref, augw_ref, pad_ref,
                     q_ref, z_ref, kcs_ref, vcs_ref, ks_ref, kw_ref, vst_ref, vwt_ref, glt_ref, cmp_scr, *, offs):
    s = pl.program_id(1)
    tiles = ROW_TILE // Q_TILE

    @pl.when(s == 0)
    def _():
        for g in range(NSA_G):
            kw_ref[0, g] = pad_ref[...]
        vwt_ref[...] = jnp.zeros_like(vwt_ref)

    @pl.when(s > 0)
    def _():
        x = x_ref[...]
        h = (x * lax.rsqrt(jnp.mean(x * x, axis=-1, keepdims=True) + RMS_EPS) * g_ref[...]).astype(BF16)
        low = lax.broadcasted_iota(jnp.int32, (ROW_TILE, LANES), 1) < NSA_DK

        def proj(c0, width=GROUP_W):
            return _dot(h, w_ref[:, c0:c0 + width])

        def head_norm(acc, c0):
            ss = _dot((acc * acc).astype(BF16), bd_ref[...])
            return acc * lax.rsqrt(ss * (1.0 / NSA_DK) + RMS_EPS) * hg_ref[:, c0:c0 + GROUP_W]

        for t in range(NSA_G):
            lanes = slice(GROUP_W * t, GROUP_W * (t + 1))
            c0 = offs["q"] + GROUP_W * t
            q_ref[:, lanes] = head_norm(proj(c0), c0).astype(BF16)
            z_ref[:, lanes] = proj(offs["z"] + GROUP_W * t).astype(BF16)

        kvw = NSA_G * NSA_DK
        ones = jnp.ones((V_ROWS - NSA_DK, Q_TILE), BF16)
        for name, aug_ref, k_ref, v_ref in (("ks", augs_ref, ks_ref, vst_ref), ("kw", augw_ref, kw_ref, vwt_ref)):
            kv = proj(offs[name], 2 * kvw)
            k = head_norm(kv[:, :kvw], offs[name])
            aug = aug_ref[...].astype(F32)
            for pair in range(NSA_G // 2):
                two = k[:, LANES * pair:LANES * (pair + 1)]
                k_ref[0, 2 * pair] = jnp.where(low, two, aug).astype(BF16)
                k_ref[0, 2 * pair + 1] = jnp.where(low, pltpu.roll(two, NSA_DK, 1), aug).astype(BF16)
            vt = kv[:, kvw:].T
            for g in range(NSA_G):
                for j in range(tiles):
                    v_ref[0, g, j, 0:NSA_DK, :] = vt[g * NSA_DK:(g + 1) * NSA_DK,
                                                     j * Q_TILE:(j + 1) * Q_TILE].astype(BF16)
                    v_ref[0, g, j, NSA_DK:V_ROWS, :] = ones

        cmp_in = proj(offs["kc"], 2 * kvw)
        for t in range(2 * kvw // LANES):
            cmp_scr[t] = cmp_in[:, LANES * t:LANES * (t + 1)]
        n_sub = ROW_TILE // CMP_STRIDE
        low_sub = lax.broadcasted_iota(jnp.int32, (n_sub, LANES), 1) < NSA_DK
        for r in range(0, CMP_STRIDE, 2):
            for i, o_ref in enumerate((kcs_ref, vcs_ref)):
                for pair in range(NSA_G // 2):
                    t = i * (NSA_G // 2) + pair
                    e2 = cmp_scr[t, pl.ds(r, n_sub, stride=CMP_STRIDE), :]
                    o2 = cmp_scr[t, pl.ds(r + 1, n_sub, stride=CMP_STRIDE), :]
                    lanes = slice(LANES * (r // 2), LANES * (r // 2 + 1))
                    o_ref[0, 2 * pair, :, lanes] = jnp.where(low_sub, e2, pltpu.roll(o2, NSA_DK, 1)).astype(BF16)
                    o_ref[0, 2 * pair + 1, :, lanes] = jnp.where(low_sub, pltpu.roll(e2, NSA_DK, 1), o2).astype(BF16)

        glt = proj(offs["gl"], LANES).T
        for g in range(NSA_G):
            for j in range(tiles):
                glt_ref[0, g, j] = glt[GATE_ROWS * g:GATE_ROWS * (g + 1), j * Q_TILE:(j + 1) * Q_TILE]


def _nsa_proj(x2d, norm_g, w, head_gain, aug_sel, aug_win, pad_rows, offs, batch, seq):
    n = batch * seq
    nsb = seq // ROW_TILE
    tiles = ROW_TILE // Q_TILE
    qw = NSA_HEADS * NSA_DK
    kvw = NSA_G * NSA_DK
    sub_w = CMP_STRIDE * NSA_DK
    lane_head = np.arange(GROUP_W) // NSA_DK
    bd = jnp.asarray(lane_head[:, None] == lane_head[None, :], BF16)
    prev = lambda s: jnp.maximum(s - 1, 0)
    rows = lambda b, s: (b * nsb + prev(s), 0)
    out_shape = [
        jax.ShapeDtypeStruct((n, qw), BF16), jax.ShapeDtypeStruct((n, qw), BF16),
        jax.ShapeDtypeStruct((batch, NSA_G, seq // CMP_STRIDE, sub_w), BF16),
        jax.ShapeDtypeStruct((batch, NSA_G, seq // CMP_STRIDE, sub_w), BF16),
        jax.ShapeDtypeStruct((batch, NSA_G, seq, LANES), BF16),
        jax.ShapeDtypeStruct((batch, NSA_G, seq + WINDOW, LANES), BF16),
        jax.ShapeDtypeStruct((batch, NSA_G, seq // Q_TILE, V_ROWS, Q_TILE), BF16),
        jax.ShapeDtypeStruct((batch, NSA_G, (seq + WINDOW) // Q_TILE, V_ROWS, Q_TILE), BF16),
        jax.ShapeDtypeStruct((batch, NSA_G, seq // Q_TILE, GATE_ROWS, Q_TILE), F32),
    ]
    out_specs = [
        pl.BlockSpec((ROW_TILE, qw), rows), pl.BlockSpec((ROW_TILE, qw), rows),
        pl.BlockSpec((1, NSA_G, ROW_TILE // CMP_STRIDE, sub_w), lambda b, s: (b, 0, prev(s), 0)),
        pl.BlockSpec((1, NSA_G, ROW_TILE // CMP_STRIDE, sub_w), lambda b, s: (b, 0, prev(s), 0)),
        pl.BlockSpec((1, NSA_G, ROW_TILE, LANES), lambda b, s: (b, 0, prev(s), 0)),
        pl.BlockSpec((1, NSA_G, ROW_TILE, LANES), lambda b, s: (b, 0, s, 0)),
        pl.BlockSpec((1, NSA_G, tiles, V_ROWS, Q_TILE), lambda b, s: (b, 0, prev(s), 0, 0)),
        pl.BlockSpec((1, NSA_G, tiles, V_ROWS, Q_TILE), lambda b, s: (b, 0, s, 0, 0)),
        pl.BlockSpec((1, NSA_G, tiles, GATE_ROWS, Q_TILE), lambda b, s: (b, 0, prev(s), 0, 0)),
    ]
    assert WINDOW == ROW_TILE
    return pl.pallas_call(
        functools.partial(_nsa_proj_kernel, offs=offs),
        grid=(batch, nsb + 1),
        in_specs=[
            pl.BlockSpec((ROW_TILE, D_MODEL), rows),
            _const_spec((1, D_MODEL)),
            _const_spec(w.shape),
            _const_spec(head_gain.shape),
            _const_spec(bd.shape),
            pl.BlockSpec((ROW_TILE, LANES), lambda b, s: (prev(s), 0)),
            pl.BlockSpec((ROW_TILE, LANES), lambda b, s: (prev(s), 0)),
            _const_spec(pad_rows.shape),
        ],
        out_specs=out_specs,
        out_shape=out_shape,
        scratch_shapes=[pltpu.VMEM((2 * kvw // LANES, ROW_TILE, LANES), F32)],
        compiler_params=pltpu.CompilerParams(
            dimension_semantics=("arbitrary", "arbitrary"), vmem_limit_bytes=VMEM_LIMIT),
        name="nsa_norm_proj",
    )(x2d, norm_g.reshape(1, D_MODEL), w, head_gain, bd, aug_sel, aug_win, pad_rows)


def _out_kernel(a_ref, wo_ref, x_ref, p_ref, wg_ref, wp_ref, o_ref):
    x1 = x_ref[...] + _dot(a_ref[...], wo_ref[...])
    gate = jax.nn.sigmoid(_dot(x1.astype(BF16), wg_ref[...]))
    emb = _dot(p_ref[...].astype(BF16), wp_ref[...])
    o_ref[...] = x1 + gate * emb


def _out_proj(a, w_out, x2d, p2d, w_gate, w_ple):
    n, k = a.shape
    return pl.pallas_call(
        _out_kernel,
        grid=(n // ROW_TILE,),
        in_specs=[
            pl.BlockSpec((ROW_TILE, k), lambda i: (i, 0)),
            _const_spec((k, D_MODEL)),
            pl.BlockSpec((ROW_TILE, D_MODEL), lambda i: (i, 0)),
            pl.BlockSpec((ROW_TILE, PLE_DIM), lambda i: (i, 0)),
            _const_spec((D_MODEL, D_MODEL)),
            _const_spec((PLE_DIM, D_MODEL)),
        ],
        out_specs=pl.BlockSpec((ROW_TILE, D_MODEL), lambda i: (i, 0)),
        out_shape=jax.ShapeDtypeStruct((n, D_MODEL), F32),
        compiler_params=pltpu.CompilerParams(
            dimension_semantics=("arbitrary",), vmem_limit_bytes=VMEM_LIMIT),
        name="out_proj_ple",
    )(a, w_out.astype(BF16), x2d, p2d, w_gate.astype(BF16), w_ple.astype(BF16))


def _cmp_kernel(xk_ref, xv_ref, w1k_ref, w1v_ref, posk_ref, posv_ref, w2k_ref, w2v_ref,
                gk_ref, shift_ref, kc_ref, vc_ref):
    n_sub = xk_ref.shape[2]
    half = xk_ref.shape[3]
    for is_k, x_ref, w1_ref, pos_ref, w2_ref, o_ref in (
            (True, xk_ref, w1k_ref, posk_ref, w2k_ref, kc_ref),
            (False, xv_ref, w1v_ref, posv_ref, w2v_ref, vc_ref)):
        w1 = w1_ref[...]
        x = x_ref[0].reshape(NSA_G * n_sub, half)
        ab = _dot(x, w1)
        pos = pos_ref[...]
        pos_term = (_dot(pos[:, :half], w1[:, :CMP_HIDDEN])
                    + _dot(pos[:, half:], w1[:, CMP_HIDDEN:]))[0:1]
        for g in range(NSA_G):
            first = ab[g * n_sub:(g + 1) * n_sub, :CMP_HIDDEN]
            second = ab[g * n_sub:(g + 1) * n_sub, CMP_HIDDEN:]
            hid = _silu(first + pltpu.roll(second, n_sub - 1, 0) + pos_term)
            c = _dot(hid.astype(BF16), w2_ref[...])
            if is_k:
                ss = jnp.sum(c * c, axis=-1, keepdims=True) * (1.0 / NSA_DK)
                c = c * lax.rsqrt(ss + RMS_EPS) * gk_ref[...] + shift_ref[...]
                o_ref[0, g] = c.astype(BF16)
            else:
                ct = jnp.concatenate([c.T[:NSA_DK], jnp.ones((V_ROWS - NSA_DK, n_sub), F32)], axis=0)
                o_ref[0, g] = ct.astype(BF16)


def _compress(xk, xv, w1k, w1v, posk, posv, w2k, w2v, gk, k_shift):
    b, g, n_sub, half = xk.shape
    blk = pl.BlockSpec((1, g, n_sub, half), lambda i: (i, 0, 0, 0))
    k_blk = pl.BlockSpec((1, g, n_sub, LANES), lambda i: (i, 0, 0, 0))
    k_sds = jax.ShapeDtypeStruct((b, g, n_sub, LANES), BF16)
    v_blk = pl.BlockSpec((1, g, V_ROWS, n_sub), lambda i: (i, 0, 0, 0))
    v_sds = jax.ShapeDtypeStruct((b, g, V_ROWS, n_sub), BF16)
    return pl.pallas_call(
        _cmp_kernel,
        grid=(b,),
        in_specs=[blk, blk,
                  _const_spec(w1k.shape), _const_spec(w1v.shape),
                  _const_spec(posk.shape), _const_spec(posv.shape),
                  _const_spec(w2k.shape), _const_spec(w2v.shape),
                  _const_spec(gk.shape), _const_spec(k_shift.shape)],
        out_specs=[k_blk, v_blk],
        out_shape=[k_sds, v_sds],
        compiler_params=pltpu.CompilerParams(
            dimension_semantics=("arbitrary",), vmem_limit_bytes=VMEM_LIMIT),
        name="nsa_compress",
    )(xk, xv, w1k, w1v, posk, posv, w2k, w2v, gk, k_shift)


def _nsa_attn_tile(qi, fast_ref, q_ref, glt_ref, z_ref, ks_ref, vst_ref, kw_ref, vwt_ref, kc_ref, vct_ref,
                   ovt_ref, cthr_ref, kmq_ref, o_ref, acc_ref, accw_ref, rank_ref, p0_ref, p1_ref, ac_ref, impt_ref):
    tq = Q_TILE
    cols = NSA_HPG * tq
    t0 = qi * tq
    rows = pl.ds(pl.multiple_of(t0, tq), tq)
    n_sel = SEL_BLOCK
    dk = NSA_DK

    qt = q_ref[rows, :].astype(F32).T
    q_heads = [qt[h * dk:(h + 1) * dk] for h in range(NSA_HPG)]

    def stack_heads(extra_rows):
        return jnp.concatenate(
            [jnp.concatenate([qh, extra_rows], axis=0) for qh in q_heads], axis=1).astype(BF16)

    blk = lax.broadcasted_iota(jnp.int32, (n_sel, tq), 0)
    ones_row = jnp.where(blk == 0, 1.0, 0.0)
    fast = fast_ref[0] > 0

    def compressed(subtract_max):
        sc = _dot(kc_ref[0, 0], stack_heads(ones_row))
        sc = jnp.where(cthr_ref[...] <= t0, sc, NEG)
        if subtract_max:
            sc = sc - jnp.maximum(jnp.max(sc, axis=0, keepdims=True), 0.1 * NEG)
        ec = jnp.exp2(sc).astype(BF16)
        r = _dot(jnp.concatenate([vct_ref[0, 0], ovt_ref[...]], axis=0), ec)
        lc = r[dk:dk + 1]
        inv_lc = 1.0 / jnp.where(lc > 0.0, lc, 1.0)
        ac_ref[...] = r[:dk] * inv_lc
        imp = r[V_ROWS:] * inv_lc
        impt_ref[...] = imp[:, 0:tq] + imp[:, tq:2 * tq] + imp[:, 2 * tq:3 * tq] + imp[:, 3 * tq:4 * tq]

    pl.when(fast)(lambda: compressed(False))
    pl.when(jnp.logical_not(fast))(lambda: compressed(True))
    imp_t = impt_ref[...]

    cur = (t0 + lax.broadcasted_iota(jnp.int32, (n_sel, tq), 1)) // SEL_BLOCK
    forced = (blk == 0) | (blk == cur) | (blk == cur - 1)
    score = jnp.where(blk <= cur, imp_t + jnp.where(forced, FORCE_BONUS, 0.0), NEG)
    sub = 8
    chunks = [score[c * sub:(c + 1) * sub] for c in range(n_sel // sub)]
    blk_sub = lax.broadcasted_iota(jnp.int32, (sub, tq), 0)
    last_blk = (t0 + tq - 1) // SEL_BLOCK
    rank_ref[...] = jnp.zeros_like(rank_ref)
    for jb in range(n_sel // sub):
        @pl.when(jb * sub <= last_blk)
        def _():
            ranks = [jnp.zeros((sub, tq), F32) for _ in chunks]
            for j in range(jb * sub, (jb + 1) * sub):
                row = jnp.broadcast_to(chunks[jb][j - jb * sub:j - jb * sub + 1, :], (sub, tq))
                for c, chunk in enumerate(chunks):
                    if c > jb:
                        one = jnp.where(row >= chunk, 1.0, 0.0)
                    elif c < jb:
                        one = jnp.where(row > chunk, 1.0, 0.0)
                    else:
                        one = jnp.where(blk_sub > j - c * sub,
                                        jnp.where(row >= chunk, 1.0, 0.0), jnp.where(row > chunk, 1.0, 0.0))
                    ranks[c] = ranks[c] + one
            rank_ref[...] += jnp.concatenate(ranks, axis=0)
    bias_t = jnp.where(rank_ref[...] < float(SEL_TOPN), 0.0, SEL_MASK_BIAS)
    q_aug = stack_heads(jnp.where(blk == 0, 1.0, bias_t))

    tiles_per_step = SEL_KEYS // tq
    n_full = t0 // SEL_KEYS
    n_wt = WINDOW // tq + 1
    diag_keep = kmq_ref[0:tq, :] <= 0

    def sel_scores(step, tail):
        base = pl.multiple_of(step * SEL_KEYS, SEL_KEYS)
        s = _dot(ks_ref[0, 0, pl.ds(base, SEL_KEYS), :], q_aug)
        if tail:
            s = jnp.where(kmq_ref[...] <= t0 - base, s, NEG)
        vt = vst_ref[0, 0, pl.ds(step * tiles_per_step, tiles_per_step)]
        return s, jnp.concatenate([vt[i] for i in range(tiles_per_step)], axis=1)

    def win_scores():
        sw = _dot(kw_ref[0, 0, pl.ds(pl.multiple_of(t0, tq), n_wt * tq), :], q_aug)
        parts = [jnp.where(diag_keep, NEG, sw[:tq])]
        if WINDOW > tq:
            parts.append(sw[tq:WINDOW])
        parts.append(jnp.where(diag_keep, sw[WINDOW:], NEG))
        sw = jnp.concatenate(parts, axis=0)
        vwt = vwt_ref[0, 0, pl.ds(qi, n_wt)]
        return sw, jnp.concatenate([vwt[i] for i in range(n_wt)], axis=1)

    @pl.when(fast)
    def _():
        acc_ref[...] = jnp.zeros_like(acc_ref)

        def values(step):
            vt = vst_ref[0, 0, pl.ds(step * tiles_per_step, tiles_per_step)]
            return jnp.concatenate([vt[i] for i in range(tiles_per_step)], axis=1)

        def probs(step, tail):
            return jnp.exp2(sel_scores(step, tail)[0]).astype(BF16)

        odd = n_full % 2
        half = SEL_KEYS // 2
        assert half == tq

        @pl.when(t0 % SEL_KEYS != 0)
        def _():
            p1_ref[...] = probs(n_full, True)

        @pl.when(t0 % SEL_KEYS == 0)
        def _():
            s = _dot(ks_ref[0, 0, pl.ds(pl.multiple_of(t0, SEL_KEYS), half), :], q_aug)
            p1_ref[0:half, :] = jnp.exp2(jnp.where(diag_keep, s, NEG)).astype(BF16)
            p1_ref[half:, :] = jnp.zeros((half, cols), BF16)

        @pl.when(odd == 1)
        def _():
            p = probs(n_full - 1, False)
            acc_ref[...] += _dot(values(n_full), p1_ref[...])
            p1_ref[...] = p

        first_pending = n_full - odd

        def pair(j, carry):
            pending = jnp.where(j == 0, first_pending, 2 * j - 1)
            pa = probs(2 * j, False)
            acc_ref[...] += _dot(values(pending), p1_ref[...])
            p0_ref[...] = pa
            pb = probs(2 * j + 1, False)
            acc_ref[...] += _dot(values(2 * j), p0_ref[...])
            p1_ref[...] = pb
            return carry

        n_pairs = n_full // 2
        lax.fori_loop(0, n_pairs, pair, 0)
        pending = jnp.where(n_pairs == 0, first_pending, 2 * n_pairs - 1)
        sw, vwt = win_scores()
        pw = jnp.exp2(sw).astype(BF16)
        acc_ref[...] += _dot(values(pending), p1_ref[...])
        accw_ref[...] = _dot(vwt, pw)

    @pl.when(jnp.logical_not(fast))
    def _():
        acc_ref[...] = jnp.zeros_like(acc_ref)

        def online(s, vt, m):
            m_new = jnp.maximum(m, jnp.max(s, axis=0, keepdims=True))
            acc_ref[...] = jnp.exp2(m - m_new) * acc_ref[...] + _dot(vt, jnp.exp2(s - m_new).astype(BF16))
            return m_new

        m = jnp.full((1, cols), NEG, F32)
        m = lax.fori_loop(0, n_full, lambda i, m: online(*sel_scores(i, False), m), m)
        online(*sel_scores(n_full, True), m)
        sw, vwt = win_scores()
        accw_ref[...] = _dot(vwt, jnp.exp2(sw - jnp.max(sw, axis=0, keepdims=True)).astype(BF16))

    a_s = acc_ref[...]
    a_w = accw_ref[...]

    a_s = a_s[:dk] * (1.0 / a_s[dk:dk + 1])
    a_w = a_w[:dk] * (1.0 / a_w[dk:dk + 1])
    gates = jax.nn.sigmoid(glt_ref[0, 0, qi])
    outs = []
    for h in range(NSA_HPG):
        sl = slice(h * tq, (h + 1) * tq)
        outs.append(gates[3 * h:3 * h + 1] * ac_ref[:, sl] + gates[3 * h + 1:3 * h + 2] * a_s[:, sl]
                    + gates[3 * h + 2:3 * h + 3] * a_w[:, sl])
    out = jnp.concatenate(outs, axis=0).T
    o_ref[rows, :] = (out * _silu(z_ref[rows, :].astype(F32))).astype(o_ref.dtype)


def _nsa_attn_kernel(*refs):
    n_tiles = refs[1].shape[0] // Q_TILE

    def tile(i, carry):
        _nsa_attn_tile(i, *refs)
        return carry

    lax.fori_loop(0, n_tiles, tile, 0)


def _nsa_attention(fast, q, z, glt, ks_aug, vst, kw_pad, vwt, kc_pad, vct, overlap_t, batch, seq):
    cols = NSA_HPG * Q_TILE
    query = np.arange(cols)[None, :] % Q_TILE
    cmp_thr = jnp.asarray(np.arange(kc_pad.shape[2])[:, None] * CMP_STRIDE + (CMP_BLOCK - 1) - query, jnp.int32)
    key_minus_query = jnp.asarray(np.arange(SEL_KEYS)[:, None] - query, jnp.int32)
    n = batch * seq
    gw = NSA_HPG * NSA_DK

    def per_group(arr):
        nd = arr.ndim - 2
        return pl.BlockSpec((1, 1) + arr.shape[2:], lambda b, g: (b, g) + (0,) * nd)

    return pl.pallas_call(
        _nsa_attn_kernel,
        grid=(batch, NSA_G),
        in_specs=[
            pl.BlockSpec(memory_space=pltpu.SMEM),
            pl.BlockSpec((seq, gw), lambda b, g: (b, g)),
            per_group(glt),
            pl.BlockSpec((seq, gw), lambda b, g: (b, g)),
            per_group(ks_aug), per_group(vst), per_group(kw_pad), per_group(vwt),
            per_group(kc_pad), per_group(vct),
            _const_spec(overlap_t.shape), _const_spec(cmp_thr.shape), _const_spec(key_minus_query.shape),
        ],
        out_specs=pl.BlockSpec((seq, gw), lambda b, g: (b, g)),
        out_shape=jax.ShapeDtypeStruct((n, NSA_HEADS * NSA_DK), BF16),
        scratch_shapes=([pltpu.VMEM((V_ROWS, cols), F32)] * 2 + [pltpu.VMEM((SEL_BLOCK, Q_TILE), F32)]
                        + [pltpu.VMEM((SEL_KEYS, cols), BF16)] * 2
                        + [pltpu.VMEM((NSA_DK, cols), F32), pltpu.VMEM((SEL_BLOCK, Q_TILE), F32)]),
        compiler_params=pltpu.CompilerParams(
            dimension_semantics=("arbitrary", "arbitrary"),
            vmem_limit_bytes=VMEM_LIMIT),
        name="nsa_attention",
    )(fast, q, glt, z, ks_aug, vst, kw_pad, vwt, kc_pad, vct, overlap_t, cmp_thr, key_minus_query)


def _ret_kernel(q_ref, k_ref, v_ref, z_ref, dec_ref, qd_ref, kd_ref, cd_ref, o_ref, st_ref):
    @pl.when(pl.program_id(1) == 0)
    def _():
        st_ref[...] = jnp.zeros_like(st_ref)

    for h in range(RET_HEADS):
        q = q_ref[:, h * RET_DK:(h + 1) * RET_DK]
        k = k_ref[:, h * RET_DK:(h + 1) * RET_DK]
        v = v_ref[:, h * RET_DV:(h + 1) * RET_DV]
        att = _dot_nt(q, k) * dec_ref[h]
        state = st_ref[h]
        o = _dot(att.astype(BF16), v) + _dot(q, state.astype(BF16)) * qd_ref[h]
        kd_t = (k.astype(F32) * kd_ref[h]).T.astype(BF16)
        st_ref[h] = state * cd_ref[h] + _dot(kd_t, v)
        mu = jnp.mean(o, axis=-1, keepdims=True)
        d = o - mu
        var = jnp.mean(d * d, axis=-1, keepdims=True)
        z = z_ref[:, h * RET_DV:(h + 1) * RET_DV].astype(F32)
        o_ref[:, h * RET_DV:(h + 1) * RET_DV] = (d * lax.rsqrt(var + GN_EPS) * _silu(z)).astype(o_ref.dtype)


def _retention(pr, tables, batch, seq):
    n = batch * seq
    n_c = seq // RET_CHUNK
    dec, qd, kd, cd = tables
    qk_w = RET_HEADS * RET_DK
    v_w = RET_HEADS * RET_DV
    row = lambda b, c: b * n_c + c
    return pl.pallas_call(
        _ret_kernel,
        grid=(batch, n_c),
        in_specs=[
            pl.BlockSpec((RET_CHUNK, qk_w), lambda b, c: (row(b, c), 0)),
            pl.BlockSpec((RET_CHUNK, qk_w), lambda b, c: (row(b, c), 1)),
            pl.BlockSpec((RET_CHUNK, v_w), lambda b, c: (row(b, c), 1)),
            pl.BlockSpec((RET_CHUNK, v_w), lambda b, c: (row(b, c), 2)),
            _const_spec(dec.shape), _const_spec(qd.shape), _const_spec(kd.shape), _const_spec(cd.shape),
        ],
        out_specs=pl.BlockSpec((RET_CHUNK, v_w), lambda b, c: (row(b, c), 0)),
        out_shape=jax.ShapeDtypeStruct((n, v_w), BF16),
        scratch_shapes=[pltpu.VMEM((RET_HEADS, RET_DK, RET_DV), F32)],
        compiler_params=pltpu.CompilerParams(
            dimension_semantics=("arbitrary", "arbitrary"), vmem_limit_bytes=VMEM_LIMIT),
        name="retention",
    )(pr, pr, pr, pr, dec, qd, kd, cd)


def _overlap_matrix_t(n_cmp_pad):
    i = np.arange(n_cmp_pad)[None, :]
    j = np.arange(SEL_BLOCK)[:, None]
    ov = (i * CMP_STRIDE < (j + 1) * SEL_BLOCK) & (i * CMP_STRIDE + CMP_BLOCK > j * SEL_BLOCK)
    return jnp.asarray(ov, BF16)


def _rotary_tables(seq):
    half = RET_DK // 2
    inv = ROPE_BASE ** (-jnp.linspace(0.0, 1.0, half, dtype=F32))
    ang = jnp.arange(seq, dtype=F32)[:, None] * inv[None, :]
    return jnp.cos(ang), jnp.sin(ang)


def _retention_tables():
    c = RET_CHUNK
    log_g = jnp.log(1.0 - 2.0 ** (-5.0 - jnp.arange(RET_HEADS, dtype=F32)))
    ix = jnp.arange(c, dtype=F32)
    diff = ix[:, None] - ix[None, :]
    dec = jnp.where(diff >= 0, jnp.exp(log_g[:, None, None] * jnp.maximum(diff, 0.0)), 0.0)
    qd = jnp.exp(log_g[:, None] * (ix + 1.0))[:, :, None]
    kd = jnp.exp(log_g[:, None] * (c - 1.0 - ix))[:, :, None]
    cd = jnp.broadcast_to(jnp.exp(log_g * c)[:, None, None], (RET_HEADS, 1, RET_DV))
    return dec, qd, kd, cd


def _nsa_layer(x2d, p2d, batch, seq, norm_g, w_in, q_g, kc_g, ks_g, kw_g, pos_k, pos_v,
               ck_w1, ck_w2, cv_w1, cv_w2, w_out, ple_w, ple_gate_w):
    qw = NSA_HEADS * NSA_DK
    kvw = NSA_G * NSA_DK
    n_gate = 3 * NSA_HPG
    sizes = [qw] + [kvw] * 6 + [3 * NSA_HEADS, qw]
    splits = np.concatenate([[0], np.cumsum(sizes)])
    names = ["q", "kc", "vc", "ks", "vs", "kw", "vw"]
    offs = {nm: int(splits[i]) for i, nm in enumerate(names)}
    front = int(splits[7])
    offs["z"], offs["gl"] = front, front + qw
    wgl = w_in[:, splits[7]:splits[8]].reshape(D_MODEL, NSA_G, n_gate)
    wgl = jnp.pad(wgl, ((0, 0), (0, 0), (0, GATE_ROWS - n_gate))).reshape(D_MODEL, NSA_G * GATE_ROWS)
    wgl = jnp.pad(wgl, ((0, 0), (0, LANES - NSA_G * GATE_ROWS)))
    w = jnp.concatenate([w_in[:, :front], w_in[:, splits[8]:], wgl], axis=1).astype(BF16)
    zeros = lambda width: jnp.zeros((width,), F32)
    head_gain = jnp.concatenate([
        jnp.tile(q_g, NSA_HEADS) * (NSA_DK ** -0.5 * LOG2E), zeros(2 * kvw),
        jnp.tile(ks_g, NSA_G), zeros(kvw), jnp.tile(kw_g, NSA_G), zeros(w.shape[1] - offs["vw"])]).reshape(1, -1)

    q_norm = LOG2E * jnp.max(jnp.abs(q_g)) * 1.02
    bound_sel = (q_norm * NSA_DK ** 0.5 * jnp.max(jnp.abs(ks_g))).astype(BF16).astype(F32) * 1.01
    bound_win = (q_norm * NSA_DK ** 0.5 * jnp.max(jnp.abs(kw_g))).astype(BF16).astype(F32) * 1.01
    bound_cmp = (q_norm * NSA_DK ** 0.5 * jnp.max(jnp.abs(kc_g))).astype(BF16).astype(F32) * 1.01
    bound_max = jnp.maximum(jnp.maximum(bound_sel, bound_win), bound_cmp)
    fast = (bound_max <= MAX_SAFE_BOUND).astype(jnp.int32).reshape(1)

    blk_id = np.arange(seq)[:, None] // SEL_BLOCK
    upper = np.arange(LANES)[None, :] - NSA_DK
    onehot = jnp.asarray((blk_id == upper) & (upper > 0), F32)
    lane64 = jnp.asarray(upper == 0)
    aug_sel = jnp.where(lane64, -bound_sel, onehot).astype(BF16)
    aug_win = jnp.where(lane64, -bound_win, jnp.zeros((seq, LANES), F32)).astype(BF16)
    pad_rows = jnp.where(lane64, SEL_MASK_BIAS, jnp.zeros((WINDOW, LANES), F32)).astype(BF16)

    q, z, kc_sub, vc_sub, ks_aug, kw_pad, vst, vwt, glt = _nsa_proj(
        x2d, norm_g, w, head_gain, aug_sel, aug_win, pad_rows, offs, batch, seq)

    n_sub = seq // CMP_STRIDE
    half = CMP_STRIDE * NSA_DK
    w1cat = lambda w1: jnp.concatenate([w1[:half], w1[half:]], axis=1).astype(BF16)
    pos_flat = lambda pos: jnp.pad(pos.reshape(1, CMP_BLOCK * NSA_DK), ((0, 7), (0, 0))).astype(BF16)
    w2pad = lambda w2: jnp.pad(w2, ((0, 0), (0, LANES - NSA_DK))).astype(BF16)
    gk = jnp.pad(kc_g, (0, LANES - NSA_DK)).reshape(1, LANES)
    k_shift = jnp.where(lane64, -bound_cmp, 0.0)
    kc_pad, vct = _compress(kc_sub, vc_sub, w1cat(ck_w1), w1cat(cv_w1),
                            pos_flat(pos_k), pos_flat(pos_v), w2pad(ck_w2), w2pad(cv_w2), gk, k_shift)

    a = _nsa_attention(fast, q, z, glt, ks_aug, vst, kw_pad, vwt, kc_pad, vct,
                       _overlap_matrix_t(n_sub), batch, seq)
    return _out_proj(a, w_out, x2d, p2d, ple_gate_w, ple_w)


def _ret_layer(x2d, p2d, batch, seq, norm_g, w_in, w_out, ple_w, ple_gate_w):
    cos, sin = _rotary_tables(seq)
    pr = _ret_proj(x2d, norm_g, w_in.astype(BF16), cos, sin, seq)
    a = _retention(pr, _retention_tables(), batch, seq)
    return _out_proj(a, w_out, x2d, p2d, ple_gate_w, ple_w)


def kernel(x, p, norm_g, nsa_w_in, nsa_q_g, nsa_kc_g, nsa_ks_g, nsa_kw_g, nsa_cmp_pos_k, nsa_cmp_pos_v, nsa_cmp_k_w1, nsa_cmp_k_w2, nsa_cmp_v_w1, nsa_cmp_v_w2, nsa_w_out, ret_w_in, ret_w_out, ple_w, ple_gate_w):
    batch, seq, d_model = x.shape
    depth = p.shape[0]
    n = batch * seq
    x2d = x.reshape(n, d_model)
    for i in range(depth):
        p2d = p[i].reshape(n, PLE_DIM)
        j = i // 2
        if i % 2 == 0:
            x2d = _nsa_layer(x2d, p2d, batch, seq, norm_g[i], nsa_w_in[j], nsa_q_g[j], nsa_kc_g[j],
                             nsa_ks_g[j], nsa_kw_g[j], nsa_cmp_pos_k[j], nsa_cmp_pos_v[j],
                             nsa_cmp_k_w1[j], nsa_cmp_k_w2[j], nsa_cmp_v_w1[j], nsa_cmp_v_w2[j],
                             nsa_w_out[j], ple_w[i], ple_gate_w[i])
        else:
            x2d = _ret_layer(x2d, p2d, batch, seq, norm_g[i], ret_w_in[j], ret_w_out[j],
                             ple_w[i], ple_gate_w[i])
    return x2d.reshape(batch, seq, d_model)
```

```python
import functools

import numpy as np
import jax
import jax.numpy as jnp
from jax import lax
from jax.experimental import pallas as pl
from jax.experimental.pallas import tpu as pltpu

F32 = jnp.float32
BF16 = jnp.bfloat16

D_MODEL = 1024
PLE_DIM = 256
RMS_EPS = 1e-6
GN_EPS = 1e-5

NSA_HEADS = 16
NSA_DK = 64
NSA_G = 4
NSA_HPG = 4
CMP_BLOCK = 32
CMP_STRIDE = 16
CMP_HIDDEN = 256
SEL_BLOCK = 64
SEL_TOPN = 16
WINDOW = 512
FORCE_BONUS = 1e4
NEG = -1e30
SEL_MASK_BIAS = -30000.0

RET_HEADS = 4
RET_DK = 256
RET_DV = 512
RET_CHUNK = 256
ROPE_BASE = 10000.0

LANES = 128
VMEM_LIMIT = 56 * 1024 * 1024

ROW_TILE = 512
GROUP_W = NSA_HPG * NSA_DK
GATE_ROWS = 16
Q_TILE = 256
Q_TILES_PER_STEP = 4
SEL_KEYS = 512
LOG2E = 1.4426950408889634
MAX_SAFE_BOUND = 50.0
V_ROWS = NSA_DK + 16


def _dot(a, b):
    return jnp.dot(a, b, preferred_element_type=F32)


def _dot_nt(a, b):
    return lax.dot_general(a, b, (((1,), (1,)), ((), ())), preferred_element_type=F32)


def _silu(x):
    return x * jax.nn.sigmoid(x)


def _const_spec(shape):
    nd = len(shape)
    return pl.BlockSpec(shape, lambda *_: (0,) * nd)


def _ret_proj_kernel(x_ref, g_ref, w_ref, cos_ref, sin_ref, o_ref):
    x = x_ref[...]
    h = (x * lax.rsqrt(jnp.mean(x * x, axis=-1, keepdims=True) + RMS_EPS) * g_ref[...]).astype(BF16)
    cos, sin = cos_ref[...], sin_ref[...]
    half = RET_DK // 2
    qk_w = RET_HEADS * RET_DK
    for t in range(2 * RET_HEADS):
        c0 = t * RET_DK
        acc = _dot(h, w_ref[:, c0:c0 + RET_DK])
        a1, a2 = acc[:, :half], acc[:, half:]
        rot = jnp.concatenate([a1 * cos - a2 * sin, a1 * sin + a2 * cos], axis=-1)
        if c0 >= qk_w:
            rot = rot * RET_DK ** -0.5
        o_ref[:, c0:c0 + RET_DK] = rot.astype(o_ref.dtype)
    for c0 in range(2 * qk_w, w_ref.shape[1], RET_DV):
        o_ref[:, c0:c0 + RET_DV] = _dot(h, w_ref[:, c0:c0 + RET_DV]).astype(o_ref.dtype)


def _ret_proj(x2d, norm_g, w, cos, sin, seq):
    n = x2d.shape[0]
    cols = w.shape[1]
    n_pos_blocks = seq // ROW_TILE
    return pl.pallas_call(
        _ret_proj_kernel,
        grid=(n // ROW_TILE,),
        in_specs=[
            pl.BlockSpec((ROW_TILE, D_MODEL), lambda i: (i, 0)),
            _const_spec((1, D_MODEL)),
            _const_spec((D_MODEL, cols)),
            pl.BlockSpec((ROW_TILE, LANES), lambda i: (i % n_pos_blocks, 0)),
            pl.BlockSpec((ROW_TILE, LANES), lambda i: (i % n_pos_blocks, 0)),
        ],
        out_specs=pl.BlockSpec((ROW_TILE, cols), lambda i: (i, 0)),
        out_shape=jax.ShapeDtypeStruct((n, cols), BF16),
        compiler_params=pltpu.CompilerParams(
            dimension_semantics=("arbitrary",), vmem_limit_bytes=VMEM_LIMIT),
        name="ret_norm_proj",
    )(x2d, norm_g.reshape(1, D_MODEL), w, cos, sin)


def _nsa_proj_kernel(x_ref, g_ref, w_ref, hg_ref, bd_ref, augs_ref, augw_ref, pad_ref,
                     q_ref, z_ref, kcs_ref, vcs_ref, ks_ref, kw_ref, vst_ref, vwt_ref, glt_ref, cmp_scr, *, offs):
    s = pl.program_id(1)
    tiles = ROW_TILE // Q_TILE

    @pl.when(s == 0)
    def _():
        for g in range(NSA_G):
            kw_ref[0, g] = pad_ref[...]
        vwt_ref[...] = jnp.zeros_like(vwt_ref)

    @pl.when(s > 0)
    def _():
        x = x_ref[...]
        h = (x * lax.rsqrt(jnp.mean(x * x, axis=-1, keepdims=True) + RMS_EPS) * g_ref[...]).astype(BF16)
        low = lax.broadcasted_iota(jnp.int32, (ROW_TILE, LANES), 1) < NSA_DK

        def proj(c0, width=GROUP_W):
            return _dot(h, w_ref[:, c0:c0 + width])

        def head_norm(acc, c0):
            ss = _dot((acc * acc).astype(BF16), bd_ref[...])
            return acc * lax.rsqrt(ss * (1.0 / NSA_DK) + RMS_EPS) * hg_ref[:, c0:c0 + GROUP_W]

        for t in range(NSA_G):
            lanes = slice(GROUP_W * t, GROUP_W * (t + 1))
            c0 = offs["q"] + GROUP_W * t
            q_ref[:, lanes] = head_norm(proj(c0), c0).astype(BF16)
            z_ref[:, lanes] = proj(offs["z"] + GROUP_W * t).astype(BF16)

        kvw = NSA_G * NSA_DK
        ones = jnp.ones((V_ROWS - NSA_DK, Q_TILE), BF16)
        for name, aug_ref, k_ref, v_ref in (("ks", augs_ref, ks_ref, vst_ref), ("kw", augw_ref, kw_ref, vwt_ref)):
            kv = proj(offs[name], 2 * kvw)
            k = head_norm(kv[:, :kvw], offs[name])
            aug = aug_ref[...].astype(F32)
            for pair in range(NSA_G // 2):
                two = k[:, LANES * pair:LANES * (pair + 1)]
                k_ref[0, 2 * pair] = jnp.where(low, two, aug).astype(BF16)
                k_ref[0, 2 * pair + 1] = jnp.where(low, pltpu.roll(two, NSA_DK, 1), aug).astype(BF16)
            vt = kv[:, kvw:].T
            for g in range(NSA_G):
                for j in range(tiles):
                    v_ref[0, g, j, 0:NSA_DK, :] = vt[g * NSA_DK:(g + 1) * NSA_DK,
                                                     j * Q_TILE:(j + 1) * Q_TILE].astype(BF16)
                    v_ref[0, g, j, NSA_DK:V_ROWS, :] = ones

        cmp_in = proj(offs["kc"], 2 * kvw)
        for t in range(2 * kvw // LANES):
            cmp_scr[t] = cmp_in[:, LANES * t:LANES * (t + 1)]
        n_sub = ROW_TILE // CMP_STRIDE
        low_sub = lax.broadcasted_iota(jnp.int32, (n_sub, LANES), 1) < NSA_DK
        for r in range(0, CMP_STRIDE, 2):
            for i, o_ref in enumerate((kcs_ref, vcs_ref)):
                for pair in range(NSA_G // 2):
                    t = i * (NSA_G // 2) + pair
                    e2 = cmp_scr[t, pl.ds(r, n_sub, stride=CMP_STRIDE), :]
                    o2 = cmp_scr[t, pl.ds(r + 1, n_sub, stride=CMP_STRIDE), :]
                    lanes = slice(LANES * (r // 2), LANES * (r // 2 + 1))
                    o_ref[0, 2 * pair, :, lanes] = jnp.where(low_sub, e2, pltpu.roll(o2, NSA_DK, 1)).astype(BF16)
                    o_ref[0, 2 * pair + 1, :, lanes] = jnp.where(low_sub, pltpu.roll(e2, NSA_DK, 1), o2).astype(BF16)

        glt = proj(offs["gl"], LANES).T
        for g in range(NSA_G):
            glt_ref[0, g] = glt[GATE_ROWS * g:GATE_ROWS * (g + 1)]


def _nsa_proj(x2d, norm_g, w, head_gain, aug_sel, aug_win, pad_rows, offs, batch, seq):
    n = batch * seq
    nsb = seq // ROW_TILE
    tiles = ROW_TILE // Q_TILE
    qw = NSA_HEADS * NSA_DK
    kvw = NSA_G * NSA_DK
    sub_w = CMP_STRIDE * NSA_DK
    lane_head = np.arange(GROUP_W) // NSA_DK
    bd = jnp.asarray(lane_head[:, None] == lane_head[None, :], BF16)
    prev = lambda s: jnp.maximum(s - 1, 0)
    rows = lambda b, s: (b * nsb + prev(s), 0)
    out_shape = [
        jax.ShapeDtypeStruct((n, qw), BF16), jax.ShapeDtypeStruct((n, qw), BF16),
        jax.ShapeDtypeStruct((batch, NSA_G, seq // CMP_STRIDE, sub_w), BF16),
        jax.ShapeDtypeStruct((batch, NSA_G, seq // CMP_STRIDE, sub_w), BF16),
        jax.ShapeDtypeStruct((batch, NSA_G, seq, LANES), BF16),
        jax.ShapeDtypeStruct((batch, NSA_G, seq + WINDOW, LANES), BF16),
        jax.ShapeDtypeStruct((batch, NSA_G, seq // Q_TILE, V_ROWS, Q_TILE), BF16),
        jax.ShapeDtypeStruct((batch, NSA_G, (seq + WINDOW) // Q_TILE, V_ROWS, Q_TILE), BF16),
        jax.ShapeDtypeStruct((batch, NSA_G, GATE_ROWS, seq), F32),
    ]
    out_specs = [
        pl.BlockSpec((ROW_TILE, qw), rows), pl.BlockSpec((ROW_TILE, qw), rows),
        pl.BlockSpec((1, NSA_G, ROW_TILE // CMP_STRIDE, sub_w), lambda b, s: (b, 0, prev(s), 0)),
        pl.BlockSpec((1, NSA_G, ROW_TILE // CMP_STRIDE, sub_w), lambda b, s: (b, 0, prev(s), 0)),
        pl.BlockSpec((1, NSA_G, ROW_TILE, LANES), lambda b, s: (b, 0, prev(s), 0)),
        pl.BlockSpec((1, NSA_G, ROW_TILE, LANES), lambda b, s: (b, 0, s, 0)),
        pl.BlockSpec((1, NSA_G, tiles, V_ROWS, Q_TILE), lambda b, s: (b, 0, prev(s), 0, 0)),
        pl.BlockSpec((1, NSA_G, tiles, V_ROWS, Q_TILE), lambda b, s: (b, 0, s, 0, 0)),
        pl.BlockSpec((1, NSA_G, GATE_ROWS, ROW_TILE), lambda b, s: (b, 0, 0, prev(s))),
    ]
    assert WINDOW == ROW_TILE
    return pl.pallas_call(
        functools.partial(_nsa_proj_kernel, offs=offs),
        grid=(batch, nsb + 1),
        in_specs=[
            pl.BlockSpec((ROW_TILE, D_MODEL), rows),
            _const_spec((1, D_MODEL)),
            _const_spec(w.shape),
            _const_spec(head_gain.shape),
            _const_spec(bd.shape),
            pl.BlockSpec((ROW_TILE, LANES), lambda b, s: (prev(s), 0)),
            pl.BlockSpec((ROW_TILE, LANES), lambda b, s: (prev(s), 0)),
            _const_spec(pad_rows.shape),
        ],
        out_specs=out_specs,
        out_shape=out_shape,
        scratch_shapes=[pltpu.VMEM((2 * kvw // LANES, ROW_TILE, LANES), F32)],
        compiler_params=pltpu.CompilerParams(
            dimension_semantics=("arbitrary", "arbitrary"), vmem_limit_bytes=VMEM_LIMIT),
        name="nsa_norm_proj",
    )(x2d, norm_g.reshape(1, D_MODEL), w, head_gain, bd, aug_sel, aug_win, pad_rows)


def _out_kernel(a_ref, wo_ref, x_ref, p_ref, wg_ref, wp_ref, o_ref):
    x1 = x_ref[...] + _dot(a_ref[...], wo_ref[...])
    gate = jax.nn.sigmoid(_dot(x1.astype(BF16), wg_ref[...]))
    emb = _dot(p_ref[...].astype(BF16), wp_ref[...])
    o_ref[...] = x1 + gate * emb


def _out_proj(a, w_out, x2d, p2d, w_gate, w_ple):
    n, k = a.shape
    return pl.pallas_call(
        _out_kernel,
        grid=(n // ROW_TILE,),
        in_specs=[
            pl.BlockSpec((ROW_TILE, k), lambda i: (i, 0)),
            _const_spec((k, D_MODEL)),
            pl.BlockSpec((ROW_TILE, D_MODEL), lambda i: (i, 0)),
            pl.BlockSpec((ROW_TILE, PLE_DIM), lambda i: (i, 0)),
            _const_spec((D_MODEL, D_MODEL)),
            _const_spec((PLE_DIM, D_MODEL)),
        ],
        out_specs=pl.BlockSpec((ROW_TILE, D_MODEL), lambda i: (i, 0)),
        out_shape=jax.ShapeDtypeStruct((n, D_MODEL), F32),
        compiler_params=pltpu.CompilerParams(
            dimension_semantics=("arbitrary",), vmem_limit_bytes=VMEM_LIMIT),
        name="out_proj_ple",
    )(a, w_out.astype(BF16), x2d, p2d, w_gate.astype(BF16), w_ple.astype(BF16))


def _cmp_kernel(xk_ref, xv_ref, w1k_ref, w1v_ref, posk_ref, posv_ref, w2k_ref, w2v_ref,
                gk_ref, shift_ref, kc_ref, vc_ref):
    n_sub = xk_ref.shape[2]
    half = xk_ref.shape[3]
    for is_k, x_ref, w1_ref, pos_ref, w2_ref, o_ref in (
            (True, xk_ref, w1k_ref, posk_ref, w2k_ref, kc_ref),
            (False, xv_ref, w1v_ref, posv_ref, w2v_ref, vc_ref)):
        w1 = w1_ref[...]
        x = x_ref[0].reshape(NSA_G * n_sub, half)
        ab = _dot(x, w1)
        pos = pos_ref[...]
        pos_term = (_dot(pos[:, :half], w1[:, :CMP_HIDDEN])
                    + _dot(pos[:, half:], w1[:, CMP_HIDDEN:]))[0:1]
        for g in range(NSA_G):
            first = ab[g * n_sub:(g + 1) * n_sub, :CMP_HIDDEN]
            second = ab[g * n_sub:(g + 1) * n_sub, CMP_HIDDEN:]
            hid = _silu(first + pltpu.roll(second, n_sub - 1, 0) + pos_term)
            c = _dot(hid.astype(BF16), w2_ref[...])
            if is_k:
                ss = jnp.sum(c * c, axis=-1, keepdims=True) * (1.0 / NSA_DK)
                c = c * lax.rsqrt(ss + RMS_EPS) * gk_ref[...] + shift_ref[...]
                o_ref[0, g] = c.astype(BF16)
            else:
                ct = jnp.concatenate([c.T[:NSA_DK], jnp.ones((V_ROWS - NSA_DK, n_sub), F32)], axis=0)
                o_ref[0, g] = ct.astype(BF16)


def _compress(xk, xv, w1k, w1v, posk, posv, w2k, w2v, gk, k_shift):
    b, g, n_sub, half = xk.shape
    blk = pl.BlockSpec((1, g, n_sub, half), lambda i: (i, 0, 0, 0))
    k_blk = pl.BlockSpec((1, g, n_sub, LANES), lambda i: (i, 0, 0, 0))
    k_sds = jax.ShapeDtypeStruct((b, g, n_sub, LANES), BF16)
    v_blk = pl.BlockSpec((1, g, V_ROWS, n_sub), lambda i: (i, 0, 0, 0))
    v_sds = jax.ShapeDtypeStruct((b, g, V_ROWS, n_sub), BF16)
    return pl.pallas_call(
        _cmp_kernel,
        grid=(b,),
        in_specs=[blk, blk,
                  _const_spec(w1k.shape), _const_spec(w1v.shape),
                  _const_spec(posk.shape), _const_spec(posv.shape),
                  _const_spec(w2k.shape), _const_spec(w2v.shape),
                  _const_spec(gk.shape), _const_spec(k_shift.shape)],
        out_specs=[k_blk, v_blk],
        out_shape=[k_sds, v_sds],
        compiler_params=pltpu.CompilerParams(
            dimension_semantics=("arbitrary",), vmem_limit_bytes=VMEM_LIMIT),
        name="nsa_compress",
    )(xk, xv, w1k, w1v, posk, posv, w2k, w2v, gk, k_shift)


def _nsa_attn_tile(sub, fast_ref, q_ref, glt_ref, z_ref, ks_ref, vst_ref, kw_ref, vwt_ref, kc_ref, vct_ref,
                   ovt_ref, cthr_ref, kmq_ref, o_ref, acc_ref, accw_ref, rank_ref, p0_ref, p1_ref, ac_ref, impt_ref):
    tq = Q_TILE
    cols = NSA_HPG * tq
    qi = pl.program_id(2) * Q_TILES_PER_STEP + sub
    t0 = qi * tq
    rows = slice(sub * tq, (sub + 1) * tq)
    n_sel = SEL_BLOCK
    dk = NSA_DK

    qt = q_ref[rows, :].astype(F32).T
    q_heads = [qt[h * dk:(h + 1) * dk] for h in range(NSA_HPG)]

    def stack_heads(extra_rows):
        return jnp.concatenate(
            [jnp.concatenate([qh, extra_rows], axis=0) for qh in q_heads], axis=1).astype(BF16)

    blk = lax.broadcasted_iota(jnp.int32, (n_sel, tq), 0)
    ones_row = jnp.where(blk == 0, 1.0, 0.0)
    fast = fast_ref[0] > 0

    def compressed(subtract_max):
        sc = _dot(kc_ref[0, 0], stack_heads(ones_row))
        sc = jnp.where(cthr_ref[...] <= t0, sc, NEG)
        if subtract_max:
            sc = sc - jnp.maximum(jnp.max(sc, axis=0, keepdims=True), 0.1 * NEG)
        ec = jnp.exp2(sc).astype(BF16)
        r = _dot(jnp.concatenate([vct_ref[0, 0], ovt_ref[...]], axis=0), ec)
        lc = r[dk:dk + 1]
        inv_lc = 1.0 / jnp.where(lc > 0.0, lc, 1.0)
        ac_ref[...] = r[:dk] * inv_lc
        imp = r[V_ROWS:] * inv_lc
        impt_ref[...] = imp[:, 0:tq] + imp[:, tq:2 * tq] + imp[:, 2 * tq:3 * tq] + imp[:, 3 * tq:4 * tq]

    pl.when(fast)(lambda: compressed(False))
    pl.when(jnp.logical_not(fast))(lambda: compressed(True))
    imp_t = impt_ref[...]

    cur = (t0 + lax.broadcasted_iota(jnp.int32, (n_sel, tq), 1)) // SEL_BLOCK
    forced = (blk == 0) | (blk == cur) | (blk == cur - 1)
    score = jnp.where(blk <= cur, imp_t + jnp.where(forced, FORCE_BONUS, 0.0), NEG)
    sub = 8
    chunks = [score[c * sub:(c + 1) * sub] for c in range(n_sel // sub)]
    blk_sub = lax.broadcasted_iota(jnp.int32, (sub, tq), 0)
    last_blk = (t0 + tq - 1) // SEL_BLOCK
    rank_ref[...] = jnp.zeros_like(rank_ref)
    for jb in range(n_sel // sub):
        @pl.when(jb * sub <= last_blk)
        def _():
            ranks = [jnp.zeros((sub, tq), F32) for _ in chunks]
            for j in range(jb * sub, (jb + 1) * sub):
                row = jnp.broadcast_to(chunks[jb][j - jb * sub:j - jb * sub + 1, :], (sub, tq))
                for c, chunk in enumerate(chunks):
                    if c > jb:
                        one = jnp.where(row >= chunk, 1.0, 0.0)
                    elif c < jb:
                        one = jnp.where(row > chunk, 1.0, 0.0)
                    else:
                        one = jnp.where(blk_sub > j - c * sub,
                                        jnp.where(row >= chunk, 1.0, 0.0), jnp.where(row > chunk, 1.0, 0.0))
                    ranks[c] = ranks[c] + one
            rank_ref[...] += jnp.concatenate(ranks, axis=0)
    bias_t = jnp.where(rank_ref[...] < float(SEL_TOPN), 0.0, SEL_MASK_BIAS)
    q_aug = stack_heads(jnp.where(blk == 0, 1.0, bias_t))

    tiles_per_step = SEL_KEYS // tq
    n_full = t0 // SEL_KEYS
    n_wt = WINDOW // tq + 1
    diag_keep = kmq_ref[0:tq, :] <= 0

    def sel_scores(step, tail):
        base = pl.multiple_of(step * SEL_KEYS, SEL_KEYS)
        s = _dot(ks_ref[0, 0, pl.ds(base, SEL_KEYS), :], q_aug)
        if tail:
            s = jnp.where(kmq_ref[...] <= t0 - base, s, NEG)
        vt = vst_ref[0, 0, pl.ds(step * tiles_per_step, tiles_per_step)]
        return s, jnp.concatenate([vt[i] for i in range(tiles_per_step)], axis=1)

    def win_scores():
        sw = _dot(kw_ref[0, 0, pl.ds(pl.multiple_of(t0, tq), n_wt * tq), :], q_aug)
        parts = [jnp.where(diag_keep, NEG, sw[:tq])]
        if WINDOW > tq:
            parts.append(sw[tq:WINDOW])
        parts.append(jnp.where(diag_keep, sw[WINDOW:], NEG))
        sw = jnp.concatenate(parts, axis=0)
        vwt = vwt_ref[0, 0, pl.ds(qi, n_wt)]
        return sw, jnp.concatenate([vwt[i] for i in range(n_wt)], axis=1)

    @pl.when(fast)
    def _():
        acc_ref[...] = jnp.zeros_like(acc_ref)

        def values(step):
            vt = vst_ref[0, 0, pl.ds(step * tiles_per_step, tiles_per_step)]
            return jnp.concatenate([vt[i] for i in range(tiles_per_step)], axis=1)

        def probs(step, tail):
            return jnp.exp2(sel_scores(step, tail)[0]).astype(BF16)

        odd = n_full % 2
        half = SEL_KEYS // 2
        assert half == tq

        @pl.when(t0 % SEL_KEYS != 0)
        def _():
            p1_ref[...] = probs(n_full, True)

        @pl.when(t0 % SEL_KEYS == 0)
        def _():
            s = _dot(ks_ref[0, 0, pl.ds(pl.multiple_of(t0, SEL_KEYS), half), :], q_aug)
            p1_ref[0:half, :] = jnp.exp2(jnp.where(diag_keep, s, NEG)).astype(BF16)
            p1_ref[half:, :] = jnp.zeros((half, cols), BF16)

        @pl.when(odd == 1)
        def _():
            p = probs(n_full - 1, False)
            acc_ref[...] += _dot(values(n_full), p1_ref[...])
            p1_ref[...] = p

        first_pending = n_full - odd

        def pair(j, carry):
            pending = jnp.where(j == 0, first_pending, 2 * j - 1)
            pa = probs(2 * j, False)
            acc_ref[...] += _dot(values(pending), p1_ref[...])
            p0_ref[...] = pa
            pb = probs(2 * j + 1, False)
            acc_ref[...] += _dot(values(2 * j), p0_ref[...])
            p1_ref[...] = pb
            return carry

        n_pairs = n_full // 2
        lax.fori_loop(0, n_pairs, pair, 0)
        pending = jnp.where(n_pairs == 0, first_pending, 2 * n_pairs - 1)
        sw, vwt = win_scores()
        pw = jnp.exp2(sw).astype(BF16)
        acc_ref[...] += _dot(values(pending), p1_ref[...])
        accw_ref[...] = _dot(vwt, pw)

    @pl.when(jnp.logical_not(fast))
    def _():
        acc_ref[...] = jnp.zeros_like(acc_ref)

        def online(s, vt, m):
            m_new = jnp.maximum(m, jnp.max(s, axis=0, keepdims=True))
            acc_ref[...] = jnp.exp2(m - m_new) * acc_ref[...] + _dot(vt, jnp.exp2(s - m_new).astype(BF16))
            return m_new

        m = jnp.full((1, cols), NEG, F32)
        m = lax.fori_loop(0, n_full, lambda i, m: online(*sel_scores(i, False), m), m)
        online(*sel_scores(n_full, True), m)
        sw, vwt = win_scores()
        accw_ref[...] = _dot(vwt, jnp.exp2(sw - jnp.max(sw, axis=0, keepdims=True)).astype(BF16))

    a_s = acc_ref[...]
    a_w = accw_ref[...]

    a_s = a_s[:dk] * (1.0 / a_s[dk:dk + 1])
    a_w = a_w[:dk] * (1.0 / a_w[dk:dk + 1])
    gates = jax.nn.sigmoid(glt_ref[0, 0, :, rows])
    outs = []
    for h in range(NSA_HPG):
        sl = slice(h * tq, (h + 1) * tq)
        outs.append(gates[3 * h:3 * h + 1] * ac_ref[:, sl] + gates[3 * h + 1:3 * h + 2] * a_s[:, sl]
                    + gates[3 * h + 2:3 * h + 3] * a_w[:, sl])
    out = jnp.concatenate(outs, axis=0).T
    o_ref[rows, :] = (out * _silu(z_ref[rows, :].astype(F32))).astype(o_ref.dtype)


def _nsa_attn_kernel(*refs):
    for sub in range(Q_TILES_PER_STEP):
        _nsa_attn_tile(sub, *refs)


def _nsa_attention(fast, q, z, glt, ks_aug, vst, kw_pad, vwt, kc_pad, vct, overlap_t, batch, seq):
    cols = NSA_HPG * Q_TILE
    query = np.arange(cols)[None, :] % Q_TILE
    cmp_thr = jnp.asarray(np.arange(kc_pad.shape[2])[:, None] * CMP_STRIDE + (CMP_BLOCK - 1) - query, jnp.int32)
    key_minus_query = jnp.asarray(np.arange(SEL_KEYS)[:, None] - query, jnp.int32)
    n = batch * seq
    step_q = Q_TILES_PER_STEP * Q_TILE
    n_q = seq // step_q
    gw = NSA_HPG * NSA_DK

    def per_group(arr):
        nd = arr.ndim - 2
        return pl.BlockSpec((1, 1) + arr.shape[2:], lambda b, g, i: (b, g) + (0,) * nd)

    return pl.pallas_call(
        _nsa_attn_kernel,
        grid=(batch, NSA_G, n_q),
        in_specs=[
            pl.BlockSpec(memory_space=pltpu.SMEM),
            pl.BlockSpec((step_q, gw), lambda b, g, i: (b * n_q + i, g)),
            pl.BlockSpec((1, 1, glt.shape[2], step_q), lambda b, g, i: (b, g, 0, i)),
            pl.BlockSpec((step_q, gw), lambda b, g, i: (b * n_q + i, g)),
            per_group(ks_aug), per_group(vst), per_group(kw_pad), per_group(vwt),
            per_group(kc_pad), per_group(vct),
            _const_spec(overlap_t.shape), _const_spec(cmp_thr.shape), _const_spec(key_minus_query.shape),
        ],
        out_specs=pl.BlockSpec((step_q, gw), lambda b, g, i: (b * n_q + i, g)),
        out_shape=jax.ShapeDtypeStruct((n, NSA_HEADS * NSA_DK), BF16),
        scratch_shapes=([pltpu.VMEM((V_ROWS, cols), F32)] * 2 + [pltpu.VMEM((SEL_BLOCK, Q_TILE), F32)]
                        + [pltpu.VMEM((SEL_KEYS, cols), BF16)] * 2
                        + [pltpu.VMEM((NSA_DK, cols), F32), pltpu.VMEM((SEL_BLOCK, Q_TILE), F32)]),
        compiler_params=pltpu.CompilerParams(
            dimension_semantics=("arbitrary", "arbitrary", "arbitrary"),
            vmem_limit_bytes=VMEM_LIMIT),
        name="nsa_attention",
    )(fast, q, glt, z, ks_aug, vst, kw_pad, vwt, kc_pad, vct, overlap_t, cmp_thr, key_minus_query)


def _ret_kernel(q_ref, k_ref, v_ref, z_ref, dec_ref, qd_ref, kd_ref, cd_ref, o_ref, st_ref):
    @pl.when(pl.program_id(1) == 0)
    def _():
        st_ref[...] = jnp.zeros_like(st_ref)

    for h in range(RET_HEADS):
        q = q_ref[:, h * RET_DK:(h + 1) * RET_DK]
        k = k_ref[:, h * RET_DK:(h + 1) * RET_DK]
        v = v_ref[:, h * RET_DV:(h + 1) * RET_DV]
        att = _dot_nt(q, k) * dec_ref[h]
        state = st_ref[h]
        o = _dot(att.astype(BF16), v) + _dot(q, state.astype(BF16)) * qd_ref[h]
        kd_t = (k.astype(F32) * kd_ref[h]).T.astype(BF16)
        st_ref[h] = state * cd_ref[h] + _dot(kd_t, v)
        mu = jnp.mean(o, axis=-1, keepdims=True)
        d = o - mu
        var = jnp.mean(d * d, axis=-1, keepdims=True)
        z = z_ref[:, h * RET_DV:(h + 1) * RET_DV].astype(F32)
        o_ref[:, h * RET_DV:(h + 1) * RET_DV] = (d * lax.rsqrt(var + GN_EPS) * _silu(z)).astype(o_ref.dtype)


def _retention(pr, tables, batch, seq):
    n = batch * seq
    n_c = seq // RET_CHUNK
    dec, qd, kd, cd = tables
    qk_w = RET_HEADS * RET_DK
    v_w = RET_HEADS * RET_DV
    row = lambda b, c: b * n_c + c
    return pl.pallas_call(
        _ret_kernel,
        grid=(batch, n_c),
        in_specs=[
            pl.BlockSpec((RET_CHUNK, qk_w), lambda b, c: (row(b, c), 0)),
            pl.BlockSpec((RET_CHUNK, qk_w), lambda b, c: (row(b, c), 1)),
            pl.BlockSpec((RET_CHUNK, v_w), lambda b, c: (row(b, c), 1)),
            pl.BlockSpec((RET_CHUNK, v_w), lambda b, c: (row(b, c), 2)),
            _const_spec(dec.shape), _const_spec(qd.shape), _const_spec(kd.shape), _const_spec(cd.shape),
        ],
        out_specs=pl.BlockSpec((RET_CHUNK, v_w), lambda b, c: (row(b, c), 0)),
        out_shape=jax.ShapeDtypeStruct((n, v_w), BF16),
        scratch_shapes=[pltpu.VMEM((RET_HEADS, RET_DK, RET_DV), F32)],
        compiler_params=pltpu.CompilerParams(
            dimension_semantics=("arbitrary", "arbitrary"), vmem_limit_bytes=VMEM_LIMIT),
        name="retention",
    )(pr, pr, pr, pr, dec, qd, kd, cd)


def _overlap_matrix_t(n_cmp_pad):
    i = np.arange(n_cmp_pad)[None, :]
    j = np.arange(SEL_BLOCK)[:, None]
    ov = (i * CMP_STRIDE < (j + 1) * SEL_BLOCK) & (i * CMP_STRIDE + CMP_BLOCK > j * SEL_BLOCK)
    return jnp.asarray(ov, BF16)


def _rotary_tables(seq):
    half = RET_DK // 2
    inv = ROPE_BASE ** (-jnp.linspace(0.0, 1.0, half, dtype=F32))
    ang = jnp.arange(seq, dtype=F32)[:, None] * inv[None, :]
    return jnp.cos(ang), jnp.sin(ang)


def _retention_tables():
    c = RET_CHUNK
    log_g = jnp.log(1.0 - 2.0 ** (-5.0 - jnp.arange(RET_HEADS, dtype=F32)))
    ix = jnp.arange(c, dtype=F32)
    diff = ix[:, None] - ix[None, :]
    dec = jnp.where(diff >= 0, jnp.exp(log_g[:, None, None] * jnp.maximum(diff, 0.0)), 0.0)
    qd = jnp.exp(log_g[:, None] * (ix + 1.0))[:, :, None]
    kd = jnp.exp(log_g[:, None] * (c - 1.0 - ix))[:, :, None]
    cd = jnp.broadcast_to(jnp.exp(log_g * c)[:, None, None], (RET_HEADS, 1, RET_DV))
    return dec, qd, kd, cd


def _nsa_layer(x2d, p2d, batch, seq, norm_g, w_in, q_g, kc_g, ks_g, kw_g, pos_k, pos_v,
               ck_w1, ck_w2, cv_w1, cv_w2, w_out, ple_w, ple_gate_w):
    qw = NSA_HEADS * NSA_DK
    kvw = NSA_G * NSA_DK
    n_gate = 3 * NSA_HPG
    sizes = [qw] + [kvw] * 6 + [3 * NSA_HEADS, qw]
    splits = np.concatenate([[0], np.cumsum(sizes)])
    names = ["q", "kc", "vc", "ks", "vs", "kw", "vw"]
    offs = {nm: int(splits[i]) for i, nm in enumerate(names)}
    front = int(splits[7])
    offs["z"], offs["gl"] = front, front + qw
    wgl = w_in[:, splits[7]:splits[8]].reshape(D_MODEL, NSA_G, n_gate)
    wgl = jnp.pad(wgl, ((0, 0), (0, 0), (0, GATE_ROWS - n_gate))).reshape(D_MODEL, NSA_G * GATE_ROWS)
    wgl = jnp.pad(wgl, ((0, 0), (0, LANES - NSA_G * GATE_ROWS)))
    w = jnp.concatenate([w_in[:, :front], w_in[:, splits[8]:], wgl], axis=1).astype(BF16)
    zeros = lambda width: jnp.zeros((width,), F32)
    head_gain = jnp.concatenate([
        jnp.tile(q_g, NSA_HEADS) * (NSA_DK ** -0.5 * LOG2E), zeros(2 * kvw),
        jnp.tile(ks_g, NSA_G), zeros(kvw), jnp.tile(kw_g, NSA_G), zeros(w.shape[1] - offs["vw"])]).reshape(1, -1)

    q_norm = LOG2E * jnp.max(jnp.abs(q_g)) * 1.02
    bound_sel = (q_norm * NSA_DK ** 0.5 * jnp.max(jnp.abs(ks_g))).astype(BF16).astype(F32) * 1.01
    bound_win = (q_norm * NSA_DK ** 0.5 * jnp.max(jnp.abs(kw_g))).astype(BF16).astype(F32) * 1.01
    bound_cmp = (q_norm * NSA_DK ** 0.5 * jnp.max(jnp.abs(kc_g))).astype(BF16).astype(F32) * 1.01
    bound_max = jnp.maximum(jnp.maximum(bound_sel, bound_win), bound_cmp)
    fast = (bound_max <= MAX_SAFE_BOUND).astype(jnp.int32).reshape(1)

    blk_id = np.arange(seq)[:, None] // SEL_BLOCK
    upper = np.arange(LANES)[None, :] - NSA_DK
    onehot = jnp.asarray((blk_id == upper) & (upper > 0), F32)
    lane64 = jnp.asarray(upper == 0)
    aug_sel = jnp.where(lane64, -bound_sel, onehot).astype(BF16)
    aug_win = jnp.where(lane64, -bound_win, jnp.zeros((seq, LANES), F32)).astype(BF16)
    pad_rows = jnp.where(lane64, SEL_MASK_BIAS, jnp.zeros((WINDOW, LANES), F32)).astype(BF16)

    q, z, kc_sub, vc_sub, ks_aug, kw_pad, vst, vwt, glt = _nsa_proj(
        x2d, norm_g, w, head_gain, aug_sel, aug_win, pad_rows, offs, batch, seq)

    n_sub = seq // CMP_STRIDE
    half = CMP_STRIDE * NSA_DK
    w1cat = lambda w1: jnp.concatenate([w1[:half], w1[half:]], axis=1).astype(BF16)
    pos_flat = lambda pos: jnp.pad(pos.reshape(1, CMP_BLOCK * NSA_DK), ((0, 7), (0, 0))).astype(BF16)
    w2pad = lambda w2: jnp.pad(w2, ((0, 0), (0, LANES - NSA_DK))).astype(BF16)
    gk = jnp.pad(kc_g, (0, LANES - NSA_DK)).reshape(1, LANES)
    k_shift = jnp.where(lane64, -bound_cmp, 0.0)
    kc_pad, vct = _compress(kc_sub, vc_sub, w1cat(ck_w1), w1cat(cv_w1),
                            pos_flat(pos_k), pos_flat(pos_v), w2pad(ck_w2), w2pad(cv_w2), gk, k_shift)

    a = _nsa_attention(fast, q, z, glt, ks_aug, vst, kw_pad, vwt, kc_pad, vct,
                       _overlap_matrix_t(n_sub), batch, seq)
    return _out_proj(a, w_out, x2d, p2d, ple_gate_w, ple_w)


def _ret_layer(x2d, p2d, batch, seq, norm_g, w_in, w_out, ple_w, ple_gate_w):
    cos, sin = _rotary_tables(seq)
    pr = _ret_proj(x2d, norm_g, w_in.astype(BF16), cos, sin, seq)
    a = _retention(pr, _retention_tables(), batch, seq)
    return _out_proj(a, w_out, x2d, p2d, ple_gate_w, ple_w)


def kernel(x, p, norm_g, nsa_w_in, nsa_q_g, nsa_kc_g, nsa_ks_g, nsa_kw_g, nsa_cmp_pos_k, nsa_cmp_pos_v, nsa_cmp_k_w1, nsa_cmp_k_w2, nsa_cmp_v_w1, nsa_cmp_v_w2, nsa_w_out, ret_w_in, ret_w_out, ple_w, ple_gate_w):
    batch, seq, d_model = x.shape
    depth = p.shape[0]
    n = batch * seq
    x2d = x.reshape(n, d_model)
    for i in range(depth):
        p2d = p[i].reshape(n, PLE_DIM)
        j = i // 2
        if i % 2 == 0:
            x2d = _nsa_layer(x2d, p2d, batch, seq, norm_g[i], nsa_w_in[j], nsa_q_g[j], nsa_kc_g[j],
                             nsa_ks_g[j], nsa_kw_g[j], nsa_cmp_pos_k[j], nsa_cmp_pos_v[j],
                             nsa_cmp_k_w1[j], nsa_cmp_k_w2[j], nsa_cmp_v_w1[j], nsa_cmp_v_w2[j],
                             nsa_w_out[j], ple_w[i], ple_gate_w[i])
        else:
            x2d = _ret_layer(x2d, p2d, batch, seq, norm_g[i], ret_w_in[j], ret_w_out[j],
                             ple_w[i], ple_gate_w[i])
    return x2d.reshape(batch, seq, d_model)
```

```python
import functools

import numpy as np
import jax
import jax.numpy as jnp
from jax import lax
from jax.experimental import pallas as pl
from jax.experimental.pallas import tpu as pltpu

F32 = jnp.float32
BF16 = jnp.bfloat16

D_MODEL = 1024
PLE_DIM = 256
RMS_EPS = 1e-6
GN_EPS = 1e-5

NSA_HEADS = 16
NSA_DK = 64
NSA_G = 4
NSA_HPG = 4
CMP_BLOCK = 32
CMP_STRIDE = 16
CMP_HIDDEN = 256
SEL_BLOCK = 64
SEL_TOPN = 16
WINDOW = 512
FORCE_BONUS = 1e4
NEG = -1e30
SEL_MASK_BIAS = -30000.0

RET_HEADS = 4
RET_DK = 256
RET_DV = 512
RET_CHUNK = 256
ROPE_BASE = 10000.0

LANES = 128
VMEM_LIMIT = 56 * 1024 * 1024

ROW_TILE = 512
GROUP_W = NSA_HPG * NSA_DK
GATE_ROWS = 16
Q_TILE = 256
Q_TILES_PER_STEP = 4
SEL_KEYS = 512
LOG2E = 1.4426950408889634
MAX_SAFE_BOUND = 50.0
V_ROWS = NSA_DK + 16


def _dot(a, b):
    return jnp.dot(a, b, preferred_element_type=F32)


def _dot_nt(a, b):
    return lax.dot_general(a, b, (((1,), (1,)), ((), ())), preferred_element_type=F32)


def _silu(x):
    return x * jax.nn.sigmoid(x)


def _const_spec(shape):
    nd = len(shape)
    return pl.BlockSpec(shape, lambda *_: (0,) * nd)


def _ret_proj_kernel(x_ref, g_ref, w_ref, cos_ref, sin_ref, o_ref):
    x = x_ref[...]
    h = (x * lax.rsqrt(jnp.mean(x * x, axis=-1, keepdims=True) + RMS_EPS) * g_ref[...]).astype(BF16)
    cos, sin = cos_ref[...], sin_ref[...]
    half = RET_DK // 2
    qk_w = RET_HEADS * RET_DK
    for t in range(2 * RET_HEADS):
        c0 = t * RET_DK
        acc = _dot(h, w_ref[:, c0:c0 + RET_DK])
        a1, a2 = acc[:, :half], acc[:, half:]
        rot = jnp.concatenate([a1 * cos - a2 * sin, a1 * sin + a2 * cos], axis=-1)
        if c0 >= qk_w:
            rot = rot * RET_DK ** -0.5
        o_ref[:, c0:c0 + RET_DK] = rot.astype(o_ref.dtype)
    for c0 in range(2 * qk_w, w_ref.shape[1], RET_DV):
        o_ref[:, c0:c0 + RET_DV] = _dot(h, w_ref[:, c0:c0 + RET_DV]).astype(o_ref.dtype)


def _ret_proj(x2d, norm_g, w, cos, sin, seq):
    n = x2d.shape[0]
    cols = w.shape[1]
    n_pos_blocks = seq // ROW_TILE
    return pl.pallas_call(
        _ret_proj_kernel,
        grid=(n // ROW_TILE,),
        in_specs=[
            pl.BlockSpec((ROW_TILE, D_MODEL), lambda i: (i, 0)),
            _const_spec((1, D_MODEL)),
            _const_spec((D_MODEL, cols)),
            pl.BlockSpec((ROW_TILE, LANES), lambda i: (i % n_pos_blocks, 0)),
            pl.BlockSpec((ROW_TILE, LANES), lambda i: (i % n_pos_blocks, 0)),
        ],
        out_specs=pl.BlockSpec((ROW_TILE, cols), lambda i: (i, 0)),
        out_shape=jax.ShapeDtypeStruct((n, cols), BF16),
        compiler_params=pltpu.CompilerParams(
            dimension_semantics=("arbitrary",), vmem_limit_bytes=VMEM_LIMIT),
        name="ret_norm_proj",
    )(x2d, norm_g.reshape(1, D_MODEL), w, cos, sin)


def _nsa_proj_kernel(x_ref, g_ref, w_ref, hg_ref, bd_ref, augs_ref, augw_ref, pad_ref,
                     q_ref, z_ref, kcs_ref, vcs_ref, ks_ref, kw_ref, vst_ref, vwt_ref, glt_ref, cmp_scr, *, offs):
    s = pl.program_id(1)
    tiles = ROW_TILE // Q_TILE

    @pl.when(s == 0)
    def _():
        for g in range(NSA_G):
            kw_ref[0, g] = pad_ref[...]
        vwt_ref[...] = jnp.zeros_like(vwt_ref)

    @pl.when(s > 0)
    def _():
        x = x_ref[...]
        h = (x * lax.rsqrt(jnp.mean(x * x, axis=-1, keepdims=True) + RMS_EPS) * g_ref[...]).astype(BF16)
        low = lax.broadcasted_iota(jnp.int32, (ROW_TILE, LANES), 1) < NSA_DK

        def proj(c0, width=GROUP_W):
            return _dot(h, w_ref[:, c0:c0 + width])

        def head_norm(acc, c0):
            ss = _dot((acc * acc).astype(BF16), bd_ref[...])
            return acc * lax.rsqrt(ss * (1.0 / NSA_DK) + RMS_EPS) * hg_ref[:, c0:c0 + GROUP_W]

        for t in range(NSA_G):
            lanes = slice(GROUP_W * t, GROUP_W * (t + 1))
            c0 = offs["q"] + GROUP_W * t
            q_ref[:, lanes] = head_norm(proj(c0), c0).astype(BF16)
            z_ref[:, lanes] = proj(offs["z"] + GROUP_W * t).astype(BF16)

        kvw = NSA_G * NSA_DK
        ones = jnp.ones((V_ROWS - NSA_DK, Q_TILE), BF16)
        for name, aug_ref, k_ref, v_ref in (("ks", augs_ref, ks_ref, vst_ref), ("kw", augw_ref, kw_ref, vwt_ref)):
            kv = proj(offs[name], 2 * kvw)
            k = head_norm(kv[:, :kvw], offs[name])
            aug = aug_ref[...].astype(F32)
            for pair in range(NSA_G // 2):
                two = k[:, LANES * pair:LANES * (pair + 1)]
                k_ref[0, 2 * pair] = jnp.where(low, two, aug).astype(BF16)
                k_ref[0, 2 * pair + 1] = jnp.where(low, pltpu.roll(two, NSA_DK, 1), aug).astype(BF16)
            vt = kv[:, kvw:].T
            for g in range(NSA_G):
                for j in range(tiles):
                    v_ref[0, g, j, 0:NSA_DK, :] = vt[g * NSA_DK:(g + 1) * NSA_DK,
                                                     j * Q_TILE:(j + 1) * Q_TILE].astype(BF16)
                    v_ref[0, g, j, NSA_DK:V_ROWS, :] = ones

        cmp_in = proj(offs["kc"], 2 * kvw)
        for t in range(2 * kvw // LANES):
            cmp_scr[t] = cmp_in[:, LANES * t:LANES * (t + 1)]
        n_sub = ROW_TILE // CMP_STRIDE
        low_sub = lax.broadcasted_iota(jnp.int32, (n_sub, LANES), 1) < NSA_DK
        for r in range(0, CMP_STRIDE, 2):
            for i, o_ref in enumerate((kcs_ref, vcs_ref)):
                for pair in range(NSA_G // 2):
                    t = i * (NSA_G // 2) + pair
                    e2 = cmp_scr[t, pl.ds(r, n_sub, stride=CMP_STRIDE), :]
                    o2 = cmp_scr[t, pl.ds(r + 1, n_sub, stride=CMP_STRIDE), :]
                    lanes = slice(LANES * (r // 2), LANES * (r // 2 + 1))
                    o_ref[0, 2 * pair, :, lanes] = jnp.where(low_sub, e2, pltpu.roll(o2, NSA_DK, 1)).astype(BF16)
                    o_ref[0, 2 * pair + 1, :, lanes] = jnp.where(low_sub, pltpu.roll(e2, NSA_DK, 1), o2).astype(BF16)

        glt = proj(offs["gl"], LANES).T
        for g in range(NSA_G):
            glt_ref[0, g] = glt[GATE_ROWS * g:GATE_ROWS * (g + 1)]


def _nsa_proj(x2d, norm_g, w, head_gain, aug_sel, aug_win, pad_rows, offs, batch, seq):
    n = batch * seq
    nsb = seq // ROW_TILE
    tiles = ROW_TILE // Q_TILE
    qw = NSA_HEADS * NSA_DK
    kvw = NSA_G * NSA_DK
    sub_w = CMP_STRIDE * NSA_DK
    lane_head = np.arange(GROUP_W) // NSA_DK
    bd = jnp.asarray(lane_head[:, None] == lane_head[None, :], BF16)
    prev = lambda s: jnp.maximum(s - 1, 0)
    rows = lambda b, s: (b * nsb + prev(s), 0)
    out_shape = [
        jax.ShapeDtypeStruct((n, qw), BF16), jax.ShapeDtypeStruct((n, qw), BF16),
        jax.ShapeDtypeStruct((batch, NSA_G, seq // CMP_STRIDE, sub_w), BF16),
        jax.ShapeDtypeStruct((batch, NSA_G, seq // CMP_STRIDE, sub_w), BF16),
        jax.ShapeDtypeStruct((batch, NSA_G, seq, LANES), BF16),
        jax.ShapeDtypeStruct((batch, NSA_G, seq + WINDOW, LANES), BF16),
        jax.ShapeDtypeStruct((batch, NSA_G, seq // Q_TILE, V_ROWS, Q_TILE), BF16),
        jax.ShapeDtypeStruct((batch, NSA_G, (seq + WINDOW) // Q_TILE, V_ROWS, Q_TILE), BF16),
        jax.ShapeDtypeStruct((batch, NSA_G, GATE_ROWS, seq), F32),
    ]
    out_specs = [
        pl.BlockSpec((ROW_TILE, qw), rows), pl.BlockSpec((ROW_TILE, qw), rows),
        pl.BlockSpec((1, NSA_G, ROW_TILE // CMP_STRIDE, sub_w), lambda b, s: (b, 0, prev(s), 0)),
        pl.BlockSpec((1, NSA_G, ROW_TILE // CMP_STRIDE, sub_w), lambda b, s: (b, 0, prev(s), 0)),
        pl.BlockSpec((1, NSA_G, ROW_TILE, LANES), lambda b, s: (b, 0, prev(s), 0)),
        pl.BlockSpec((1, NSA_G, ROW_TILE, LANES), lambda b, s: (b, 0, s, 0)),
        pl.BlockSpec((1, NSA_G, tiles, V_ROWS, Q_TILE), lambda b, s: (b, 0, prev(s), 0, 0)),
        pl.BlockSpec((1, NSA_G, tiles, V_ROWS, Q_TILE), lambda b, s: (b, 0, s, 0, 0)),
        pl.BlockSpec((1, NSA_G, GATE_ROWS, ROW_TILE), lambda b, s: (b, 0, 0, prev(s))),
    ]
    assert WINDOW == ROW_TILE
    return pl.pallas_call(
        functools.partial(_nsa_proj_kernel, offs=offs),
        grid=(batch, nsb + 1),
        in_specs=[
            pl.BlockSpec((ROW_TILE, D_MODEL), rows),
            _const_spec((1, D_MODEL)),
            _const_spec(w.shape),
            _const_spec(head_gain.shape),
            _const_spec(bd.shape),
            pl.BlockSpec((ROW_TILE, LANES), lambda b, s: (prev(s), 0)),
            pl.BlockSpec((ROW_TILE, LANES), lambda b, s: (prev(s), 0)),
            _const_spec(pad_rows.shape),
        ],
        out_specs=out_specs,
        out_shape=out_shape,
        scratch_shapes=[pltpu.VMEM((2 * kvw // LANES, ROW_TILE, LANES), F32)],
        compiler_params=pltpu.CompilerParams(
            dimension_semantics=("arbitrary", "arbitrary"), vmem_limit_bytes=VMEM_LIMIT),
        name="nsa_norm_proj",
    )(x2d, norm_g.reshape(1, D_MODEL), w, head_gain, bd, aug_sel, aug_win, pad_rows)


def _out_kernel(a_ref, wo_ref, x_ref, p_ref, wg_ref, wp_ref, o_ref):
    x1 = x_ref[...] + _dot(a_ref[...], wo_ref[...])
    gate = jax.nn.sigmoid(_dot(x1.astype(BF16), wg_ref[...]))
    emb = _dot(p_ref[0].astype(BF16), wp_ref[...])
    o_ref[...] = x1 + gate * emb


def _out_proj(a, w_out, x2d, p_all, layer, w_gate, w_ple):
    n, k = a.shape
    return pl.pallas_call(
        _out_kernel,
        grid=(n // ROW_TILE,),
        in_specs=[
            pl.BlockSpec((ROW_TILE, k), lambda i: (i, 0)),
            _const_spec((k, D_MODEL)),
            pl.BlockSpec((ROW_TILE, D_MODEL), lambda i: (i, 0)),
            pl.BlockSpec((1, ROW_TILE, PLE_DIM), lambda i: (layer, i, 0)),
            _const_spec((D_MODEL, D_MODEL)),
            _const_spec((PLE_DIM, D_MODEL)),
        ],
        out_specs=pl.BlockSpec((ROW_TILE, D_MODEL), lambda i: (i, 0)),
        out_shape=jax.ShapeDtypeStruct((n, D_MODEL), F32),
        compiler_params=pltpu.CompilerParams(
            dimension_semantics=("arbitrary",), vmem_limit_bytes=VMEM_LIMIT),
        name="out_proj_ple",
    )(a, w_out.astype(BF16), x2d, p_all, w_gate.astype(BF16), w_ple.astype(BF16))


def _cmp_kernel(xk_ref, xv_ref, w1k_ref, w1v_ref, posk_ref, posv_ref, w2k_ref, w2v_ref,
                gk_ref, shift_ref, kc_ref, vc_ref):
    n_sub = xk_ref.shape[2]
    half = xk_ref.shape[3]
    for is_k, x_ref, w1_ref, pos_ref, w2_ref, o_ref in (
            (True, xk_ref, w1k_ref, posk_ref, w2k_ref, kc_ref),
            (False, xv_ref, w1v_ref, posv_ref, w2v_ref, vc_ref)):
        w1 = w1_ref[...]
        x = x_ref[0].reshape(NSA_G * n_sub, half)
        ab = _dot(x, w1)
        pos = pos_ref[...]
        pos_term = (_dot(pos[:, :half], w1[:, :CMP_HIDDEN])
                    + _dot(pos[:, half:], w1[:, CMP_HIDDEN:]))[0:1]
        for g in range(NSA_G):
            first = ab[g * n_sub:(g + 1) * n_sub, :CMP_HIDDEN]
            second = ab[g * n_sub:(g + 1) * n_sub, CMP_HIDDEN:]
            hid = _silu(first + pltpu.roll(second, n_sub - 1, 0) + pos_term)
            c = _dot(hid.astype(BF16), w2_ref[...])
            if is_k:
                ss = jnp.sum(c * c, axis=-1, keepdims=True) * (1.0 / NSA_DK)
                c = c * lax.rsqrt(ss + RMS_EPS) * gk_ref[...] + shift_ref[...]
                o_ref[0, g] = c.astype(BF16)
            else:
                ct = jnp.concatenate([c.T[:NSA_DK], jnp.ones((V_ROWS - NSA_DK, n_sub), F32)], axis=0)
                o_ref[0, g] = ct.astype(BF16)


def _compress(xk, xv, w1k, w1v, posk, posv, w2k, w2v, gk, k_shift):
    b, g, n_sub, half = xk.shape
    blk = pl.BlockSpec((1, g, n_sub, half), lambda i: (i, 0, 0, 0))
    k_blk = pl.BlockSpec((1, g, n_sub, LANES), lambda i: (i, 0, 0, 0))
    k_sds = jax.ShapeDtypeStruct((b, g, n_sub, LANES), BF16)
    v_blk = pl.BlockSpec((1, g, V_ROWS, n_sub), lambda i: (i, 0, 0, 0))
    v_sds = jax.ShapeDtypeStruct((b, g, V_ROWS, n_sub), BF16)
    return pl.pallas_call(
        _cmp_kernel,
        grid=(b,),
        in_specs=[blk, blk,
                  _const_spec(w1k.shape), _const_spec(w1v.shape),
                  _const_spec(posk.shape), _const_spec(posv.shape),
                  _const_spec(w2k.shape), _const_spec(w2v.shape),
                  _const_spec(gk.shape), _const_spec(k_shift.shape)],
        out_specs=[k_blk, v_blk],
        out_shape=[k_sds, v_sds],
        compiler_params=pltpu.CompilerParams(
            dimension_semantics=("arbitrary",), vmem_limit_bytes=VMEM_LIMIT),
        name="nsa_compress",
    )(xk, xv, w1k, w1v, posk, posv, w2k, w2v, gk, k_shift)


def _nsa_attn_tile(sub, fast_ref, q_ref, glt_ref, z_ref, ks_ref, vst_ref, kw_ref, vwt_ref, kc_ref, vct_ref,
                   ovt_ref, cthr_ref, kmq_ref, o_ref, acc_ref, accw_ref, rank_ref, p0_ref, p1_ref, ac_ref, impt_ref):
    tq = Q_TILE
    cols = NSA_HPG * tq
    qi = pl.program_id(2) * Q_TILES_PER_STEP + sub
    t0 = qi * tq
    rows = slice(sub * tq, (sub + 1) * tq)
    n_sel = SEL_BLOCK
    dk = NSA_DK

    qt = q_ref[rows, :].astype(F32).T
    q_heads = [qt[h * dk:(h + 1) * dk] for h in range(NSA_HPG)]

    def stack_heads(extra_rows):
        return jnp.concatenate(
            [jnp.concatenate([qh, extra_rows], axis=0) for qh in q_heads], axis=1).astype(BF16)

    blk = lax.broadcasted_iota(jnp.int32, (n_sel, tq), 0)
    ones_row = jnp.where(blk == 0, 1.0, 0.0)
    fast = fast_ref[0] > 0

    def compressed(subtract_max):
        sc = _dot(kc_ref[0, 0], stack_heads(ones_row))
        sc = jnp.where(cthr_ref[...] <= t0, sc, NEG)
        if subtract_max:
            sc = sc - jnp.maximum(jnp.max(sc, axis=0, keepdims=True), 0.1 * NEG)
        ec = jnp.exp2(sc).astype(BF16)
        r = _dot(jnp.concatenate([vct_ref[0, 0], ovt_ref[...]], axis=0), ec)
        lc = r[dk:dk + 1]
        inv_lc = 1.0 / jnp.where(lc > 0.0, lc, 1.0)
        ac_ref[...] = r[:dk] * inv_lc
        imp = r[V_ROWS:] * inv_lc
        impt_ref[...] = imp[:, 0:tq] + imp[:, tq:2 * tq] + imp[:, 2 * tq:3 * tq] + imp[:, 3 * tq:4 * tq]

    pl.when(fast)(lambda: compressed(False))
    pl.when(jnp.logical_not(fast))(lambda: compressed(True))
    imp_t = impt_ref[...]

    cur = (t0 + lax.broadcasted_iota(jnp.int32, (n_sel, tq), 1)) // SEL_BLOCK
    forced = (blk == 0) | (blk == cur) | (blk == cur - 1)
    score = jnp.where(blk <= cur, imp_t + jnp.where(forced, FORCE_BONUS, 0.0), NEG)
    sub = 8
    chunks = [score[c * sub:(c + 1) * sub] for c in range(n_sel // sub)]
    blk_sub = lax.broadcasted_iota(jnp.int32, (sub, tq), 0)
    last_blk = (t0 + tq - 1) // SEL_BLOCK
    rank_ref[...] = jnp.zeros_like(rank_ref)
    for jb in range(n_sel // sub):
        @pl.when(jb * sub <= last_blk)
        def _():
            ranks = [jnp.zeros((sub, tq), F32) for _ in chunks]
            for j in range(jb * sub, (jb + 1) * sub):
                row = jnp.broadcast_to(chunks[jb][j - jb * sub:j - jb * sub + 1, :], (sub, tq))
                for c, chunk in enumerate(chunks):
                    if c > jb:
                        one = jnp.where(row >= chunk, 1.0, 0.0)
                    elif c < jb:
                        one = jnp.where(row > chunk, 1.0, 0.0)
                    else:
                        one = jnp.where(blk_sub > j - c * sub,
                                        jnp.where(row >= chunk, 1.0, 0.0), jnp.where(row > chunk, 1.0, 0.0))
                    ranks[c] = ranks[c] + one
            rank_ref[...] += jnp.concatenate(ranks, axis=0)
    bias_t = jnp.where(rank_ref[...] < float(SEL_TOPN), 0.0, SEL_MASK_BIAS)
    q_aug = stack_heads(jnp.where(blk == 0, 1.0, bias_t))

    tiles_per_step = SEL_KEYS // tq
    n_full = t0 // SEL_KEYS
    n_wt = WINDOW // tq + 1
    diag_keep = kmq_ref[0:tq, :] <= 0

    def sel_scores(step, tail):
        base = pl.multiple_of(step * SEL_KEYS, SEL_KEYS)
        s = _dot(ks_ref[0, 0, pl.ds(base, SEL_KEYS), :], q_aug)
        if tail:
            s = jnp.where(kmq_ref[...] <= t0 - base, s, NEG)
        vt = vst_ref[0, 0, pl.ds(step * tiles_per_step, tiles_per_step)]
        return s, jnp.concatenate([vt[i] for i in range(tiles_per_step)], axis=1)

    def win_scores():
        sw = _dot(kw_ref[0, 0, pl.ds(pl.multiple_of(t0, tq), n_wt * tq), :], q_aug)
        parts = [jnp.where(diag_keep, NEG, sw[:tq])]
        if WINDOW > tq:
            parts.append(sw[tq:WINDOW])
        parts.append(jnp.where(diag_keep, sw[WINDOW:], NEG))
        sw = jnp.concatenate(parts, axis=0)
        vwt = vwt_ref[0, 0, pl.ds(qi, n_wt)]
        return sw, jnp.concatenate([vwt[i] for i in range(n_wt)], axis=1)

    @pl.when(fast)
    def _():
        acc_ref[...] = jnp.zeros_like(acc_ref)

        def values(step):
            vt = vst_ref[0, 0, pl.ds(step * tiles_per_step, tiles_per_step)]
            return jnp.concatenate([vt[i] for i in range(tiles_per_step)], axis=1)

        def probs(step, tail):
            return jnp.exp2(sel_scores(step, tail)[0]).astype(BF16)

        odd = n_full % 2
        half = SEL_KEYS // 2
        assert half == tq

        @pl.when(t0 % SEL_KEYS != 0)
        def _():
            p1_ref[...] = probs(n_full, True)

        @pl.when(t0 % SEL_KEYS == 0)
        def _():
            s = _dot(ks_ref[0, 0, pl.ds(pl.multiple_of(t0, SEL_KEYS), half), :], q_aug)
            p1_ref[0:half, :] = jnp.exp2(jnp.where(diag_keep, s, NEG)).astype(BF16)
            p1_ref[half:, :] = jnp.zeros((half, cols), BF16)

        @pl.when(odd == 1)
        def _():
            p = probs(n_full - 1, False)
            acc_ref[...] += _dot(values(n_full), p1_ref[...])
            p1_ref[...] = p

        first_pending = n_full - odd

        def pair(j, carry):
            pending = jnp.where(j == 0, first_pending, 2 * j - 1)
            pa = probs(2 * j, False)
            acc_ref[...] += _dot(values(pending), p1_ref[...])
            p0_ref[...] = pa
            pb = probs(2 * j + 1, False)
            acc_ref[...] += _dot(values(2 * j), p0_ref[...])
            p1_ref[...] = pb
            return carry

        n_pairs = n_full // 2
        lax.fori_loop(0, n_pairs, pair, 0)
        pending = jnp.where(n_pairs == 0, first_pending, 2 * n_pairs - 1)
        sw, vwt = win_scores()
        pw = jnp.exp2(sw).astype(BF16)
        acc_ref[...] += _dot(values(pending), p1_ref[...])
        accw_ref[...] = _dot(vwt, pw)

    @pl.when(jnp.logical_not(fast))
    def _():
        acc_ref[...] = jnp.zeros_like(acc_ref)

        def online(s, vt, m):
            m_new = jnp.maximum(m, jnp.max(s, axis=0, keepdims=True))
            acc_ref[...] = jnp.exp2(m - m_new) * acc_ref[...] + _dot(vt, jnp.exp2(s - m_new).astype(BF16))
            return m_new

        m = jnp.full((1, cols), NEG, F32)
        m = lax.fori_loop(0, n_full, lambda i, m: online(*sel_scores(i, False), m), m)
        online(*sel_scores(n_full, True), m)
        sw, vwt = win_scores()
        accw_ref[...] = _dot(vwt, jnp.exp2(sw - jnp.max(sw, axis=0, keepdims=True)).astype(BF16))

    a_s = acc_ref[...]
    a_w = accw_ref[...]

    a_s = a_s[:dk] * (1.0 / a_s[dk:dk + 1])
    a_w = a_w[:dk] * (1.0 / a_w[dk:dk + 1])
    gates = jax.nn.sigmoid(glt_ref[0, 0, :, rows])
    outs = []
    for h in range(NSA_HPG):
        sl = slice(h * tq, (h + 1) * tq)
        outs.append(gates[3 * h:3 * h + 1] * ac_ref[:, sl] + gates[3 * h + 1:3 * h + 2] * a_s[:, sl]
                    + gates[3 * h + 2:3 * h + 3] * a_w[:, sl])
    out = jnp.concatenate(outs, axis=0).T
    o_ref[rows, :] = (out * _silu(z_ref[rows, :].astype(F32))).astype(o_ref.dtype)


def _nsa_attn_kernel(*refs):
    for sub in range(Q_TILES_PER_STEP):
        _nsa_attn_tile(sub, *refs)


def _nsa_attention(fast, q, z, glt, ks_aug, vst, kw_pad, vwt, kc_pad, vct, overlap_t, batch, seq):
    cols = NSA_HPG * Q_TILE
    query = np.arange(cols)[None, :] % Q_TILE
    cmp_thr = jnp.asarray(np.arange(kc_pad.shape[2])[:, None] * CMP_STRIDE + (CMP_BLOCK - 1) - query, jnp.int32)
    key_minus_query = jnp.asarray(np.arange(SEL_KEYS)[:, None] - query, jnp.int32)
    n = batch * seq
    step_q = Q_TILES_PER_STEP * Q_TILE
    n_q = seq // step_q
    gw = NSA_HPG * NSA_DK

    def per_group(arr):
        nd = arr.ndim - 2
        return pl.BlockSpec((1, 1) + arr.shape[2:], lambda b, g, i: (b, g) + (0,) * nd)

    return pl.pallas_call(
        _nsa_attn_kernel,
        grid=(batch, NSA_G, n_q),
        in_specs=[
            pl.BlockSpec(memory_space=pltpu.SMEM),
            pl.BlockSpec((step_q, gw), lambda b, g, i: (b * n_q + i, g)),
            pl.BlockSpec((1, 1, glt.shape[2], step_q), lambda b, g, i: (b, g, 0, i)),
            pl.BlockSpec((step_q, gw), lambda b, g, i: (b * n_q + i, g)),
            per_group(ks_aug), per_group(vst), per_group(kw_pad), per_group(vwt),
            per_group(kc_pad), per_group(vct),
            _const_spec(overlap_t.shape), _const_spec(cmp_thr.shape), _const_spec(key_minus_query.shape),
        ],
        out_specs=pl.BlockSpec((step_q, gw), lambda b, g, i: (b * n_q + i, g)),
        out_shape=jax.ShapeDtypeStruct((n, NSA_HEADS * NSA_DK), BF16),
        scratch_shapes=([pltpu.VMEM((V_ROWS, cols), F32)] * 2 + [pltpu.VMEM((SEL_BLOCK, Q_TILE), F32)]
                        + [pltpu.VMEM((SEL_KEYS, cols), BF16)] * 2
                        + [pltpu.VMEM((NSA_DK, cols), F32), pltpu.VMEM((SEL_BLOCK, Q_TILE), F32)]),
        compiler_params=pltpu.CompilerParams(
            dimension_semantics=("arbitrary", "arbitrary", "arbitrary"),
            vmem_limit_bytes=VMEM_LIMIT),
        name="nsa_attention",
    )(fast, q, glt, z, ks_aug, vst, kw_pad, vwt, kc_pad, vct, overlap_t, cmp_thr, key_minus_query)


def _ret_kernel(q_ref, k_ref, v_ref, z_ref, dec_ref, qd_ref, kd_ref, cd_ref, o_ref, st_ref):
    @pl.when(pl.program_id(1) == 0)
    def _():
        st_ref[...] = jnp.zeros_like(st_ref)

    for h in range(RET_HEADS):
        q = q_ref[:, h * RET_DK:(h + 1) * RET_DK]
        k = k_ref[:, h * RET_DK:(h + 1) * RET_DK]
        v = v_ref[:, h * RET_DV:(h + 1) * RET_DV]
        att = _dot_nt(q, k) * dec_ref[h]
        state = st_ref[h]
        o = _dot(att.astype(BF16), v) + _dot(q, state.astype(BF16)) * qd_ref[h]
        kd_t = (k.astype(F32) * kd_ref[h]).T.astype(BF16)
        st_ref[h] = state * cd_ref[h] + _dot(kd_t, v)
        mu = jnp.mean(o, axis=-1, keepdims=True)
        d = o - mu
        var = jnp.mean(d * d, axis=-1, keepdims=True)
        z = z_ref[:, h * RET_DV:(h + 1) * RET_DV].astype(F32)
        o_ref[:, h * RET_DV:(h + 1) * RET_DV] = (d * lax.rsqrt(var + GN_EPS) * _silu(z)).astype(o_ref.dtype)


def _retention(pr, tables, batch, seq):
    n = batch * seq
    n_c = seq // RET_CHUNK
    dec, qd, kd, cd = tables
    qk_w = RET_HEADS * RET_DK
    v_w = RET_HEADS * RET_DV
    row = lambda b, c: b * n_c + c
    return pl.pallas_call(
        _ret_kernel,
        grid=(batch, n_c),
        in_specs=[
            pl.BlockSpec((RET_CHUNK, qk_w), lambda b, c: (row(b, c), 0)),
            pl.BlockSpec((RET_CHUNK, qk_w), lambda b, c: (row(b, c), 1)),
            pl.BlockSpec((RET_CHUNK, v_w), lambda b, c: (row(b, c), 1)),
            pl.BlockSpec((RET_CHUNK, v_w), lambda b, c: (row(b, c), 2)),
            _const_spec(dec.shape), _const_spec(qd.shape), _const_spec(kd.shape), _const_spec(cd.shape),
        ],
        out_specs=pl.BlockSpec((RET_CHUNK, v_w), lambda b, c: (row(b, c), 0)),
        out_shape=jax.ShapeDtypeStruct((n, v_w), BF16),
        scratch_shapes=[pltpu.VMEM((RET_HEADS, RET_DK, RET_DV), F32)],
        compiler_params=pltpu.CompilerParams(
            dimension_semantics=("arbitrary", "arbitrary"), vmem_limit_bytes=VMEM_LIMIT),
        name="retention",
    )(pr, pr, pr, pr, dec, qd, kd, cd)


def _overlap_matrix_t(n_cmp_pad):
    i = np.arange(n_cmp_pad)[None, :]
    j = np.arange(SEL_BLOCK)[:, None]
    ov = (i * CMP_STRIDE < (j + 1) * SEL_BLOCK) & (i * CMP_STRIDE + CMP_BLOCK > j * SEL_BLOCK)
    return jnp.asarray(ov, BF16)


def _rotary_tables(seq):
    half = RET_DK // 2
    inv = ROPE_BASE ** (-jnp.linspace(0.0, 1.0, half, dtype=F32))
    ang = jnp.arange(seq, dtype=F32)[:, None] * inv[None, :]
    return jnp.cos(ang), jnp.sin(ang)


def _retention_tables():
    c = RET_CHUNK
    log_g = jnp.log(1.0 - 2.0 ** (-5.0 - jnp.arange(RET_HEADS, dtype=F32)))
    ix = jnp.arange(c, dtype=F32)
    diff = ix[:, None] - ix[None, :]
    dec = jnp.where(diff >= 0, jnp.exp(log_g[:, None, None] * jnp.maximum(diff, 0.0)), 0.0)
    qd = jnp.exp(log_g[:, None] * (ix + 1.0))[:, :, None]
    kd = jnp.exp(log_g[:, None] * (c - 1.0 - ix))[:, :, None]
    cd = jnp.broadcast_to(jnp.exp(log_g * c)[:, None, None], (RET_HEADS, 1, RET_DV))
    return dec, qd, kd, cd


def _nsa_layer(x2d, p_all, layer, batch, seq, norm_g, w_in, q_g, kc_g, ks_g, kw_g, pos_k, pos_v,
               ck_w1, ck_w2, cv_w1, cv_w2, w_out, ple_w, ple_gate_w):
    qw = NSA_HEADS * NSA_DK
    kvw = NSA_G * NSA_DK
    n_gate = 3 * NSA_HPG
    sizes = [qw] + [kvw] * 6 + [3 * NSA_HEADS, qw]
    splits = np.concatenate([[0], np.cumsum(sizes)])
    names = ["q", "kc", "vc", "ks", "vs", "kw", "vw"]
    offs = {nm: int(splits[i]) for i, nm in enumerate(names)}
    front = int(splits[7])
    offs["z"], offs["gl"] = front, front + qw
    wgl = w_in[:, splits[7]:splits[8]].reshape(D_MODEL, NSA_G, n_gate)
    wgl = jnp.pad(wgl, ((0, 0), (0, 0), (0, GATE_ROWS - n_gate))).reshape(D_MODEL, NSA_G * GATE_ROWS)
    wgl = jnp.pad(wgl, ((0, 0), (0, LANES - NSA_G * GATE_ROWS)))
    w = jnp.concatenate([w_in[:, :front], w_in[:, splits[8]:], wgl], axis=1).astype(BF16)
    zeros = lambda width: jnp.zeros((width,), F32)
    head_gain = jnp.concatenate([
        jnp.tile(q_g, NSA_HEADS) * (NSA_DK ** -0.5 * LOG2E), zeros(2 * kvw),
        jnp.tile(ks_g, NSA_G), zeros(kvw), jnp.tile(kw_g, NSA_G), zeros(w.shape[1] - offs["vw"])]).reshape(1, -1)

    q_norm = LOG2E * jnp.max(jnp.abs(q_g)) * 1.02
    bound_sel = (q_norm * NSA_DK ** 0.5 * jnp.max(jnp.abs(ks_g))).astype(BF16).astype(F32) * 1.01
    bound_win = (q_norm * NSA_DK ** 0.5 * jnp.max(jnp.abs(kw_g))).astype(BF16).astype(F32) * 1.01
    bound_cmp = (q_norm * NSA_DK ** 0.5 * jnp.max(jnp.abs(kc_g))).astype(BF16).astype(F32) * 1.01
    bound_max = jnp.maximum(jnp.maximum(bound_sel, bound_win), bound_cmp)
    fast = (bound_max <= MAX_SAFE_BOUND).astype(jnp.int32).reshape(1)

    blk_id = np.arange(seq)[:, None] // SEL_BLOCK
    upper = np.arange(LANES)[None, :] - NSA_DK
    onehot = jnp.asarray((blk_id == upper) & (upper > 0), F32)
    lane64 = jnp.asarray(upper == 0)
    aug_sel = jnp.where(lane64, -bound_sel, onehot).astype(BF16)
    aug_win = jnp.where(lane64, -bound_win, jnp.zeros((seq, LANES), F32)).astype(BF16)
    pad_rows = jnp.where(lane64, SEL_MASK_BIAS, jnp.zeros((WINDOW, LANES), F32)).astype(BF16)

    q, z, kc_sub, vc_sub, ks_aug, kw_pad, vst, vwt, glt = _nsa_proj(
        x2d, norm_g, w, head_gain, aug_sel, aug_win, pad_rows, offs, batch, seq)

    n_sub = seq // CMP_STRIDE
    half = CMP_STRIDE * NSA_DK
    w1cat = lambda w1: jnp.concatenate([w1[:half], w1[half:]], axis=1).astype(BF16)
    pos_flat = lambda pos: jnp.pad(pos.reshape(1, CMP_BLOCK * NSA_DK), ((0, 7), (0, 0))).astype(BF16)
    w2pad = lambda w2: jnp.pad(w2, ((0, 0), (0, LANES - NSA_DK))).astype(BF16)
    gk = jnp.pad(kc_g, (0, LANES - NSA_DK)).reshape(1, LANES)
    k_shift = jnp.where(lane64, -bound_cmp, 0.0)
    kc_pad, vct = _compress(kc_sub, vc_sub, w1cat(ck_w1), w1cat(cv_w1),
                            pos_flat(pos_k), pos_flat(pos_v), w2pad(ck_w2), w2pad(cv_w2), gk, k_shift)

    a = _nsa_attention(fast, q, z, glt, ks_aug, vst, kw_pad, vwt, kc_pad, vct,
                       _overlap_matrix_t(n_sub), batch, seq)
    return _out_proj(a, w_out, x2d, p_all, layer, ple_gate_w, ple_w)


def _ret_layer(x2d, p_all, layer, batch, seq, norm_g, w_in, w_out, ple_w, ple_gate_w):
    cos, sin = _rotary_tables(seq)
    pr = _ret_proj(x2d, norm_g, w_in.astype(BF16), cos, sin, seq)
    a = _retention(pr, _retention_tables(), batch, seq)
    return _out_proj(a, w_out, x2d, p_all, layer, ple_gate_w, ple_w)


def kernel(x, p, norm_g, nsa_w_in, nsa_q_g, nsa_kc_g, nsa_ks_g, nsa_kw_g, nsa_cmp_pos_k, nsa_cmp_pos_v, nsa_cmp_k_w1, nsa_cmp_k_w2, nsa_cmp_v_w1, nsa_cmp_v_w2, nsa_w_out, ret_w_in, ret_w_out, ple_w, ple_gate_w):
    batch, seq, d_model = x.shape
    depth = p.shape[0]
    n = batch * seq
    x2d = x.reshape(n, d_model)
    p_all = p.reshape(depth, n, PLE_DIM)
    for i in range(depth):
        j = i // 2
        if i % 2 == 0:
            x2d = _nsa_layer(x2d, p_all, i, batch, seq, norm_g[i], nsa_w_in[j], nsa_q_g[j], nsa_kc_g[j],
                             nsa_ks_g[j], nsa_kw_g[j], nsa_cmp_pos_k[j], nsa_cmp_pos_v[j],
                             nsa_cmp_k_w1[j], nsa_cmp_k_w2[j], nsa_cmp_v_w1[j], nsa_cmp_v_w2[j],
                             nsa_w_out[j], ple_w[i], ple_gate_w[i])
        else:
            x2d = _ret_layer(x2d, p_all, i, batch, seq, norm_g[i], ret_w_in[j], ret_w_out[j],
                             ple_w[i], ple_gate_w[i])
    return x2d.reshape(batch, seq, d_model)
```

```python
import functools

import numpy as np
import jax
import jax.numpy as jnp
from jax import lax
from jax.experimental import pallas as pl
from jax.experimental.pallas import tpu as pltpu

F32 = jnp.float32
BF16 = jnp.bfloat16

D_MODEL = 1024
PLE_DIM = 256
RMS_EPS = 1e-6
GN_EPS = 1e-5

NSA_HEADS = 16
NSA_DK = 64
NSA_G = 4
NSA_HPG = 4
CMP_BLOCK = 32
CMP_STRIDE = 16
CMP_HIDDEN = 256
SEL_BLOCK = 64
SEL_TOPN = 16
WINDOW = 512
FORCE_BONUS = 1e4
NEG = -1e30
SEL_MASK_BIAS = -30000.0

RET_HEADS = 4
RET_DK = 256
RET_DV = 512
RET_CHUNK = 512
ROPE_BASE = 10000.0

LANES = 128
VMEM_LIMIT = 56 * 1024 * 1024

ROW_TILE = 512
GROUP_W = NSA_HPG * NSA_DK
GATE_ROWS = 16
Q_TILE = 256
Q_TILES_PER_STEP = 4
SEL_KEYS = 512
LOG2E = 1.4426950408889634
MAX_SAFE_BOUND = 50.0
V_ROWS = NSA_DK + 16


def _dot(a, b):
    return jnp.dot(a, b, preferred_element_type=F32)


def _dot_nt(a, b):
    return lax.dot_general(a, b, (((1,), (1,)), ((), ())), preferred_element_type=F32)


def _silu(x):
    return x * jax.nn.sigmoid(x)


def _const_spec(shape):
    nd = len(shape)
    return pl.BlockSpec(shape, lambda *_: (0,) * nd)


def _ret_proj_kernel(x_ref, g_ref, w_ref, cos_ref, sin_ref, o_ref):
    x = x_ref[...]
    h = (x * lax.rsqrt(jnp.mean(x * x, axis=-1, keepdims=True) + RMS_EPS) * g_ref[...]).astype(BF16)
    cos, sin = cos_ref[...], sin_ref[...]
    half = RET_DK // 2
    qk_w = RET_HEADS * RET_DK
    for t in range(2 * RET_HEADS):
        c0 = t * RET_DK
        acc = _dot(h, w_ref[:, c0:c0 + RET_DK])
        a1, a2 = acc[:, :half], acc[:, half:]
        rot = jnp.concatenate([a1 * cos - a2 * sin, a1 * sin + a2 * cos], axis=-1)
        if c0 >= qk_w:
            rot = rot * RET_DK ** -0.5
        o_ref[:, c0:c0 + RET_DK] = rot.astype(o_ref.dtype)
    for c0 in range(2 * qk_w, w_ref.shape[1], RET_DV):
        o_ref[:, c0:c0 + RET_DV] = _dot(h, w_ref[:, c0:c0 + RET_DV]).astype(o_ref.dtype)


def _ret_proj(x2d, norm_g, w, cos, sin, seq):
    n = x2d.shape[0]
    cols = w.shape[1]
    n_pos_blocks = seq // ROW_TILE
    return pl.pallas_call(
        _ret_proj_kernel,
        grid=(n // ROW_TILE,),
        in_specs=[
            pl.BlockSpec((ROW_TILE, D_MODEL), lambda i: (i, 0)),
            _const_spec((1, D_MODEL)),
            _const_spec((D_MODEL, cols)),
            pl.BlockSpec((ROW_TILE, LANES), lambda i: (i % n_pos_blocks, 0)),
            pl.BlockSpec((ROW_TILE, LANES), lambda i: (i % n_pos_blocks, 0)),
        ],
        out_specs=pl.BlockSpec((ROW_TILE, cols), lambda i: (i, 0)),
        out_shape=jax.ShapeDtypeStruct((n, cols), BF16),
        compiler_params=pltpu.CompilerParams(
            dimension_semantics=("arbitrary",), vmem_limit_bytes=VMEM_LIMIT),
        name="ret_norm_proj",
    )(x2d, norm_g.reshape(1, D_MODEL), w, cos, sin)


def _nsa_proj_kernel(x_ref, g_ref, w_ref, hg_ref, bd_ref, augs_ref, augw_ref, pad_ref,
                     q_ref, z_ref, kcs_ref, vcs_ref, ks_ref, kw_ref, vst_ref, vwt_ref, glt_ref, cmp_scr, *, offs):
    s = pl.program_id(1)
    tiles = ROW_TILE // Q_TILE

    @pl.when(s == 0)
    def _():
        for g in range(NSA_G):
            kw_ref[0, g] = pad_ref[...]
        vwt_ref[...] = jnp.zeros_like(vwt_ref)

    @pl.when(s > 0)
    def _():
        x = x_ref[...]
        h = (x * lax.rsqrt(jnp.mean(x * x, axis=-1, keepdims=True) + RMS_EPS) * g_ref[...]).astype(BF16)
        low = lax.broadcasted_iota(jnp.int32, (ROW_TILE, LANES), 1) < NSA_DK

        def proj(c0, width=GROUP_W):
            return _dot(h, w_ref[:, c0:c0 + width])

        def head_norm(acc, c0):
            ss = _dot((acc * acc).astype(BF16), bd_ref[...])
            return acc * lax.rsqrt(ss * (1.0 / NSA_DK) + RMS_EPS) * hg_ref[:, c0:c0 + GROUP_W]

        kvw = NSA_G * NSA_DK
        ones = jnp.ones((V_ROWS - NSA_DK, Q_TILE), BF16)
        for name, aug_ref, k_ref, v_ref in (("ks", augs_ref, ks_ref, vst_ref), ("kw", augw_ref, kw_ref, vwt_ref)):
            kv = proj(offs[name], 2 * kvw)
            k = head_norm(kv[:, :kvw], offs[name])
            aug = aug_ref[...].astype(F32)
            for pair in range(NSA_G // 2):
                two = k[:, LANES * pair:LANES * (pair + 1)]
                k_ref[0, 2 * pair] = jnp.where(low, two, aug).astype(BF16)
                k_ref[0, 2 * pair + 1] = jnp.where(low, pltpu.roll(two, NSA_DK, 1), aug).astype(BF16)
            vt = kv[:, kvw:].T
            for g in range(NSA_G):
                for j in range(tiles):
                    v_ref[0, g, j, 0:NSA_DK, :] = vt[g * NSA_DK:(g + 1) * NSA_DK,
                                                     j * Q_TILE:(j + 1) * Q_TILE].astype(BF16)
                    v_ref[0, g, j, NSA_DK:V_ROWS, :] = ones

        cmp_in = proj(offs["kc"], 2 * kvw)
        for t in range(2 * kvw // LANES):
            cmp_scr[t] = cmp_in[:, LANES * t:LANES * (t + 1)]
        n_sub = ROW_TILE // CMP_STRIDE
        low_sub = lax.broadcasted_iota(jnp.int32, (n_sub, LANES), 1) < NSA_DK
        for r in range(0, CMP_STRIDE, 2):
            for i, o_ref in enumerate((kcs_ref, vcs_ref)):
                for pair in range(NSA_G // 2):
                    t = i * (NSA_G // 2) + pair
                    e2 = cmp_scr[t, pl.ds(r, n_sub, stride=CMP_STRIDE), :]
                    o2 = cmp_scr[t, pl.ds(r + 1, n_sub, stride=CMP_STRIDE), :]
                    lanes = slice(LANES * (r // 2), LANES * (r // 2 + 1))
                    o_ref[0, 2 * pair, :, lanes] = jnp.where(low_sub, e2, pltpu.roll(o2, NSA_DK, 1)).astype(BF16)
                    o_ref[0, 2 * pair + 1, :, lanes] = jnp.where(low_sub, pltpu.roll(e2, NSA_DK, 1), o2).astype(BF16)

        glt = proj(offs["gl"], LANES).T
        for g in range(NSA_G):
            glt_ref[0, g] = glt[GATE_ROWS * g:GATE_ROWS * (g + 1)]

        for t in range(NSA_G):
            lanes = slice(GROUP_W * t, GROUP_W * (t + 1))
            c0 = offs["q"] + GROUP_W * t
            q_ref[:, lanes] = head_norm(proj(c0), c0).astype(BF16)
            z_ref[:, lanes] = proj(offs["z"] + GROUP_W * t).astype(BF16)


def _nsa_proj(x2d, norm_g, w, head_gain, aug_sel, aug_win, pad_rows, offs, batch, seq):
    n = batch * seq
    nsb = seq // ROW_TILE
    tiles = ROW_TILE // Q_TILE
    qw = NSA_HEADS * NSA_DK
    kvw = NSA_G * NSA_DK
    sub_w = CMP_STRIDE * NSA_DK
    lane_head = np.arange(GROUP_W) // NSA_DK
    bd = jnp.asarray(lane_head[:, None] == lane_head[None, :], BF16)
    prev = lambda s: jnp.maximum(s - 1, 0)
    rows = lambda b, s: (b * nsb + prev(s), 0)
    out_shape = [
        jax.ShapeDtypeStruct((n, qw), BF16), jax.ShapeDtypeStruct((n, qw), BF16),
        jax.ShapeDtypeStruct((batch, NSA_G, seq // CMP_STRIDE, sub_w), BF16),
        jax.ShapeDtypeStruct((batch, NSA_G, seq // CMP_STRIDE, sub_w), BF16),
        jax.ShapeDtypeStruct((batch, NSA_G, seq, LANES), BF16),
        jax.ShapeDtypeStruct((batch, NSA_G, seq + WINDOW, LANES), BF16),
        jax.ShapeDtypeStruct((batch, NSA_G, seq // Q_TILE, V_ROWS, Q_TILE), BF16),
        jax.ShapeDtypeStruct((batch, NSA_G, (seq + WINDOW) // Q_TILE, V_ROWS, Q_TILE), BF16),
        jax.ShapeDtypeStruct((batch, NSA_G, GATE_ROWS, seq), F32),
    ]
    out_specs = [
        pl.BlockSpec((ROW_TILE, qw), rows), pl.BlockSpec((ROW_TILE, qw), rows),
        pl.BlockSpec((1, NSA_G, ROW_TILE // CMP_STRIDE, sub_w), lambda b, s: (b, 0, prev(s), 0)),
        pl.BlockSpec((1, NSA_G, ROW_TILE // CMP_STRIDE, sub_w), lambda b, s: (b, 0, prev(s), 0)),
        pl.BlockSpec((1, NSA_G, ROW_TILE, LANES), lambda b, s: (b, 0, prev(s), 0)),
        pl.BlockSpec((1, NSA_G, ROW_TILE, LANES), lambda b, s: (b, 0, s, 0)),
        pl.BlockSpec((1, NSA_G, tiles, V_ROWS, Q_TILE), lambda b, s: (b, 0, prev(s), 0, 0)),
        pl.BlockSpec((1, NSA_G, tiles, V_ROWS, Q_TILE), lambda b, s: (b, 0, s, 0, 0)),
        pl.BlockSpec((1, NSA_G, GATE_ROWS, ROW_TILE), lambda b, s: (b, 0, 0, prev(s))),
    ]
    assert WINDOW == ROW_TILE
    return pl.pallas_call(
        functools.partial(_nsa_proj_kernel, offs=offs),
        grid=(batch, nsb + 1),
        in_specs=[
            pl.BlockSpec((ROW_TILE, D_MODEL), rows),
            _const_spec((1, D_MODEL)),
            _const_spec(w.shape),
            _const_spec(head_gain.shape),
            _const_spec(bd.shape),
            pl.BlockSpec((ROW_TILE, LANES), lambda b, s: (prev(s), 0)),
            pl.BlockSpec((ROW_TILE, LANES), lambda b, s: (prev(s), 0)),
            _const_spec(pad_rows.shape),
        ],
        out_specs=out_specs,
        out_shape=out_shape,
        scratch_shapes=[pltpu.VMEM((2 * kvw // LANES, ROW_TILE, LANES), F32)],
        compiler_params=pltpu.CompilerParams(
            dimension_semantics=("arbitrary", "arbitrary"), vmem_limit_bytes=VMEM_LIMIT),
        name="nsa_norm_proj",
    )(x2d, norm_g.reshape(1, D_MODEL), w, head_gain, bd, aug_sel, aug_win, pad_rows)


def _out_kernel(a_ref, wo_ref, x_ref, p_ref, wg_ref, wp_ref, o_ref):
    x1 = x_ref[...] + _dot(a_ref[...], wo_ref[...])
    gate = jax.nn.sigmoid(_dot(x1.astype(BF16), wg_ref[...]))
    emb = _dot(p_ref[0].astype(BF16), wp_ref[...])
    o_ref[...] = x1 + gate * emb


def _out_proj(a, w_out, x2d, p_all, layer, w_gate, w_ple):
    n, k = a.shape
    return pl.pallas_call(
        _out_kernel,
        grid=(n // ROW_TILE,),
        in_specs=[
            pl.BlockSpec((ROW_TILE, k), lambda i: (i, 0)),
            _const_spec((k, D_MODEL)),
            pl.BlockSpec((ROW_TILE, D_MODEL), lambda i: (i, 0)),
            pl.BlockSpec((1, ROW_TILE, PLE_DIM), lambda i: (layer, i, 0)),
            _const_spec((D_MODEL, D_MODEL)),
            _const_spec((PLE_DIM, D_MODEL)),
        ],
        out_specs=pl.BlockSpec((ROW_TILE, D_MODEL), lambda i: (i, 0)),
        out_shape=jax.ShapeDtypeStruct((n, D_MODEL), F32),
        compiler_params=pltpu.CompilerParams(
            dimension_semantics=("arbitrary",), vmem_limit_bytes=VMEM_LIMIT),
        name="out_proj_ple",
    )(a, w_out.astype(BF16), x2d, p_all, w_gate.astype(BF16), w_ple.astype(BF16))


def _cmp_kernel(xk_ref, xv_ref, w1k_ref, w1v_ref, posk_ref, posv_ref, w2k_ref, w2v_ref,
                gk_ref, shift_ref, kc_ref, vc_ref):
    n_sub = xk_ref.shape[2]
    half = xk_ref.shape[3]
    for is_k, x_ref, w1_ref, pos_ref, w2_ref, o_ref in (
            (True, xk_ref, w1k_ref, posk_ref, w2k_ref, kc_ref),
            (False, xv_ref, w1v_ref, posv_ref, w2v_ref, vc_ref)):
        w1 = w1_ref[...]
        x = x_ref[0].reshape(NSA_G * n_sub, half)
        ab = _dot(x, w1)
        pos = pos_ref[...]
        pos_term = (_dot(pos[:, :half], w1[:, :CMP_HIDDEN])
                    + _dot(pos[:, half:], w1[:, CMP_HIDDEN:]))[0:1]
        for g in range(NSA_G):
            first = ab[g * n_sub:(g + 1) * n_sub, :CMP_HIDDEN]
            second = ab[g * n_sub:(g + 1) * n_sub, CMP_HIDDEN:]
            hid = _silu(first + pltpu.roll(second, n_sub - 1, 0) + pos_term)
            c = _dot(hid.astype(BF16), w2_ref[...])
            if is_k:
                ss = jnp.sum(c * c, axis=-1, keepdims=True) * (1.0 / NSA_DK)
                c = c * lax.rsqrt(ss + RMS_EPS) * gk_ref[...] + shift_ref[...]
                o_ref[0, g] = c.astype(BF16)
            else:
                ct = jnp.concatenate([c.T[:NSA_DK], jnp.ones((V_ROWS - NSA_DK, n_sub), F32)], axis=0)
                o_ref[0, g] = ct.astype(BF16)


def _compress(xk, xv, w1k, w1v, posk, posv, w2k, w2v, gk, k_shift):
    b, g, n_sub, half = xk.shape
    blk = pl.BlockSpec((1, g, n_sub, half), lambda i: (i, 0, 0, 0))
    k_blk = pl.BlockSpec((1, g, n_sub, LANES), lambda i: (i, 0, 0, 0))
    k_sds = jax.ShapeDtypeStruct((b, g, n_sub, LANES), BF16)
    v_blk = pl.BlockSpec((1, g, V_ROWS, n_sub), lambda i: (i, 0, 0, 0))
    v_sds = jax.ShapeDtypeStruct((b, g, V_ROWS, n_sub), BF16)
    return pl.pallas_call(
        _cmp_kernel,
        grid=(b,),
        in_specs=[blk, blk,
                  _const_spec(w1k.shape), _const_spec(w1v.shape),
                  _const_spec(posk.shape), _const_spec(posv.shape),
                  _const_spec(w2k.shape), _const_spec(w2v.shape),
                  _const_spec(gk.shape), _const_spec(k_shift.shape)],
        out_specs=[k_blk, v_blk],
        out_shape=[k_sds, v_sds],
        compiler_params=pltpu.CompilerParams(
            dimension_semantics=("arbitrary",), vmem_limit_bytes=VMEM_LIMIT),
        name="nsa_compress",
    )(xk, xv, w1k, w1v, posk, posv, w2k, w2v, gk, k_shift)


def _nsa_attn_tile(sub, fast_ref, q_ref, glt_ref, z_ref, ks_ref, vst_ref, kw_ref, vwt_ref, kc_ref, vct_ref,
                   ovt_ref, cthr_ref, kmq_ref, o_ref, acc_ref, accw_ref, rank_ref, p0_ref, p1_ref, ac_ref, impt_ref):
    tq = Q_TILE
    cols = NSA_HPG * tq
    qi = pl.program_id(2) * Q_TILES_PER_STEP + sub
    t0 = qi * tq
    rows = slice(sub * tq, (sub + 1) * tq)
    n_sel = SEL_BLOCK
    dk = NSA_DK

    qt = q_ref[rows, :].astype(F32).T
    q_heads = [qt[h * dk:(h + 1) * dk] for h in range(NSA_HPG)]

    def stack_heads(extra_rows):
        return jnp.concatenate(
            [jnp.concatenate([qh, extra_rows], axis=0) for qh in q_heads], axis=1).astype(BF16)

    blk = lax.broadcasted_iota(jnp.int32, (n_sel, tq), 0)
    ones_row = jnp.where(blk == 0, 1.0, 0.0)
    fast = fast_ref[0] > 0

    def compressed(subtract_max):
        sc = _dot(kc_ref[0, 0], stack_heads(ones_row))
        sc = jnp.where(cthr_ref[...] <= t0, sc, NEG)
        if subtract_max:
            sc = sc - jnp.maximum(jnp.max(sc, axis=0, keepdims=True), 0.1 * NEG)
        ec = jnp.exp2(sc).astype(BF16)
        r = _dot(jnp.concatenate([vct_ref[0, 0], ovt_ref[...]], axis=0), ec)
        lc = r[dk:dk + 1]
        inv_lc = 1.0 / jnp.where(lc > 0.0, lc, 1.0)
        ac_ref[...] = r[:dk] * inv_lc
        imp = r[V_ROWS:] * inv_lc
        impt_ref[...] = imp[:, 0:tq] + imp[:, tq:2 * tq] + imp[:, 2 * tq:3 * tq] + imp[:, 3 * tq:4 * tq]

    pl.when(fast)(lambda: compressed(False))
    pl.when(jnp.logical_not(fast))(lambda: compressed(True))
    imp_t = impt_ref[...]

    cur = (t0 + lax.broadcasted_iota(jnp.int32, (n_sel, tq), 1)) // SEL_BLOCK
    forced = (blk == 0) | (blk == cur) | (blk == cur - 1)
    score = jnp.where(blk <= cur, imp_t + jnp.where(forced, FORCE_BONUS, 0.0), NEG)
    sub = 8
    chunks = [score[c * sub:(c + 1) * sub] for c in range(n_sel // sub)]
    blk_sub = lax.broadcasted_iota(jnp.int32, (sub, tq), 0)
    last_blk = (t0 + tq - 1) // SEL_BLOCK
    rank_ref[...] = jnp.zeros_like(rank_ref)
    for jb in range(n_sel // sub):
        @pl.when(jb * sub <= last_blk)
        def _():
            ranks = [jnp.zeros((sub, tq), F32) for _ in chunks]
            for j in range(jb * sub, (jb + 1) * sub):
                row = jnp.broadcast_to(chunks[jb][j - jb * sub:j - jb * sub + 1, :], (sub, tq))
                for c, chunk in enumerate(chunks):
                    if c > jb:
                        one = jnp.where(row >= chunk, 1.0, 0.0)
                    elif c < jb:
                        one = jnp.where(row > chunk, 1.0, 0.0)
                    else:
                        one = jnp.where(blk_sub > j - c * sub,
                                        jnp.where(row >= chunk, 1.0, 0.0), jnp.where(row > chunk, 1.0, 0.0))
                    ranks[c] = ranks[c] + one
            rank_ref[...] += jnp.concatenate(ranks, axis=0)
    bias_t = jnp.where(rank_ref[...] < float(SEL_TOPN), 0.0, SEL_MASK_BIAS)
    q_aug = stack_heads(jnp.where(blk == 0, 1.0, bias_t))

    tiles_per_step = SEL_KEYS // tq
    n_full = t0 // SEL_KEYS
    n_wt = WINDOW // tq + 1
    diag_keep = kmq_ref[0:tq, :] <= 0

    def sel_scores(step, tail):
        base = pl.multiple_of(step * SEL_KEYS, SEL_KEYS)
        s = _dot(ks_ref[0, 0, pl.ds(base, SEL_KEYS), :], q_aug)
        if tail:
            s = jnp.where(kmq_ref[...] <= t0 - base, s, NEG)
        vt = vst_ref[0, 0, pl.ds(step * tiles_per_step, tiles_per_step)]
        return s, jnp.concatenate([vt[i] for i in range(tiles_per_step)], axis=1)

    def win_scores():
        sw = _dot(kw_ref[0, 0, pl.ds(pl.multiple_of(t0, tq), n_wt * tq), :], q_aug)
        parts = [jnp.where(diag_keep, NEG, sw[:tq])]
        if WINDOW > tq:
            parts.append(sw[tq:WINDOW])
        parts.append(jnp.where(diag_keep, sw[WINDOW:], NEG))
        sw = jnp.concatenate(parts, axis=0)
        vwt = vwt_ref[0, 0, pl.ds(qi, n_wt)]
        return sw, jnp.concatenate([vwt[i] for i in range(n_wt)], axis=1)

    @pl.when(fast)
    def _():
        acc_ref[...] = jnp.zeros_like(acc_ref)

        def values(step):
            vt = vst_ref[0, 0, pl.ds(step * tiles_per_step, tiles_per_step)]
            return jnp.concatenate([vt[i] for i in range(tiles_per_step)], axis=1)

        def probs(step, tail):
            return jnp.exp2(sel_scores(step, tail)[0]).astype(BF16)

        odd = n_full % 2
        half = SEL_KEYS // 2
        assert half == tq

        @pl.when(t0 % SEL_KEYS != 0)
        def _():
            p1_ref[...] = probs(n_full, True)

        @pl.when(t0 % SEL_KEYS == 0)
        def _():
            s = _dot(ks_ref[0, 0, pl.ds(pl.multiple_of(t0, SEL_KEYS), half), :], q_aug)
            p1_ref[0:half, :] = jnp.exp2(jnp.where(diag_keep, s, NEG)).astype(BF16)
            p1_ref[half:, :] = jnp.zeros((half, cols), BF16)

        @pl.when(odd == 1)
        def _():
            p = probs(n_full - 1, False)
            acc_ref[...] += _dot(values(n_full), p1_ref[...])
            p1_ref[...] = p

        first_pending = n_full - odd

        def pair(j, carry):
            pending = jnp.where(j == 0, first_pending, 2 * j - 1)
            pa = probs(2 * j, False)
            acc_ref[...] += _dot(values(pending), p1_ref[...])
            p0_ref[...] = pa
            pb = probs(2 * j + 1, False)
            acc_ref[...] += _dot(values(2 * j), p0_ref[...])
            p1_ref[...] = pb
            return carry

        n_pairs = n_full // 2
        lax.fori_loop(0, n_pairs, pair, 0)
        pending = jnp.where(n_pairs == 0, first_pending, 2 * n_pairs - 1)
        sw, vwt = win_scores()
        pw = jnp.exp2(sw).astype(BF16)
        acc_ref[...] += _dot(values(pending), p1_ref[...])
        accw_ref[...] = _dot(vwt, pw)

    @pl.when(jnp.logical_not(fast))
    def _():
        acc_ref[...] = jnp.zeros_like(acc_ref)

        def online(s, vt, m):
            m_new = jnp.maximum(m, jnp.max(s, axis=0, keepdims=True))
            acc_ref[...] = jnp.exp2(m - m_new) * acc_ref[...] + _dot(vt, jnp.exp2(s - m_new).astype(BF16))
            return m_new

        m = jnp.full((1, cols), NEG, F32)
        m = lax.fori_loop(0, n_full, lambda i, m: online(*sel_scores(i, False), m), m)
        online(*sel_scores(n_full, True), m)
        sw, vwt = win_scores()
        accw_ref[...] = _dot(vwt, jnp.exp2(sw - jnp.max(sw, axis=0, keepdims=True)).astype(BF16))

    a_s = acc_ref[...]
    a_w = accw_ref[...]

    a_s = a_s[:dk] * (1.0 / a_s[dk:dk + 1])
    a_w = a_w[:dk] * (1.0 / a_w[dk:dk + 1])
    gates = jax.nn.sigmoid(glt_ref[0, 0, :, rows])
    outs = []
    for h in range(NSA_HPG):
        sl = slice(h * tq, (h + 1) * tq)
        outs.append(gates[3 * h:3 * h + 1] * ac_ref[:, sl] + gates[3 * h + 1:3 * h + 2] * a_s[:, sl]
                    + gates[3 * h + 2:3 * h + 3] * a_w[:, sl])
    out = jnp.concatenate(outs, axis=0).T
    o_ref[rows, :] = (out * _silu(z_ref[rows, :].astype(F32))).astype(o_ref.dtype)


def _nsa_attn_kernel(*refs):
    for sub in range(Q_TILES_PER_STEP):
        _nsa_attn_tile(sub, *refs)


def _nsa_attention(fast, q, z, glt, ks_aug, vst, kw_pad, vwt, kc_pad, vct, overlap_t, batch, seq):
    cols = NSA_HPG * Q_TILE
    query = np.arange(cols)[None, :] % Q_TILE
    cmp_thr = jnp.asarray(np.arange(kc_pad.shape[2])[:, None] * CMP_STRIDE + (CMP_BLOCK - 1) - query, jnp.int32)
    key_minus_query = jnp.asarray(np.arange(SEL_KEYS)[:, None] - query, jnp.int32)
    n = batch * seq
    step_q = Q_TILES_PER_STEP * Q_TILE
    n_q = seq // step_q
    gw = NSA_HPG * NSA_DK

    def per_group(arr):
        nd = arr.ndim - 2
        return pl.BlockSpec((1, 1) + arr.shape[2:], lambda b, g, i: (b, g) + (0,) * nd)

    return pl.pallas_call(
        _nsa_attn_kernel,
        grid=(batch, NSA_G, n_q),
        in_specs=[
            pl.BlockSpec(memory_space=pltpu.SMEM),
            pl.BlockSpec((step_q, gw), lambda b, g, i: (b * n_q + i, g)),
            pl.BlockSpec((1, 1, glt.shape[2], step_q), lambda b, g, i: (b, g, 0, i)),
            pl.BlockSpec((step_q, gw), lambda b, g, i: (b * n_q + i, g)),
            per_group(ks_aug), per_group(vst), per_group(kw_pad), per_group(vwt),
            per_group(kc_pad), per_group(vct),
            _const_spec(overlap_t.shape), _const_spec(cmp_thr.shape), _const_spec(key_minus_query.shape),
        ],
        out_specs=pl.BlockSpec((step_q, gw), lambda b, g, i: (b * n_q + i, g)),
        out_shape=jax.ShapeDtypeStruct((n, NSA_HEADS * NSA_DK), BF16),
        scratch_shapes=([pltpu.VMEM((V_ROWS, cols), F32)] * 2 + [pltpu.VMEM((SEL_BLOCK, Q_TILE), F32)]
                        + [pltpu.VMEM((SEL_KEYS, cols), BF16)] * 2
                        + [pltpu.VMEM((NSA_DK, cols), F32), pltpu.VMEM((SEL_BLOCK, Q_TILE), F32)]),
        compiler_params=pltpu.CompilerParams(
            dimension_semantics=("arbitrary", "arbitrary", "arbitrary"),
            vmem_limit_bytes=VMEM_LIMIT),
        name="nsa_attention",
    )(fast, q, glt, z, ks_aug, vst, kw_pad, vwt, kc_pad, vct, overlap_t, cmp_thr, key_minus_query)


def _ret_kernel(q_ref, k_ref, v_ref, z_ref, dec_ref, qd_ref, kd_ref, cd_ref, o_ref, st_ref):
    @pl.when(pl.program_id(1) == 0)
    def _():
        st_ref[...] = jnp.zeros_like(st_ref)

    for h in range(RET_HEADS):
        q = q_ref[:, h * RET_DK:(h + 1) * RET_DK]
        k = k_ref[:, h * RET_DK:(h + 1) * RET_DK]
        v = v_ref[:, h * RET_DV:(h + 1) * RET_DV]
        att = _dot_nt(q, k) * dec_ref[h]
        state = st_ref[h]
        o = _dot(att.astype(BF16), v) + _dot(q, state.astype(BF16)) * qd_ref[h]
        kd_t = (k.astype(F32) * kd_ref[h]).T.astype(BF16)
        st_ref[h] = state * cd_ref[h] + _dot(kd_t, v)
        mu = jnp.mean(o, axis=-1, keepdims=True)
        d = o - mu
        var = jnp.mean(d * d, axis=-1, keepdims=True)
        z = z_ref[:, h * RET_DV:(h + 1) * RET_DV].astype(F32)
        o_ref[:, h * RET_DV:(h + 1) * RET_DV] = (d * lax.rsqrt(var + GN_EPS) * _silu(z)).astype(o_ref.dtype)


def _retention(pr, tables, batch, seq):
    n = batch * seq
    n_c = seq // RET_CHUNK
    dec, qd, kd, cd = tables
    qk_w = RET_HEADS * RET_DK
    v_w = RET_HEADS * RET_DV
    row = lambda b, c: b * n_c + c
    return pl.pallas_call(
        _ret_kernel,
        grid=(batch, n_c),
        in_specs=[
            pl.BlockSpec((RET_CHUNK, qk_w), lambda b, c: (row(b, c), 0)),
            pl.BlockSpec((RET_CHUNK, qk_w), lambda b, c: (row(b, c), 1)),
            pl.BlockSpec((RET_CHUNK, v_w), lambda b, c: (row(b, c), 1)),
            pl.BlockSpec((RET_CHUNK, v_w), lambda b, c: (row(b, c), 2)),
            _const_spec(dec.shape), _const_spec(qd.shape), _const_spec(kd.shape), _const_spec(cd.shape),
        ],
        out_specs=pl.BlockSpec((RET_CHUNK, v_w), lambda b, c: (row(b, c), 0)),
        out_shape=jax.ShapeDtypeStruct((n, v_w), BF16),
        scratch_shapes=[pltpu.VMEM((RET_HEADS, RET_DK, RET_DV), F32)],
        compiler_params=pltpu.CompilerParams(
            dimension_semantics=("arbitrary", "arbitrary"), vmem_limit_bytes=VMEM_LIMIT),
        name="retention",
    )(pr, pr, pr, pr, dec, qd, kd, cd)


def _overlap_matrix_t(n_cmp_pad):
    i = np.arange(n_cmp_pad)[None, :]
    j = np.arange(SEL_BLOCK)[:, None]
    ov = (i * CMP_STRIDE < (j + 1) * SEL_BLOCK) & (i * CMP_STRIDE + CMP_BLOCK > j * SEL_BLOCK)
    return jnp.asarray(ov, BF16)


def _rotary_tables(seq):
    half = RET_DK // 2
    inv = ROPE_BASE ** (-jnp.linspace(0.0, 1.0, half, dtype=F32))
    ang = jnp.arange(seq, dtype=F32)[:, None] * inv[None, :]
    return jnp.cos(ang), jnp.sin(ang)


def _retention_tables():
    c = RET_CHUNK
    log_g = jnp.log(1.0 - 2.0 ** (-5.0 - jnp.arange(RET_HEADS, dtype=F32)))
    ix = jnp.arange(c, dtype=F32)
    diff = ix[:, None] - ix[None, :]
    dec = jnp.where(diff >= 0, jnp.exp(log_g[:, None, None] * jnp.maximum(diff, 0.0)), 0.0)
    qd = jnp.exp(log_g[:, None] * (ix + 1.0))[:, :, None]
    kd = jnp.exp(log_g[:, None] * (c - 1.0 - ix))[:, :, None]
    cd = jnp.broadcast_to(jnp.exp(log_g * c)[:, None, None], (RET_HEADS, 1, RET_DV))
    return dec, qd, kd, cd


def _nsa_layer(x2d, p_all, layer, batch, seq, norm_g, w_in, q_g, kc_g, ks_g, kw_g, pos_k, pos_v,
               ck_w1, ck_w2, cv_w1, cv_w2, w_out, ple_w, ple_gate_w):
    qw = NSA_HEADS * NSA_DK
    kvw = NSA_G * NSA_DK
    n_gate = 3 * NSA_HPG
    sizes = [qw] + [kvw] * 6 + [3 * NSA_HEADS, qw]
    splits = np.concatenate([[0], np.cumsum(sizes)])
    names = ["q", "kc", "vc", "ks", "vs", "kw", "vw"]
    offs = {nm: int(splits[i]) for i, nm in enumerate(names)}
    front = int(splits[7])
    offs["z"], offs["gl"] = front, front + qw
    wgl = w_in[:, splits[7]:splits[8]].reshape(D_MODEL, NSA_G, n_gate)
    wgl = jnp.pad(wgl, ((0, 0), (0, 0), (0, GATE_ROWS - n_gate))).reshape(D_MODEL, NSA_G * GATE_ROWS)
    wgl = jnp.pad(wgl, ((0, 0), (0, LANES - NSA_G * GATE_ROWS)))
    w = jnp.concatenate([w_in[:, :front], w_in[:, splits[8]:], wgl], axis=1).astype(BF16)
    zeros = lambda width: jnp.zeros((width,), F32)
    head_gain = jnp.concatenate([
        jnp.tile(q_g, NSA_HEADS) * (NSA_DK ** -0.5 * LOG2E), zeros(2 * kvw),
        jnp.tile(ks_g, NSA_G), zeros(kvw), jnp.tile(kw_g, NSA_G), zeros(w.shape[1] - offs["vw"])]).reshape(1, -1)

    q_norm = LOG2E * jnp.max(jnp.abs(q_g)) * 1.02
    bound_sel = (q_norm * NSA_DK ** 0.5 * jnp.max(jnp.abs(ks_g))).astype(BF16).astype(F32) * 1.01
    bound_win = (q_norm * NSA_DK ** 0.5 * jnp.max(jnp.abs(kw_g))).astype(BF16).astype(F32) * 1.01
    bound_cmp = (q_norm * NSA_DK ** 0.5 * jnp.max(jnp.abs(kc_g))).astype(BF16).astype(F32) * 1.01
    bound_max = jnp.maximum(jnp.maximum(bound_sel, bound_win), bound_cmp)
    fast = (bound_max <= MAX_SAFE_BOUND).astype(jnp.int32).reshape(1)

    blk_id = np.arange(seq)[:, None] // SEL_BLOCK
    upper = np.arange(LANES)[None, :] - NSA_DK
    onehot = jnp.asarray((blk_id == upper) & (upper > 0), F32)
    lane64 = jnp.asarray(upper == 0)
    aug_sel = jnp.where(lane64, -bound_sel, onehot).astype(BF16)
    aug_win = jnp.where(lane64, -bound_win, jnp.zeros((seq, LANES), F32)).astype(BF16)
    pad_rows = jnp.where(lane64, SEL_MASK_BIAS, jnp.zeros((WINDOW, LANES), F32)).astype(BF16)

    q, z, kc_sub, vc_sub, ks_aug, kw_pad, vst, vwt, glt = _nsa_proj(
        x2d, norm_g, w, head_gain, aug_sel, aug_win, pad_rows, offs, batch, seq)

    n_sub = seq // CMP_STRIDE
    half = CMP_STRIDE * NSA_DK
    w1cat = lambda w1: jnp.concatenate([w1[:half], w1[half:]], axis=1).astype(BF16)
    pos_flat = lambda pos: jnp.pad(pos.reshape(1, CMP_BLOCK * NSA_DK), ((0, 7), (0, 0))).astype(BF16)
    w2pad = lambda w2: jnp.pad(w2, ((0, 0), (0, LANES - NSA_DK))).astype(BF16)
    gk = jnp.pad(kc_g, (0, LANES - NSA_DK)).reshape(1, LANES)
    k_shift = jnp.where(lane64, -bound_cmp, 0.0)
    kc_pad, vct = _compress(kc_sub, vc_sub, w1cat(ck_w1), w1cat(cv_w1),
                            pos_flat(pos_k), pos_flat(pos_v), w2pad(ck_w2), w2pad(cv_w2), gk, k_shift)

    a = _nsa_attention(fast, q, z, glt, ks_aug, vst, kw_pad, vwt, kc_pad, vct,
                       _overlap_matrix_t(n_sub), batch, seq)
    return _out_proj(a, w_out, x2d, p_all, layer, ple_gate_w, ple_w)


def _ret_layer(x2d, p_all, layer, batch, seq, norm_g, w_in, w_out, ple_w, ple_gate_w):
    cos, sin = _rotary_tables(seq)
    pr = _ret_proj(x2d, norm_g, w_in.astype(BF16), cos, sin, seq)
    a = _retention(pr, _retention_tables(), batch, seq)
    return _out_proj(a, w_out, x2d, p_all, layer, ple_gate_w, ple_w)


def kernel(x, p, norm_g, nsa_w_in, nsa_q_g, nsa_kc_g, nsa_ks_g, nsa_kw_g, nsa_cmp_pos_k, nsa_cmp_pos_v, nsa_cmp_k_w1, nsa_cmp_k_w2, nsa_cmp_v_w1, nsa_cmp_v_w2, nsa_w_out, ret_w_in, ret_w_out, ple_w, ple_gate_w):
    batch, seq, d_model = x.shape
    depth = p.shape[0]
    n = batch * seq
    x2d = x.reshape(n, d_model)
    p_all = p.reshape(depth, n, PLE_DIM)
    for i in range(depth):
        j = i // 2
        if i % 2 == 0:
            x2d = _nsa_layer(x2d, p_all, i, batch, seq, norm_g[i], nsa_w_in[j], nsa_q_g[j], nsa_kc_g[j],
                             nsa_ks_g[j], nsa_kw_g[j], nsa_cmp_pos_k[j], nsa_cmp_pos_v[j],
                             nsa_cmp_k_w1[j], nsa_cmp_k_w2[j], nsa_cmp_v_w1[j], nsa_cmp_v_w2[j],
                             nsa_w_out[j], ple_w[i], ple_gate_w[i])
        else:
            x2d = _ret_layer(x2d, p_all, i, batch, seq, norm_g[i], ret_w_in[j], ret_w_out[j],
                             ple_w[i], ple_gate_w[i])
    return x2d.reshape(batch, seq, d_model)
```

```python
import functools

import numpy as np
import jax
import jax.numpy as jnp
from jax import lax
from jax.experimental import pallas as pl
from jax.experimental.pallas import tpu as pltpu

F32 = jnp.float32
BF16 = jnp.bfloat16

D_MODEL = 1024
PLE_DIM = 256
RMS_EPS = 1e-6
GN_EPS = 1e-5

NSA_HEADS = 16
NSA_DK = 64
NSA_G = 4
NSA_HPG = 4
CMP_BLOCK = 32
CMP_STRIDE = 16
CMP_HIDDEN = 256
SEL_BLOCK = 64
SEL_TOPN = 16
WINDOW = 512
FORCE_BONUS = 1e4
NEG = -1e30
SEL_MASK_BIAS = -30000.0

RET_HEADS = 4
RET_DK = 256
RET_DV = 512
RET_CHUNK = 512
ROPE_BASE = 10000.0

LANES = 128
VMEM_LIMIT = 56 * 1024 * 1024

ROW_TILE = 512
GROUP_W = NSA_HPG * NSA_DK
GATE_ROWS = 16
Q_TILE = 256
Q_TILES_PER_STEP = 4
SEL_KEYS = 512
LOG2E = 1.4426950408889634
MAX_SAFE_BOUND = 50.0
V_ROWS = NSA_DK + 16


def _dot(a, b):
    return jnp.dot(a, b, preferred_element_type=F32)


def _dot_nt(a, b):
    return lax.dot_general(a, b, (((1,), (1,)), ((), ())), preferred_element_type=F32)


def _silu(x):
    return x * jax.nn.sigmoid(x)


def _const_spec(shape):
    nd = len(shape)
    return pl.BlockSpec(shape, lambda *_: (0,) * nd)


def _ret_proj_kernel(x_ref, g_ref, w_ref, cos_ref, sin_ref, o_ref):
    x = x_ref[...]
    h = (x * lax.rsqrt(jnp.mean(x * x, axis=-1, keepdims=True) + RMS_EPS) * g_ref[...]).astype(BF16)
    cos, sin = cos_ref[...], sin_ref[...]
    half = RET_DK // 2
    qk_w = RET_HEADS * RET_DK
    for t in range(2 * RET_HEADS):
        c0 = t * RET_DK
        acc = _dot(h, w_ref[:, c0:c0 + RET_DK])
        a1, a2 = acc[:, :half], acc[:, half:]
        rot = jnp.concatenate([a1 * cos - a2 * sin, a1 * sin + a2 * cos], axis=-1)
        if c0 >= qk_w:
            rot = rot * RET_DK ** -0.5
        o_ref[:, c0:c0 + RET_DK] = rot.astype(o_ref.dtype)
    for c0 in range(2 * qk_w, w_ref.shape[1], RET_DV):
        o_ref[:, c0:c0 + RET_DV] = _dot(h, w_ref[:, c0:c0 + RET_DV]).astype(o_ref.dtype)


def _ret_proj(x2d, norm_g, w, cos, sin, seq):
    n = x2d.shape[0]
    cols = w.shape[1]
    n_pos_blocks = seq // ROW_TILE
    return pl.pallas_call(
        _ret_proj_kernel,
        grid=(n // ROW_TILE,),
        in_specs=[
            pl.BlockSpec((ROW_TILE, D_MODEL), lambda i: (i, 0)),
            _const_spec((1, D_MODEL)),
            _const_spec((D_MODEL, cols)),
            pl.BlockSpec((ROW_TILE, LANES), lambda i: (i % n_pos_blocks, 0)),
            pl.BlockSpec((ROW_TILE, LANES), lambda i: (i % n_pos_blocks, 0)),
        ],
        out_specs=pl.BlockSpec((ROW_TILE, cols), lambda i: (i, 0)),
        out_shape=jax.ShapeDtypeStruct((n, cols), BF16),
        compiler_params=pltpu.CompilerParams(
            dimension_semantics=("arbitrary",), vmem_limit_bytes=VMEM_LIMIT),
        name="ret_norm_proj",
    )(x2d, norm_g.reshape(1, D_MODEL), w, cos, sin)


def _nsa_proj_kernel(x_ref, g_ref, w_ref, hg_ref, bd_ref, augs_ref, augw_ref, pad_ref,
                     q_ref, z_ref, kcs_ref, vcs_ref, ks_ref, kw_ref, vst_ref, vwt_ref, glt_ref, cmp_scr, *, offs):
    s = pl.program_id(1)
    tiles = ROW_TILE // Q_TILE

    @pl.when(s == 0)
    def _():
        for g in range(NSA_G):
            kw_ref[0, g] = pad_ref[...]
        vwt_ref[...] = jnp.zeros_like(vwt_ref)

    @pl.when(s > 0)
    def _():
        x = x_ref[...]
        h = (x * lax.rsqrt(jnp.mean(x * x, axis=-1, keepdims=True) + RMS_EPS) * g_ref[...]).astype(BF16)
        low = lax.broadcasted_iota(jnp.int32, (ROW_TILE, LANES), 1) < NSA_DK

        def proj(c0, width=GROUP_W):
            return _dot(h, w_ref[:, c0:c0 + width])

        def head_norm(acc, c0):
            ss = _dot((acc * acc).astype(BF16), bd_ref[...])
            return acc * lax.rsqrt(ss * (1.0 / NSA_DK) + RMS_EPS) * hg_ref[:, c0:c0 + GROUP_W]

        kvw = NSA_G * NSA_DK
        ones = jnp.ones((V_ROWS - NSA_DK, Q_TILE), BF16)
        for name, aug_ref, k_ref, v_ref in (("ks", augs_ref, ks_ref, vst_ref), ("kw", augw_ref, kw_ref, vwt_ref)):
            kv = proj(offs[name], 2 * kvw)
            k = head_norm(kv[:, :kvw], offs[name])
            aug = aug_ref[...].astype(F32)
            for pair in range(NSA_G // 2):
                two = k[:, LANES * pair:LANES * (pair + 1)]
                k_ref[0, 2 * pair] = jnp.where(low, two, aug).astype(BF16)
                k_ref[0, 2 * pair + 1] = jnp.where(low, pltpu.roll(two, NSA_DK, 1), aug).astype(BF16)
            vt = kv[:, kvw:].T
            for g in range(NSA_G):
                for j in range(tiles):
                    v_ref[0, g, j, 0:NSA_DK, :] = vt[g * NSA_DK:(g + 1) * NSA_DK,
                                                     j * Q_TILE:(j + 1) * Q_TILE].astype(BF16)
                    v_ref[0, g, j, NSA_DK:V_ROWS, :] = ones

        cmp_in = proj(offs["kc"], 2 * kvw)
        for t in range(2 * kvw // LANES):
            cmp_scr[t] = cmp_in[:, LANES * t:LANES * (t + 1)]
        n_sub = ROW_TILE // CMP_STRIDE
        low_sub = lax.broadcasted_iota(jnp.int32, (n_sub, LANES), 1) < NSA_DK
        for r in range(0, CMP_STRIDE, 2):
            for i, o_ref in enumerate((kcs_ref, vcs_ref)):
                for pair in range(NSA_G // 2):
                    t = i * (NSA_G // 2) + pair
                    e2 = cmp_scr[t, pl.ds(r, n_sub, stride=CMP_STRIDE), :]
                    o2 = cmp_scr[t, pl.ds(r + 1, n_sub, stride=CMP_STRIDE), :]
                    lanes = slice(LANES * (r // 2), LANES * (r // 2 + 1))
                    o_ref[0, 2 * pair, :, lanes] = jnp.where(low_sub, e2, pltpu.roll(o2, NSA_DK, 1)).astype(BF16)
                    o_ref[0, 2 * pair + 1, :, lanes] = jnp.where(low_sub, pltpu.roll(e2, NSA_DK, 1), o2).astype(BF16)

        glt = proj(offs["gl"], LANES).T
        for g in range(NSA_G):
            glt_ref[0, g] = glt[GATE_ROWS * g:GATE_ROWS * (g + 1)]

        for t in range(NSA_G):
            lanes = slice(GROUP_W * t, GROUP_W * (t + 1))
            c0 = offs["q"] + GROUP_W * t
            q_ref[:, lanes] = head_norm(proj(c0), c0).astype(BF16)
            z_ref[:, lanes] = proj(offs["z"] + GROUP_W * t).astype(BF16)


def _nsa_proj(x2d, norm_g, w, head_gain, aug_sel, aug_win, pad_rows, offs, batch, seq):
    n = batch * seq
    nsb = seq // ROW_TILE
    tiles = ROW_TILE // Q_TILE
    qw = NSA_HEADS * NSA_DK
    kvw = NSA_G * NSA_DK
    sub_w = CMP_STRIDE * NSA_DK
    lane_head = np.arange(GROUP_W) // NSA_DK
    bd = jnp.asarray(lane_head[:, None] == lane_head[None, :], BF16)
    prev = lambda s: jnp.maximum(s - 1, 0)
    rows = lambda b, s: (b * nsb + prev(s), 0)
    out_shape = [
        jax.ShapeDtypeStruct((n, qw), BF16), jax.ShapeDtypeStruct((n, qw), BF16),
        jax.ShapeDtypeStruct((batch, NSA_G, seq // CMP_STRIDE, sub_w), BF16),
        jax.ShapeDtypeStruct((batch, NSA_G, seq // CMP_STRIDE, sub_w), BF16),
        jax.ShapeDtypeStruct((batch, NSA_G, seq, LANES), BF16),
        jax.ShapeDtypeStruct((batch, NSA_G, seq + WINDOW, LANES), BF16),
        jax.ShapeDtypeStruct((batch, NSA_G, seq // Q_TILE, V_ROWS, Q_TILE), BF16),
        jax.ShapeDtypeStruct((batch, NSA_G, (seq + WINDOW) // Q_TILE, V_ROWS, Q_TILE), BF16),
        jax.ShapeDtypeStruct((batch, NSA_G, GATE_ROWS, seq), F32),
    ]
    out_specs = [
        pl.BlockSpec((ROW_TILE, qw), rows), pl.BlockSpec((ROW_TILE, qw), rows),
        pl.BlockSpec((1, NSA_G, ROW_TILE // CMP_STRIDE, sub_w), lambda b, s: (b, 0, prev(s), 0)),
        pl.BlockSpec((1, NSA_G, ROW_TILE // CMP_STRIDE, sub_w), lambda b, s: (b, 0, prev(s), 0)),
        pl.BlockSpec((1, NSA_G, ROW_TILE, LANES), lambda b, s: (b, 0, prev(s), 0)),
        pl.BlockSpec((1, NSA_G, ROW_TILE, LANES), lambda b, s: (b, 0, s, 0)),
        pl.BlockSpec((1, NSA_G, tiles, V_ROWS, Q_TILE), lambda b, s: (b, 0, prev(s), 0, 0)),
        pl.BlockSpec((1, NSA_G, tiles, V_ROWS, Q_TILE), lambda b, s: (b, 0, s, 0, 0)),
        pl.BlockSpec((1, NSA_G, GATE_ROWS, ROW_TILE), lambda b, s: (b, 0, 0, prev(s))),
    ]
    assert WINDOW == ROW_TILE
    return pl.pallas_call(
        functools.partial(_nsa_proj_kernel, offs=offs),
        grid=(batch, nsb + 1),
        in_specs=[
            pl.BlockSpec((ROW_TILE, D_MODEL), rows),
            _const_spec((1, D_MODEL)),
            _const_spec(w.shape),
            _const_spec(head_gain.shape),
            _const_spec(bd.shape),
            pl.BlockSpec((ROW_TILE, LANES), lambda b, s: (prev(s), 0)),
            pl.BlockSpec((ROW_TILE, LANES), lambda b, s: (prev(s), 0)),
            _const_spec(pad_rows.shape),
        ],
        out_specs=out_specs,
        out_shape=out_shape,
        scratch_shapes=[pltpu.VMEM((2 * kvw // LANES, ROW_TILE, LANES), F32)],
        compiler_params=pltpu.CompilerParams(
            dimension_semantics=("arbitrary", "arbitrary"), vmem_limit_bytes=VMEM_LIMIT),
        name="nsa_norm_proj",
    )(x2d, norm_g.reshape(1, D_MODEL), w, head_gain, bd, aug_sel, aug_win, pad_rows)


def _out_kernel(a_ref, wo_ref, x_ref, p_ref, wg_ref, wp_ref, o_ref):
    x1 = x_ref[...] + _dot(a_ref[...], wo_ref[...])
    gate = jax.nn.sigmoid(_dot(x1.astype(BF16), wg_ref[...]))
    emb = _dot(p_ref[0].astype(BF16), wp_ref[...])
    o_ref[...] = x1 + gate * emb


def _out_proj(a, w_out, x2d, p_all, layer, w_gate, w_ple):
    n, k = a.shape
    return pl.pallas_call(
        _out_kernel,
        grid=(n // ROW_TILE,),
        in_specs=[
            pl.BlockSpec((ROW_TILE, k), lambda i: (i, 0)),
            _const_spec((k, D_MODEL)),
            pl.BlockSpec((ROW_TILE, D_MODEL), lambda i: (i, 0)),
            pl.BlockSpec((1, ROW_TILE, PLE_DIM), lambda i: (layer, i, 0)),
            _const_spec((D_MODEL, D_MODEL)),
            _const_spec((PLE_DIM, D_MODEL)),
        ],
        out_specs=pl.BlockSpec((ROW_TILE, D_MODEL), lambda i: (i, 0)),
        out_shape=jax.ShapeDtypeStruct((n, D_MODEL), F32),
        compiler_params=pltpu.CompilerParams(
            dimension_semantics=("arbitrary",), vmem_limit_bytes=VMEM_LIMIT),
        name="out_proj_ple",
    )(a, w_out.astype(BF16), x2d, p_all, w_gate.astype(BF16), w_ple.astype(BF16))


def _cmp_kernel(xk_ref, xv_ref, w1k_ref, w1v_ref, posk_ref, posv_ref, w2k_ref, w2v_ref,
                gk_ref, shift_ref, kc_ref, vc_ref):
    n_sub = xk_ref.shape[2]
    half = xk_ref.shape[3]
    for is_k, x_ref, w1_ref, pos_ref, w2_ref, o_ref in (
            (True, xk_ref, w1k_ref, posk_ref, w2k_ref, kc_ref),
            (False, xv_ref, w1v_ref, posv_ref, w2v_ref, vc_ref)):
        w1 = w1_ref[...]
        x = x_ref[0].reshape(NSA_G * n_sub, half)
        ab = _dot(x, w1)
        pos = pos_ref[...]
        pos_term = (_dot(pos[:, :half], w1[:, :CMP_HIDDEN])
                    + _dot(pos[:, half:], w1[:, CMP_HIDDEN:]))[0:1]
        for g in range(NSA_G):
            first = ab[g * n_sub:(g + 1) * n_sub, :CMP_HIDDEN]
            second = ab[g * n_sub:(g + 1) * n_sub, CMP_HIDDEN:]
            hid = _silu(first + pltpu.roll(second, n_sub - 1, 0) + pos_term)
            c = _dot(hid.astype(BF16), w2_ref[...])
            if is_k:
                ss = jnp.sum(c * c, axis=-1, keepdims=True) * (1.0 / NSA_DK)
                c = c * lax.rsqrt(ss + RMS_EPS) * gk_ref[...] + shift_ref[...]
                o_ref[0, g] = c.astype(BF16)
            else:
                ct = jnp.concatenate([c.T[:NSA_DK], jnp.ones((V_ROWS - NSA_DK, n_sub), F32)], axis=0)
                o_ref[0, g] = ct.astype(BF16)


def _compress(xk, xv, w1k, w1v, posk, posv, w2k, w2v, gk, k_shift):
    b, g, n_sub, half = xk.shape
    blk = pl.BlockSpec((1, g, n_sub, half), lambda i: (i, 0, 0, 0))
    k_blk = pl.BlockSpec((1, g, n_sub, LANES), lambda i: (i, 0, 0, 0))
    k_sds = jax.ShapeDtypeStruct((b, g, n_sub, LANES), BF16)
    v_blk = pl.BlockSpec((1, g, V_ROWS, n_sub), lambda i: (i, 0, 0, 0))
    v_sds = jax.ShapeDtypeStruct((b, g, V_ROWS, n_sub), BF16)
    return pl.pallas_call(
        _cmp_kernel,
        grid=(b,),
        in_specs=[blk, blk,
                  _const_spec(w1k.shape), _const_spec(w1v.shape),
                  _const_spec(posk.shape), _const_spec(posv.shape),
                  _const_spec(w2k.shape), _const_spec(w2v.shape),
                  _const_spec(gk.shape), _const_spec(k_shift.shape)],
        out_specs=[k_blk, v_blk],
        out_shape=[k_sds, v_sds],
        compiler_params=pltpu.CompilerParams(
            dimension_semantics=("arbitrary",), vmem_limit_bytes=VMEM_LIMIT),
        name="nsa_compress",
    )(xk, xv, w1k, w1v, posk, posv, w2k, w2v, gk, k_shift)


def _nsa_attn_tile(sub, fast_ref, q_ref, glt_ref, z_ref, ks_ref, vst_ref, kw_ref, vwt_ref, kc_ref, vct_ref,
                   ovt_ref, cthr_ref, kmq_ref, o_ref, acc_ref, accw_ref, rank_ref, p0_ref, p1_ref, ac_ref, impt_ref):
    tq = Q_TILE
    cols = NSA_HPG * tq
    qi = pl.program_id(2) * Q_TILES_PER_STEP + sub
    t0 = qi * tq
    rows = slice(sub * tq, (sub + 1) * tq)
    n_sel = SEL_BLOCK
    dk = NSA_DK

    qt = q_ref[rows, :].astype(F32).T
    q_heads = [qt[h * dk:(h + 1) * dk] for h in range(NSA_HPG)]

    def stack_heads(extra_rows):
        return jnp.concatenate(
            [jnp.concatenate([qh, extra_rows], axis=0) for qh in q_heads], axis=1).astype(BF16)

    blk = lax.broadcasted_iota(jnp.int32, (n_sel, tq), 0)
    ones_row = jnp.where(blk == 0, 1.0, 0.0)
    fast = fast_ref[0] > 0

    def compressed(subtract_max, n_rows):
        sc = _dot(kc_ref[0, 0, 0:n_rows, :], stack_heads(ones_row))
        sc = jnp.where(cthr_ref[0:n_rows, :] <= t0, sc, NEG)
        if subtract_max:
            sc = sc - jnp.maximum(jnp.max(sc, axis=0, keepdims=True), 0.1 * NEG)
        ec = jnp.exp2(sc).astype(BF16)
        lhs = jnp.concatenate([vct_ref[0, 0, :, 0:n_rows], ovt_ref[:, 0:n_rows]], axis=0)
        r = _dot(lhs, ec)
        lc = r[dk:dk + 1]
        inv_lc = 1.0 / jnp.where(lc > 0.0, lc, 1.0)
        ac_ref[...] = r[:dk] * inv_lc
        imp = r[V_ROWS:] * inv_lc
        impt_ref[...] = imp[:, 0:tq] + imp[:, tq:2 * tq] + imp[:, 2 * tq:3 * tq] + imp[:, 3 * tq:4 * tq]

    n_cmp = kc_ref.shape[2]
    few = (n_cmp // 2) * CMP_STRIDE + CMP_BLOCK - 1 > t0 + tq - 1
    for subtract_max, path in ((False, fast), (True, jnp.logical_not(fast))):
        pl.when(path & few)(functools.partial(compressed, subtract_max, n_cmp // 2))
        pl.when(path & jnp.logical_not(few))(functools.partial(compressed, subtract_max, n_cmp))
    imp_t = impt_ref[...]

    cur = (t0 + lax.broadcasted_iota(jnp.int32, (n_sel, tq), 1)) // SEL_BLOCK
    forced = (blk == 0) | (blk == cur) | (blk == cur - 1)
    score = jnp.where(blk <= cur, imp_t + jnp.where(forced, FORCE_BONUS, 0.0), NEG)
    sub = 8
    chunks = [score[c * sub:(c + 1) * sub] for c in range(n_sel // sub)]
    blk_sub = lax.broadcasted_iota(jnp.int32, (sub, tq), 0)
    last_blk = (t0 + tq - 1) // SEL_BLOCK
    rank_ref[...] = jnp.zeros_like(rank_ref)
    for jb in range(n_sel // sub):
        @pl.when(jb * sub <= last_blk)
        def _():
            ranks = [jnp.zeros((sub, tq), F32) for _ in chunks]
            for j in range(jb * sub, (jb + 1) * sub):
                row = jnp.broadcast_to(chunks[jb][j - jb * sub:j - jb * sub + 1, :], (sub, tq))
                for c, chunk in enumerate(chunks):
                    if c > jb:
                        one = jnp.where(row >= chunk, 1.0, 0.0)
                    elif c < jb:
                        one = jnp.where(row > chunk, 1.0, 0.0)
                    else:
                        one = jnp.where(blk_sub > j - c * sub,
                                        jnp.where(row >= chunk, 1.0, 0.0), jnp.where(row > chunk, 1.0, 0.0))
                    ranks[c] = ranks[c] + one
            rank_ref[...] += jnp.concatenate(ranks, axis=0)
    bias_t = jnp.where(rank_ref[...] < float(SEL_TOPN), 0.0, SEL_MASK_BIAS)
    q_aug = stack_heads(jnp.where(blk == 0, 1.0, bias_t))

    tiles_per_step = SEL_KEYS // tq
    n_full = t0 // SEL_KEYS
    n_wt = WINDOW // tq + 1
    diag_keep = kmq_ref[0:tq, :] <= 0

    def sel_scores(step, tail):
        base = pl.multiple_of(step * SEL_KEYS, SEL_KEYS)
        s = _dot(ks_ref[0, 0, pl.ds(base, SEL_KEYS), :], q_aug)
        if tail:
            s = jnp.where(kmq_ref[...] <= t0 - base, s, NEG)
        vt = vst_ref[0, 0, pl.ds(step * tiles_per_step, tiles_per_step)]
        return s, jnp.concatenate([vt[i] for i in range(tiles_per_step)], axis=1)

    def win_scores():
        sw = _dot(kw_ref[0, 0, pl.ds(pl.multiple_of(t0, tq), n_wt * tq), :], q_aug)
        parts = [jnp.where(diag_keep, NEG, sw[:tq])]
        if WINDOW > tq:
            parts.append(sw[tq:WINDOW])
        parts.append(jnp.where(diag_keep, sw[WINDOW:], NEG))
        sw = jnp.concatenate(parts, axis=0)
        vwt = vwt_ref[0, 0, pl.ds(qi, n_wt)]
        return sw, jnp.concatenate([vwt[i] for i in range(n_wt)], axis=1)

    @pl.when(fast)
    def _():
        acc_ref[...] = jnp.zeros_like(acc_ref)

        def values(step):
            vt = vst_ref[0, 0, pl.ds(step * tiles_per_step, tiles_per_step)]
            return jnp.concatenate([vt[i] for i in range(tiles_per_step)], axis=1)

        def probs(step, tail):
            return jnp.exp2(sel_scores(step, tail)[0]).astype(BF16)

        odd = n_full % 2
        half = SEL_KEYS // 2
        assert half == tq

        @pl.when(t0 % SEL_KEYS != 0)
        def _():
            p1_ref[...] = probs(n_full, True)

        @pl.when(t0 % SEL_KEYS == 0)
        def _():
            s = _dot(ks_ref[0, 0, pl.ds(pl.multiple_of(t0, SEL_KEYS), half), :], q_aug)
            p1_ref[0:half, :] = jnp.exp2(jnp.where(diag_keep, s, NEG)).astype(BF16)
            p1_ref[half:, :] = jnp.zeros((half, cols), BF16)

        @pl.when(odd == 1)
        def _():
            p = probs(n_full - 1, False)
            acc_ref[...] += _dot(values(n_full), p1_ref[...])
            p1_ref[...] = p

        first_pending = n_full - odd

        def pair(j, carry):
            pending = jnp.where(j == 0, first_pending, 2 * j - 1)
            pa = probs(2 * j, False)
            acc_ref[...] += _dot(values(pending), p1_ref[...])
            p0_ref[...] = pa
            pb = probs(2 * j + 1, False)
            acc_ref[...] += _dot(values(2 * j), p0_ref[...])
            p1_ref[...] = pb
            return carry

        n_pairs = n_full // 2
        lax.fori_loop(0, n_pairs, pair, 0)
        pending = jnp.where(n_pairs == 0, first_pending, 2 * n_pairs - 1)
        sw, vwt = win_scores()
        pw = jnp.exp2(sw).astype(BF16)
        acc_ref[...] += _dot(values(pending), p1_ref[...])
        accw_ref[...] = _dot(vwt, pw)

    @pl.when(jnp.logical_not(fast))
    def _():
        acc_ref[...] = jnp.zeros_like(acc_ref)

        def online(s, vt, m):
            m_new = jnp.maximum(m, jnp.max(s, axis=0, keepdims=True))
            acc_ref[...] = jnp.exp2(m - m_new) * acc_ref[...] + _dot(vt, jnp.exp2(s - m_new).astype(BF16))
            return m_new

        m = jnp.full((1, cols), NEG, F32)
        m = lax.fori_loop(0, n_full, lambda i, m: online(*sel_scores(i, False), m), m)
        online(*sel_scores(n_full, True), m)
        sw, vwt = win_scores()
        accw_ref[...] = _dot(vwt, jnp.exp2(sw - jnp.max(sw, axis=0, keepdims=True)).astype(BF16))

    a_s = acc_ref[...]
    a_w = accw_ref[...]

    a_s = a_s[:dk] * (1.0 / a_s[dk:dk + 1])
    a_w = a_w[:dk] * (1.0 / a_w[dk:dk + 1])
    gates = jax.nn.sigmoid(glt_ref[0, 0, :, rows])
    outs = []
    for h in range(NSA_HPG):
        sl = slice(h * tq, (h + 1) * tq)
        outs.append(gates[3 * h:3 * h + 1] * ac_ref[:, sl] + gates[3 * h + 1:3 * h + 2] * a_s[:, sl]
                    + gates[3 * h + 2:3 * h + 3] * a_w[:, sl])
    out = jnp.concatenate(outs, axis=0).T
    o_ref[rows, :] = (out * _silu(z_ref[rows, :].astype(F32))).astype(o_ref.dtype)


def _nsa_attn_kernel(*refs):
    for sub in range(Q_TILES_PER_STEP):
        _nsa_attn_tile(sub, *refs)


def _nsa_attention(fast, q, z, glt, ks_aug, vst, kw_pad, vwt, kc_pad, vct, overlap_t, batch, seq):
    cols = NSA_HPG * Q_TILE
    query = np.arange(cols)[None, :] % Q_TILE
    cmp_thr = jnp.asarray(np.arange(kc_pad.shape[2])[:, None] * CMP_STRIDE + (CMP_BLOCK - 1) - query, jnp.int32)
    key_minus_query = jnp.asarray(np.arange(SEL_KEYS)[:, None] - query, jnp.int32)
    n = batch * seq
    step_q = Q_TILES_PER_STEP * Q_TILE
    n_q = seq // step_q
    gw = NSA_HPG * NSA_DK

    def per_group(arr):
        nd = arr.ndim - 2
        return pl.BlockSpec((1, 1) + arr.shape[2:], lambda b, g, i: (b, g) + (0,) * nd)

    return pl.pallas_call(
        _nsa_attn_kernel,
        grid=(batch, NSA_G, n_q),
        in_specs=[
            pl.BlockSpec(memory_space=pltpu.SMEM),
            pl.BlockSpec((step_q, gw), lambda b, g, i: (b * n_q + i, g)),
            pl.BlockSpec((1, 1, glt.shape[2], step_q), lambda b, g, i: (b, g, 0, i)),
            pl.BlockSpec((step_q, gw), lambda b, g, i: (b * n_q + i, g)),
            per_group(ks_aug), per_group(vst), per_group(kw_pad), per_group(vwt),
            per_group(kc_pad), per_group(vct),
            _const_spec(overlap_t.shape), _const_spec(cmp_thr.shape), _const_spec(key_minus_query.shape),
        ],
        out_specs=pl.BlockSpec((step_q, gw), lambda b, g, i: (b * n_q + i, g)),
        out_shape=jax.ShapeDtypeStruct((n, NSA_HEADS * NSA_DK), BF16),
        scratch_shapes=([pltpu.VMEM((V_ROWS, cols), F32)] * 2 + [pltpu.VMEM((SEL_BLOCK, Q_TILE), F32)]
                        + [pltpu.VMEM((SEL_KEYS, cols), BF16)] * 2
                        + [pltpu.VMEM((NSA_DK, cols), F32), pltpu.VMEM((SEL_BLOCK, Q_TILE), F32)]),
        compiler_params=pltpu.CompilerParams(
            dimension_semantics=("arbitrary", "arbitrary", "arbitrary"),
            vmem_limit_bytes=VMEM_LIMIT),
        name="nsa_attention",
    )(fast, q, glt, z, ks_aug, vst, kw_pad, vwt, kc_pad, vct, overlap_t, cmp_thr, key_minus_query)


def _ret_kernel(q_ref, k_ref, v_ref, z_ref, dec_ref, qd_ref, kd_ref, cd_ref, o_ref, st_ref):
    @pl.when(pl.program_id(1) == 0)
    def _():
        st_ref[...] = jnp.zeros_like(st_ref)

    for h in range(RET_HEADS):
        q = q_ref[:, h * RET_DK:(h + 1) * RET_DK]
        k = k_ref[:, h * RET_DK:(h + 1) * RET_DK]
        v = v_ref[:, h * RET_DV:(h + 1) * RET_DV]
        att = _dot_nt(q, k) * dec_ref[h]
        state = st_ref[h]
        o = _dot(att.astype(BF16), v) + _dot(q, state.astype(BF16)) * qd_ref[h]
        kd_t = (k.astype(F32) * kd_ref[h]).T.astype(BF16)
        st_ref[h] = state * cd_ref[h] + _dot(kd_t, v)
        mu = jnp.mean(o, axis=-1, keepdims=True)
        d = o - mu
        var = jnp.mean(d * d, axis=-1, keepdims=True)
        z = z_ref[:, h * RET_DV:(h + 1) * RET_DV].astype(F32)
        o_ref[:, h * RET_DV:(h + 1) * RET_DV] = (d * lax.rsqrt(var + GN_EPS) * _silu(z)).astype(o_ref.dtype)


def _retention(pr, tables, batch, seq):
    n = batch * seq
    n_c = seq // RET_CHUNK
    dec, qd, kd, cd = tables
    qk_w = RET_HEADS * RET_DK
    v_w = RET_HEADS * RET_DV
    row = lambda b, c: b * n_c + c
    return pl.pallas_call(
        _ret_kernel,
        grid=(batch, n_c),
        in_specs=[
            pl.BlockSpec((RET_CHUNK, qk_w), lambda b, c: (row(b, c), 0)),
            pl.BlockSpec((RET_CHUNK, qk_w), lambda b, c: (row(b, c), 1)),
            pl.BlockSpec((RET_CHUNK, v_w), lambda b, c: (row(b, c), 1)),
            pl.BlockSpec((RET_CHUNK, v_w), lambda b, c: (row(b, c), 2)),
            _const_spec(dec.shape), _const_spec(qd.shape), _const_spec(kd.shape), _const_spec(cd.shape),
        ],
        out_specs=pl.BlockSpec((RET_CHUNK, v_w), lambda b, c: (row(b, c), 0)),
        out_shape=jax.ShapeDtypeStruct((n, v_w), BF16),
        scratch_shapes=[pltpu.VMEM((RET_HEADS, RET_DK, RET_DV), F32)],
        compiler_params=pltpu.CompilerParams(
            dimension_semantics=("arbitrary", "arbitrary"), vmem_limit_bytes=VMEM_LIMIT),
        name="retention",
    )(pr, pr, pr, pr, dec, qd, kd, cd)


def _overlap_matrix_t(n_cmp_pad):
    i = np.arange(n_cmp_pad)[None, :]
    j = np.arange(SEL_BLOCK)[:, None]
    ov = (i * CMP_STRIDE < (j + 1) * SEL_BLOCK) & (i * CMP_STRIDE + CMP_BLOCK > j * SEL_BLOCK)
    return jnp.asarray(ov, BF16)


def _rotary_tables(seq):
    half = RET_DK // 2
    inv = ROPE_BASE ** (-jnp.linspace(0.0, 1.0, half, dtype=F32))
    ang = jnp.arange(seq, dtype=F32)[:, None] * inv[None, :]
    return jnp.cos(ang), jnp.sin(ang)


def _retention_tables():
    c = RET_CHUNK
    log_g = jnp.log(1.0 - 2.0 ** (-5.0 - jnp.arange(RET_HEADS, dtype=F32)))
    ix = jnp.arange(c, dtype=F32)
    diff = ix[:, None] - ix[None, :]
    dec = jnp.where(diff >= 0, jnp.exp(log_g[:, None, None] * jnp.maximum(diff, 0.0)), 0.0)
    qd = jnp.exp(log_g[:, None] * (ix + 1.0))[:, :, None]
    kd = jnp.exp(log_g[:, None] * (c - 1.0 - ix))[:, :, None]
    cd = jnp.broadcast_to(jnp.exp(log_g * c)[:, None, None], (RET_HEADS, 1, RET_DV))
    return dec, qd, kd, cd


def _nsa_layer(x2d, p_all, layer, batch, seq, norm_g, w_in, q_g, kc_g, ks_g, kw_g, pos_k, pos_v,
               ck_w1, ck_w2, cv_w1, cv_w2, w_out, ple_w, ple_gate_w):
    qw = NSA_HEADS * NSA_DK
    kvw = NSA_G * NSA_DK
    n_gate = 3 * NSA_HPG
    sizes = [qw] + [kvw] * 6 + [3 * NSA_HEADS, qw]
    splits = np.concatenate([[0], np.cumsum(sizes)])
    names = ["q", "kc", "vc", "ks", "vs", "kw", "vw"]
    offs = {nm: int(splits[i]) for i, nm in enumerate(names)}
    front = int(splits[7])
    offs["z"], offs["gl"] = front, front + qw
    wgl = w_in[:, splits[7]:splits[8]].reshape(D_MODEL, NSA_G, n_gate)
    wgl = jnp.pad(wgl, ((0, 0), (0, 0), (0, GATE_ROWS - n_gate))).reshape(D_MODEL, NSA_G * GATE_ROWS)
    wgl = jnp.pad(wgl, ((0, 0), (0, LANES - NSA_G * GATE_ROWS)))
    w = jnp.concatenate([w_in[:, :front], w_in[:, splits[8]:], wgl], axis=1).astype(BF16)
    zeros = lambda width: jnp.zeros((width,), F32)
    head_gain = jnp.concatenate([
        jnp.tile(q_g, NSA_HEADS) * (NSA_DK ** -0.5 * LOG2E), zeros(2 * kvw),
        jnp.tile(ks_g, NSA_G), zeros(kvw), jnp.tile(kw_g, NSA_G), zeros(w.shape[1] - offs["vw"])]).reshape(1, -1)

    q_norm = LOG2E * jnp.max(jnp.abs(q_g)) * 1.02
    bound_sel = (q_norm * NSA_DK ** 0.5 * jnp.max(jnp.abs(ks_g))).astype(BF16).astype(F32) * 1.01
    bound_win = (q_norm * NSA_DK ** 0.5 * jnp.max(jnp.abs(kw_g))).astype(BF16).astype(F32) * 1.01
    bound_cmp = (q_norm * NSA_DK ** 0.5 * jnp.max(jnp.abs(kc_g))).astype(BF16).astype(F32) * 1.01
    bound_max = jnp.maximum(jnp.maximum(bound_sel, bound_win), bound_cmp)
    fast = (bound_max <= MAX_SAFE_BOUND).astype(jnp.int32).reshape(1)

    blk_id = np.arange(seq)[:, None] // SEL_BLOCK
    upper = np.arange(LANES)[None, :] - NSA_DK
    onehot = jnp.asarray((blk_id == upper) & (upper > 0), F32)
    lane64 = jnp.asarray(upper == 0)
    aug_sel = jnp.where(lane64, -bound_sel, onehot).astype(BF16)
    aug_win = jnp.where(lane64, -bound_win, jnp.zeros((seq, LANES), F32)).astype(BF16)
    pad_rows = jnp.where(lane64, SEL_MASK_BIAS, jnp.zeros((WINDOW, LANES), F32)).astype(BF16)

    q, z, kc_sub, vc_sub, ks_aug, kw_pad, vst, vwt, glt = _nsa_proj(
        x2d, norm_g, w, head_gain, aug_sel, aug_win, pad_rows, offs, batch, seq)

    n_sub = seq // CMP_STRIDE
    half = CMP_STRIDE * NSA_DK
    w1cat = lambda w1: jnp.concatenate([w1[:half], w1[half:]], axis=1).astype(BF16)
    pos_flat = lambda pos: jnp.pad(pos.reshape(1, CMP_BLOCK * NSA_DK), ((0, 7), (0, 0))).astype(BF16)
    w2pad = lambda w2: jnp.pad(w2, ((0, 0), (0, LANES - NSA_DK))).astype(BF16)
    gk = jnp.pad(kc_g, (0, LANES - NSA_DK)).reshape(1, LANES)
    k_shift = jnp.where(lane64, -bound_cmp, 0.0)
    kc_pad, vct = _compress(kc_sub, vc_sub, w1cat(ck_w1), w1cat(cv_w1),
                            pos_flat(pos_k), pos_flat(pos_v), w2pad(ck_w2), w2pad(cv_w2), gk, k_shift)

    a = _nsa_attention(fast, q, z, glt, ks_aug, vst, kw_pad, vwt, kc_pad, vct,
                       _overlap_matrix_t(n_sub), batch, seq)
    return _out_proj(a, w_out, x2d, p_all, layer, ple_gate_w, ple_w)


def _ret_layer(x2d, p_all, layer, batch, seq, norm_g, w_in, w_out, ple_w, ple_gate_w):
    cos, sin = _rotary_tables(seq)
    pr = _ret_proj(x2d, norm_g, w_in.astype(BF16), cos, sin, seq)
    a = _retention(pr, _retention_tables(), batch, seq)
    return _out_proj(a, w_out, x2d, p_all, layer, ple_gate_w, ple_w)


def kernel(x, p, norm_g, nsa_w_in, nsa_q_g, nsa_kc_g, nsa_ks_g, nsa_kw_g, nsa_cmp_pos_k, nsa_cmp_pos_v, nsa_cmp_k_w1, nsa_cmp_k_w2, nsa_cmp_v_w1, nsa_cmp_v_w2, nsa_w_out, ret_w_in, ret_w_out, ple_w, ple_gate_w):
    batch, seq, d_model = x.shape
    depth = p.shape[0]
    n = batch * seq
    x2d = x.reshape(n, d_model)
    p_all = p.reshape(depth, n, PLE_DIM)
    for i in range(depth):
        j = i // 2
        if i % 2 == 0:
            x2d = _nsa_layer(x2d, p_all, i, batch, seq, norm_g[i], nsa_w_in[j], nsa_q_g[j], nsa_kc_g[j],
                             nsa_ks_g[j], nsa_kw_g[j], nsa_cmp_pos_k[j], nsa_cmp_pos_v[j],
                             nsa_cmp_k_w1[j], nsa_cmp_k_w2[j], nsa_cmp_v_w1[j], nsa_cmp_v_w2[j],
                             nsa_w_out[j], ple_w[i], ple_gate_w[i])
        else:
            x2d = _ret_layer(x2d, p_all, i, batch, seq, norm_g[i], ret_w_in[j], ret_w_out[j],
                             ple_w[i], ple_gate_w[i])
    return x2d.reshape(batch, seq, d_model)
```
